```python
import jax, jax.numpy as jnp
from jax import lax
import numpy as np

D_MODEL = 1024
BATCH = 8
SEQ = 2048
DEPTH = 1
DEC_BATCH = 128
DEC_SEQ = 8
PAST_LEN = 8192
PAGE_SIZE = 128

HEAD_DIM = 64
FOX_HEADS = 8
FOX_KV_HEADS = 2
FOX_REP = FOX_HEADS // FOX_KV_HEADS
NSA_HEADS = 8
NSA_GROUPS = 2
NSA_REP = NSA_HEADS // NSA_GROUPS
CMP_STRIDE = 16
CMP_BLOCK = 2 * CMP_STRIDE
CMP_HIDDEN = 2 * HEAD_DIM
SEL_BLOCK = 64
N_SELECT = 16
WINDOW = 512
Q_BLOCK = 128
D_FF = 4 * D_MODEL
RMS_EPS = 1e-6
FORCE_BONUS = 1.0e4
FOX_QW = FOX_HEADS * HEAD_DIM
FOX_KVW = FOX_KV_HEADS * HEAD_DIM
NSA_QW = NSA_HEADS * HEAD_DIM
NSA_KVW = NSA_GROUPS * HEAD_DIM
IN_SPLITS = (FOX_QW, FOX_KVW, FOX_KVW, FOX_HEADS, NSA_QW, 6 * NSA_KVW, 3 * NSA_HEADS, 2 * D_MODEL)
IN_COLS = sum(IN_SPLITS)
IN_OFFSETS = tuple(int(o) for o in np.cumsum(IN_SPLITS)[:-1])

kernel_name = 'fox_nsa_gated_hybrid_step'


def rms_norm(x, g):
    xf = x.astype(jnp.float32)
    y = xf * lax.rsqrt(jnp.mean(xf * xf, axis=-1, keepdims=True) + RMS_EPS)
    return (y * g.astype(jnp.float32)).astype(x.dtype)


def alibi_slopes(n):
    return jnp.asarray(2.0 ** (-8.0 * np.arange(1, n + 1) / n), jnp.float32)


def masked_softmax(s, mask):
    s = jnp.where(mask, s, -jnp.inf)
    mx = jnp.max(s, axis=-1, keepdims=True)
    p = jnp.exp(s - jnp.where(jnp.isfinite(mx), mx, 0.0))
    den = jnp.sum(p, axis=-1, keepdims=True)
    return p / jnp.where(den > 0, den, 1.0)


def gather_pages(pool, page_table):
    g = pool[page_table]
    return g.reshape((g.shape[0], g.shape[1] * g.shape[2]) + g.shape[3:])


def block_importance(p_cmp, n_sel):
    r = SEL_BLOCK // CMP_STRIDE
    n_cmp = p_cmp.shape[-1]
    pp = jnp.pad(p_cmp, [(0, 0)] * (p_cmp.ndim - 1) + [(1, r * n_sel - n_cmp)])
    main = pp[..., :r * n_sel].reshape(p_cmp.shape[:-1] + (n_sel, r))
    last = pp[..., r:r * n_sel + 1:r]
    return 0.5 * main[..., 0] + jnp.sum(main[..., 1:], axis=-1) + 0.5 * last


def compress_blocks(x, pos, w1, w2):
    b, n = x.shape[:2]
    n_chunk = n // CMP_STRIDE
    xc = x[:, :n_chunk * CMP_STRIDE].reshape(b, n_chunk, CMP_STRIDE, NSA_GROUPS, HEAD_DIM)
    pos = pos.reshape(2, CMP_STRIDE, 1, HEAD_DIM)
    w1 = w1.reshape(2, CMP_STRIDE, HEAD_DIM, CMP_HIDDEN)
    lead = jnp.einsum('bncgd,cdh->bngh', xc + pos[0], w1[0])
    trail = jnp.einsum('bncgd,cdh->bngh', xc + pos[1], w1[1])
    return jnp.einsum('bngh,hd->bngd', jax.nn.silu(lead[:, :-1] + trail[:, 1:]), w2)


def compressed_kv(k_raw, v_raw, lp):
    kc = rms_norm(compress_blocks(k_raw, lp['cmp_pos_k'], lp['cmp_w1_k'], lp['cmp_w2_k']), lp['nsa_kn_cmp_g'])
    vc = compress_blocks(v_raw, lp['cmp_pos_v'], lp['cmp_w1_v'], lp['cmp_w2_v'])
    c_end = jnp.arange(kc.shape[1]) * CMP_STRIDE + (CMP_BLOCK - 1)
    return kc, vc, c_end


def to_sel_blocks(x, n_sel):
    b, n = x.shape[:2]
    x = jnp.pad(x, ((0, 0), (0, n_sel * SEL_BLOCK - n), (0, 0), (0, 0)))
    return x.reshape(b, n_sel, SEL_BLOCK, NSA_GROUPS, HEAD_DIM).transpose(0, 3, 1, 2, 4)


def fox_attend(q, cq, q_pos, k, v, ck, k_pos):
    s = jnp.einsum('btgrd,bsgd->bgrts', q, k).astype(jnp.float32) * HEAD_DIM ** -0.5
    bias = jnp.moveaxis(cq, 1, -1)[..., :, None] - jnp.moveaxis(ck, 1, -1)[..., None, :]
    p = masked_softmax(s + bias, k_pos[None, :] <= q_pos[:, None])
    return jnp.einsum('bgrts,bsgd->btgrd', p.astype(v.dtype), v)


def nsa_attend(q, q_pos, gates, kc, vc, c_end, ks_blk, vs_blk, kw, vw, w_pos):
    b, t_len = q.shape[:2]
    scale = HEAD_DIM ** -0.5
    m = alibi_slopes(NSA_HEADS).reshape(NSA_GROUPS, NSA_REP)[None, :, :, None, None]
    dc = q_pos[:, None] - c_end[None, :]
    s = jnp.einsum('btgrd,bngd->bgrtn', q, kc).astype(jnp.float32) * scale - m * dc.astype(jnp.float32)
    p_c = masked_softmax(s, dc >= 0)
    o_c = jnp.einsum('bgrtn,bngd->btgrd', p_c.astype(vc.dtype), vc)
    n_sel = ks_blk.shape[2]
    imp = block_importance(jnp.sum(p_c, axis=2), n_sel)
    blk = jnp.arange(n_sel)[None, :]
    cur = (q_pos // SEL_BLOCK)[:, None]
    forced = (blk == 0) | (blk == cur) | (blk == cur - 1)
    score = jnp.where(blk <= cur, imp + jnp.where(forced, FORCE_BONUS, 0.0), -jnp.inf)
    _, idx = lax.top_k(score, min(N_SELECT, n_sel))
    n_k = idx.shape[-1]
    take = jax.vmap(jax.vmap(lambda blocks, i: blocks[i]))
    ksel = take(ks_blk, idx)
    vsel = take(vs_blk, idx)
    ds = q_pos[None, None, :, None, None] - (idx[..., None] * SEL_BLOCK + jnp.arange(SEL_BLOCK))
    s = jnp.einsum('btgrd,bgtjsd->bgrtjs', q, ksel).astype(jnp.float32) * scale - m[..., None] * ds[:, :, None].astype(jnp.float32)
    p_s = masked_softmax(s.reshape(b, NSA_GROUPS, NSA_REP, t_len, n_k * SEL_BLOCK),
                         (ds >= 0)[:, :, None].reshape(b, NSA_GROUPS, 1, t_len, n_k * SEL_BLOCK))
    p_s = p_s.reshape(b, NSA_GROUPS, NSA_REP, t_len, n_k, SEL_BLOCK)
    o_s = jnp.einsum('bgrtjs,bgtjsd->btgrd', p_s.astype(vsel.dtype), vsel)
    dw = q_pos[:, None] - w_pos[None, :]
    s = jnp.einsum('btgrd,bwgd->bgrtw', q, kw).astype(jnp.float32) * scale - m * dw.astype(jnp.float32)
    p_w = masked_softmax(s, (dw >= 0) & (dw < WINDOW) & (w_pos >= 0)[None, :])
    o_w = jnp.einsum('bgrtw,bwgd->btgrd', p_w.astype(vw.dtype), vw)
    return gates[..., 0:1] * o_c + gates[..., 1:2] * o_s + gates[..., 2:3] * o_w


def pre_mixer(x, c, lp):
    b, t_len, _ = x.shape
    mod = jnp.einsum('bd,de->be', jax.nn.silu(c), lp['w_ada']) + lp['b_ada']
    sh1, sc1, gt1, sh2, sc2, gt2 = jnp.split(mod[:, None, :], 6, axis=-1)
    h = rms_norm(x, lp['norm1_g']) * (1.0 + sc1) + sh1
    z = jnp.einsum('btd,dc->btc', h, lp['w_in'])
    fq, fk, fv, ff, nq, nkv, ng, mg = jnp.split(z, IN_OFFSETS, axis=-1)
    fox_k = rms_norm(fk.reshape(b, t_len, FOX_KV_HEADS, HEAD_DIM), lp['fox_kn_g'])
    fox_v = fv.reshape(b, t_len, FOX_KV_HEADS, HEAD_DIM)
    nkv = nkv.reshape(b, t_len, 6, NSA_GROUPS, HEAD_DIM)
    slc_k = rms_norm(nkv[:, :, 2], lp['nsa_kn_slc_g'])
    win_k = rms_norm(nkv[:, :, 4], lp['nsa_kn_win_g'])
    g_fox, g_nsa = jnp.split(jax.nn.sigmoid(mg), 2, axis=-1)
    return dict(
        sc2=sc2, sh2=sh2, gt1=gt1, gt2=gt2, g_fox=g_fox, g_nsa=g_nsa,
        fox_q=rms_norm(fq.reshape(b, t_len, FOX_KV_HEADS, FOX_REP, HEAD_DIM), lp['fox_qn_g']),
        fox_k=fox_k, fox_v=fox_v,
        fox_rows=jnp.stack([fox_k, fox_v], axis=2),
        logf=jax.nn.log_sigmoid((ff + lp['b_fox_f']).astype(jnp.float32)),
        nsa_q=rms_norm(nq.reshape(b, t_len, NSA_GROUPS, NSA_REP, HEAD_DIM), lp['nsa_qn_g']),
        nsa_gate=jax.nn.sigmoid(ng.reshape(b, t_len, NSA_GROUPS, NSA_REP, 3)),
        nsa_rows=jnp.stack([nkv[:, :, 0], nkv[:, :, 1], slc_k, nkv[:, :, 3]], axis=2),
        win_rows=jnp.stack([win_k, nkv[:, :, 5]], axis=2),
    )


def post_mixer(x, pm, o_fox, o_nsa, lp):
    b, t_len, _ = x.shape
    br_fox = jnp.einsum('btk,kd->btd', o_fox.reshape(b, t_len, FOX_QW), lp['w_br_fox'])
    br_nsa = jnp.einsum('btk,kd->btd', o_nsa.reshape(b, t_len, NSA_QW), lp['w_br_nsa'])
    mix = pm['g_fox'] * br_fox + pm['g_nsa'] * br_nsa
    x = x + pm['gt1'] * jnp.einsum('btd,de->bte', mix, lp['w_out'])
    h = rms_norm(x, lp['norm2_g']) * (1.0 + pm['sc2']) + pm['sh2']
    u = jax.nn.relu(jnp.einsum('btd,df->btf', h, lp['w_up']))
    return x + pm['gt2'] * jnp.einsum('btf,fd->btd', u * u, lp['w_down'])


def layer_prompt(x, c, lp, win_len):
    b, t_len, _ = x.shape
    pm = pre_mixer(x, c, lp)
    pos = jnp.arange(t_len)
    nb = t_len // Q_BLOCK
    blocks = lambda a: a.reshape((b, nb, Q_BLOCK) + a.shape[2:]).swapaxes(0, 1)
    cf = jnp.cumsum(pm['logf'], axis=1).reshape(b, t_len, FOX_KV_HEADS, FOX_REP)
    fk, fv = pm['fox_k'], pm['fox_v']
    o_fox = lax.map(lambda a: fox_attend(a[0], a[1], a[2], fk, fv, cf, pos),
                    (blocks(pm['fox_q']), blocks(cf), pos.reshape(nb, Q_BLOCK)))
    o_fox = o_fox.swapaxes(0, 1).reshape(b, t_len, FOX_KV_HEADS, FOX_REP, HEAD_DIM)
    rows = pm['nsa_rows']
    kc, vc, c_end = compressed_kv(rows[:, :, 0], rows[:, :, 1], lp)
    n_sel = -(-t_len // SEL_BLOCK)
    ks_blk = to_sel_blocks(rows[:, :, 2], n_sel)
    vs_blk = to_sel_blocks(rows[:, :, 3], n_sel)
    w_pad = jnp.pad(pm['win_rows'], ((0, 0), (WINDOW, 0), (0, 0), (0, 0), (0, 0)))

    def nsa_block(a):
        qb, gb, i = a
        start = i * Q_BLOCK
        w = lax.dynamic_slice_in_dim(w_pad, start, WINDOW + Q_BLOCK, axis=1)
        q_pos = start + jnp.arange(Q_BLOCK)
        w_pos = start - WINDOW + jnp.arange(WINDOW + Q_BLOCK)
        return nsa_attend(qb, q_pos, gb, kc, vc, c_end, ks_blk, vs_blk, w[:, :, 0], w[:, :, 1], w_pos)

    o_nsa = lax.map(nsa_block, (blocks(pm['nsa_q']), blocks(pm['nsa_gate']), jnp.arange(nb)))
    o_nsa = o_nsa.swapaxes(0, 1).reshape(b, t_len, NSA_GROUPS, NSA_REP, HEAD_DIM)
    y = post_mixer(x, pm, o_fox, o_nsa, lp)
    win_new = jnp.pad(pm['win_rows'], ((0, 0), (max(win_len - t_len, 0), 0), (0, 0), (0, 0), (0, 0)))[:, -win_len:]
    return y, pm['fox_rows'], pm['logf'].astype(x.dtype), rows, win_new


def layer_sample(x, c, fox_pool, logf_pool, nsa_pool, win_buf, page_table, lp):
    b, t_len, _ = x.shape
    pm = pre_mixer(x, c, lp)
    fox_all = jnp.concatenate([gather_pages(fox_pool, page_table), pm['fox_rows']], axis=1)
    past = fox_all.shape[1] - t_len
    n_all = past + t_len
    lf_all = jnp.concatenate([gather_pages(logf_pool, page_table).astype(jnp.float32), pm['logf']], axis=1)
    cf = jnp.cumsum(lf_all, axis=1).reshape(b, n_all, FOX_KV_HEADS, FOX_REP)
    pos = jnp.arange(n_all)
    o_fox = fox_attend(pm['fox_q'], cf[:, past:], pos[past:], fox_all[:, :, 0], fox_all[:, :, 1], cf, pos)
    nsa_all = jnp.concatenate([gather_pages(nsa_pool, page_table), pm['nsa_rows']], axis=1)
    kc, vc, c_end = compressed_kv(nsa_all[:, :, 0], nsa_all[:, :, 1], lp)
    n_sel = -(-n_all // SEL_BLOCK)
    ks_blk = to_sel_blocks(nsa_all[:, :, 2], n_sel)
    vs_blk = to_sel_blocks(nsa_all[:, :, 3], n_sel)
    win_len = win_buf.shape[1]
    win_all = jnp.concatenate([win_buf, pm['win_rows']], axis=1)
    w_pos = past - win_len + jnp.arange(win_len + t_len)
    o_nsa = nsa_attend(pm['nsa_q'], pos[past:], pm['nsa_gate'], kc, vc, c_end, ks_blk, vs_blk,
                       win_all[:, :, 0], win_all[:, :, 1], w_pos)
    y = post_mixer(x, pm, o_fox, o_nsa, lp)
    return y, pm['fox_rows'], pm['logf'].astype(x.dtype), pm['nsa_rows'], win_all[:, t_len:]


def setup_inputs(seed: int = 0) -> dict:
    key = jax.random.key(seed)
    keys = iter(jax.random.split(key, 64))
    nrm = lambda shape, scale=1.0: scale * jax.random.normal(next(keys), shape, jnp.float32)
    gain = lambda n: 1.0 + nrm((DEPTH, n), 0.05)
    n_pages = PAST_LEN // PAGE_SIZE
    n_phys = (DEC_BATCH * n_pages * 5) // 4
    win_len = min(WINDOW, PAST_LEN)
    page_table = jax.random.permutation(next(keys), n_phys)[:DEC_BATCH * n_pages].reshape(DEC_BATCH, n_pages).astype(jnp.int32)
    return {
        'x_prompt': nrm((BATCH, SEQ, D_MODEL)),
        'x_sample': nrm((DEC_BATCH, DEC_SEQ, D_MODEL)),
        'cache_fox_kv': nrm((DEPTH, n_phys, PAGE_SIZE, 2, FOX_KV_HEADS, HEAD_DIM)),
        'cache_fox_logf': jax.nn.log_sigmoid(nrm((DEPTH, n_phys, PAGE_SIZE, FOX_HEADS)) + 4.5),
        'cache_nsa_kv': nrm((DEPTH, n_phys, PAGE_SIZE, 4, NSA_GROUPS, HEAD_DIM)),
        'state_win_kv': nrm((DEPTH, DEC_BATCH, win_len, 2, NSA_GROUPS, HEAD_DIM)),
        'page_table': page_table,
        'c_prompt': nrm((BATCH, D_MODEL)),
        'c_sample': nrm((DEC_BATCH, D_MODEL)),
        'norm1_g': gain(D_MODEL),
        'norm2_g': gain(D_MODEL),
        'w_ada': nrm((DEPTH, D_MODEL, 6 * D_MODEL), 0.5 * D_MODEL ** -0.5),
        'b_ada': nrm((DEPTH, 6 * D_MODEL), 0.02),
        'w_in': nrm((DEPTH, D_MODEL, IN_COLS), D_MODEL ** -0.5),
        'b_fox_f': jnp.linspace(2.0, 7.0, FOX_HEADS)[None, :] + nrm((DEPTH, FOX_HEADS), 0.1),
        'fox_qn_g': gain(HEAD_DIM),
        'fox_kn_g': gain(HEAD_DIM),
        'nsa_qn_g': gain(HEAD_DIM),
        'nsa_kn_cmp_g': gain(HEAD_DIM),
        'nsa_kn_slc_g': gain(HEAD_DIM),
        'nsa_kn_win_g': gain(HEAD_DIM),
        'cmp_pos_k': nrm((DEPTH, CMP_BLOCK, HEAD_DIM), 0.1),
        'cmp_w1_k': nrm((DEPTH, CMP_BLOCK * HEAD_DIM, CMP_HIDDEN), (CMP_BLOCK * HEAD_DIM) ** -0.5),
        'cmp_w2_k': nrm((DEPTH, CMP_HIDDEN, HEAD_DIM), CMP_HIDDEN ** -0.5),
        'cmp_pos_v': nrm((DEPTH, CMP_BLOCK, HEAD_DIM), 0.1),
        'cmp_w1_v': nrm((DEPTH, CMP_BLOCK * HEAD_DIM, CMP_HIDDEN), (CMP_BLOCK * HEAD_DIM) ** -0.5),
        'cmp_w2_v': nrm((DEPTH, CMP_HIDDEN, HEAD_DIM), CMP_HIDDEN ** -0.5),
        'w_br_fox': nrm((DEPTH, FOX_QW, D_MODEL), FOX_QW ** -0.5),
        'w_br_nsa': nrm((DEPTH, NSA_QW, D_MODEL), NSA_QW ** -0.5),
        'w_out': nrm((DEPTH, D_MODEL, D_MODEL), D_MODEL ** -0.5),
        'w_up': nrm((DEPTH, D_MODEL, D_FF), D_MODEL ** -0.5),
        'w_down': nrm((DEPTH, D_FF, D_MODEL), D_FF ** -0.5),
    }


def reference(x_prompt, x_sample, cache_fox_kv, cache_fox_logf, cache_nsa_kv, state_win_kv, page_table,
              c_prompt, c_sample, norm1_g, norm2_g, w_ada, b_ada, w_in, b_fox_f, fox_qn_g, fox_kn_g,
              nsa_qn_g, nsa_kn_cmp_g, nsa_kn_slc_g, nsa_kn_win_g, cmp_pos_k, cmp_w1_k, cmp_w2_k,
              cmp_pos_v, cmp_w1_v, cmp_w2_v, w_br_fox, w_br_nsa, w_out, w_up, w_down):
    yp, ys = x_prompt, x_sample
    p_fox, p_lf, p_nsa, p_win = [], [], [], []
    s_fox, s_lf, s_nsa, s_win = [], [], [], []
    for l in range(DEPTH):
        lp = dict(norm1_g=norm1_g[l], norm2_g=norm2_g[l], w_ada=w_ada[l], b_ada=b_ada[l], w_in=w_in[l],
                  b_fox_f=b_fox_f[l], fox_qn_g=fox_qn_g[l], fox_kn_g=fox_kn_g[l], nsa_qn_g=nsa_qn_g[l],
                  nsa_kn_cmp_g=nsa_kn_cmp_g[l], nsa_kn_slc_g=nsa_kn_slc_g[l], nsa_kn_win_g=nsa_kn_win_g[l],
                  cmp_pos_k=cmp_pos_k[l], cmp_w1_k=cmp_w1_k[l], cmp_w2_k=cmp_w2_k[l],
                  cmp_pos_v=cmp_pos_v[l], cmp_w1_v=cmp_w1_v[l], cmp_w2_v=cmp_w2_v[l],
                  w_br_fox=w_br_fox[l], w_br_nsa=w_br_nsa[l], w_out=w_out[l], w_up=w_up[l], w_down=w_down[l])
        yp, a_fox, a_lf, a_nsa, a_win = layer_prompt(yp, c_prompt, lp, state_win_kv.shape[2])
        ys, b_fox, b_lf, b_nsa, b_win = layer_sample(ys, c_sample, cache_fox_kv[l], cache_fox_logf[l],
                                                     cache_nsa_kv[l], state_win_kv[l], page_table, lp)
        p_fox.append(a_fox); p_lf.append(a_lf); p_nsa.append(a_nsa); p_win.append(a_win)
        s_fox.append(b_fox); s_lf.append(b_lf); s_nsa.append(b_nsa); s_win.append(b_win)
    return (yp, ys, jnp.stack(p_fox), jnp.stack(p_lf), jnp.stack(p_nsa), jnp.stack(p_win),
            jnp.stack(s_fox), jnp.stack(s_lf), jnp.stack(s_nsa), jnp.stack(s_win))
```

```python
import functools

import numpy as np
import jax
import jax.numpy as jnp
from jax import lax
from jax.experimental import pallas as pl
from jax.experimental.pallas import tpu as pltpu

F32 = jnp.float32
BF16 = jnp.bfloat16

D_MODEL = 1024
HEAD_DIM = 64
N_HEADS = 8
N_GROUPS = 2
N_REP = N_HEADS // N_GROUPS
PAGE_SIZE = 128
CMP_STRIDE = 16
CMP_HIDDEN = 2 * HEAD_DIM
SEL_BLOCK = 64
N_SELECT = 16
WINDOW = 512
D_FF = 4 * D_MODEL
RMS_EPS = 1e-6
FORCE_BONUS = 1.0e4
LANES = 128
NEG = -1e30
MASK_BIG = 2.0 ** 100
GATE_LANE0 = 0
LOGF_LANE0 = 24
VMEM_LIMIT = 56 * 1024 * 1024


def _dot(a, b):
    return jnp.dot(a, b, preferred_element_type=F32)


def _dot_nt(a, b):
    return lax.dot_general(a, b, (((1,), (1,)), ((), ())), preferred_element_type=F32)


def _split3(x):
    x1 = x.astype(BF16)
    r = x - x1.astype(F32)
    x2 = r.astype(BF16)
    x3 = (r - x2.astype(F32)).astype(BF16)
    return x1, x2, x3


def _dot_exact_r(x, m):
    a, b, c = _split3(x)
    return _dot(a, m) + _dot(b, m) + _dot(c, m)


def _dot_exact_l(m, x):
    a, b, c = _split3(x)
    return _dot(m, a) + _dot(m, b) + _dot(m, c)


def _sigmoid(x):
    return 1.0 / (1.0 + jnp.exp(-x))


def _head_rms(zc, bd):
    a = zc * zc
    a1 = a.astype(BF16)
    a2 = (a - a1.astype(F32)).astype(BF16)
    ss = _dot(a1, bd) + _dot(a2, bd)
    return zc * lax.rsqrt(ss * (1.0 / HEAD_DIM) + RMS_EPS)


def _cparams(sem, vmem=VMEM_LIMIT):
    return pltpu.CompilerParams(dimension_semantics=sem, vmem_limit_bytes=vmem)


def _ada_kernel(c_ref, w_ref, b_ref, o_ref):
    c = c_ref[...]
    a = c * _sigmoid(c)
    o_ref[...] = _dot(a.astype(BF16), w_ref[...].astype(BF16)) + b_ref[...]


def _ada_call(c, w_ada, b_ada):
    r, d = c.shape
    n = w_ada.shape[1]
    tn = 1024
    return pl.pallas_call(
        _ada_kernel,
        grid=(n // tn,),
        in_specs=[pl.BlockSpec((r, d), lambda j: (0, 0)),
                  pl.BlockSpec((d, tn), lambda j: (0, j)),
                  pl.BlockSpec((1, tn), lambda j: (0, j))],
        out_specs=pl.BlockSpec((r, tn), lambda j: (0, j)),
        out_shape=jax.ShapeDtypeStruct((r, n), F32),
        compiler_params=_cparams(("arbitrary",)),
        name="ada_mod",
    )(c, w_ada, b_ada.reshape(1, n))


_C_FQ, _C_FK, _C_FV, _C_NQ, _C_NKV, _C_SMALL, _C_END = 0, 512, 640, 768, 1280, 2048, 2176


def _pre_kernel(x_ref, sc_ref, sh_ref, g1_ref, w_ref, bd_ref, tri_ref, gains_ref, bff_ref,
                qf_ref, frow_ref, qn_ref, nrow_ref, wrow_ref, kvb_ref, gate_ref, lf_ref,
                *rest, do_cum):
    nb, tt, d = x_ref.shape
    tm = nb * tt
    x = x_ref[...]
    ms = jnp.mean(x * x, axis=-1, keepdims=True)
    h = x * lax.rsqrt(ms + RMS_EPS) * g1_ref[...] * (1.0 + sc_ref[...]) + sh_ref[...]
    h = h.reshape(tm, d).astype(BF16)
    bd = bd_ref[...]

    def proj(c0):
        return _dot(h, w_ref[:, c0:c0 + LANES])

    scale = HEAD_DIM ** -0.5
    for r in range(N_REP):
        qf_ref[:, r * LANES:(r + 1) * LANES] = (
            _head_rms(proj(_C_FQ + r * LANES), bd) * (gains_ref[0:1, :] * scale)).astype(BF16)
        qn_ref[:, r * LANES:(r + 1) * LANES] = (
            _head_rms(proj(_C_NQ + r * LANES), bd) * (gains_ref[2:3, :] * scale)).astype(BF16)

    fk = _head_rms(proj(_C_FK), bd) * gains_ref[1:2, :]
    fv = proj(_C_FV)
    frow_ref[:, 0:LANES] = fk
    frow_ref[:, LANES:2 * LANES] = fv
    kvb_ref[:, 4 * LANES:5 * LANES] = fk.astype(BF16)
    kvb_ref[:, 5 * LANES:6 * LANES] = fv.astype(BF16)

    ck = proj(_C_NKV)
    cv = proj(_C_NKV + LANES)
    sk = _head_rms(proj(_C_NKV + 2 * LANES), bd) * gains_ref[3:4, :]
    sv = proj(_C_NKV + 3 * LANES)
    wk = _head_rms(proj(_C_NKV + 4 * LANES), bd) * gains_ref[4:5, :]
    wv = proj(_C_NKV + 5 * LANES)
    nrow_ref[:, 0:LANES] = ck
    nrow_ref[:, LANES:2 * LANES] = cv
    nrow_ref[:, 2 * LANES:3 * LANES] = sk
    nrow_ref[:, 3 * LANES:4 * LANES] = sv
    wrow_ref[:, 0:LANES] = wk
    wrow_ref[:, LANES:2 * LANES] = wv
    kvb_ref[:, 0:LANES] = sk.astype(BF16)
    kvb_ref[:, LANES:2 * LANES] = sv.astype(BF16)
    kvb_ref[:, 2 * LANES:3 * LANES] = wk.astype(BF16)
    kvb_ref[:, 3 * LANES:4 * LANES] = wv.astype(BF16)

    zl = proj(_C_SMALL)
    gate_ref[...] = _sigmoid(zl)
    xl = zl + bff_ref[...]
    lf = jnp.minimum(xl, 0.0) - jnp.log1p(jnp.exp(-jnp.abs(xl)))
    lf_ref[...] = lf

    if do_cum:
        c_ref, ct_ref, carry_ref = rest

        @pl.when(pl.program_id(1) == 0)
        def _():
            carry_ref[...] = jnp.zeros_like(carry_ref)

        c = _dot_exact_l(tri_ref[...], lf) + carry_ref[0:1, :]
        c_ref[...] = c
        carry_ref[...] = jnp.broadcast_to(c[tm - 1:tm, :], carry_ref.shape)
        ct_ref[...] = c.T[LOGF_LANE0:LOGF_LANE0 + N_HEADS, :]


def _pre_call(x3, mod3, g1, w_p, bd, tri, gains, bff, nb, tt, do_cum):
    NB, TT, d = x3.shape
    tm = nb * tt
    n = NB * TT
    gi, gj = NB // nb, TT // tt
    tok = lambda i, j: (i * gj + j, 0)
    full = lambda i, j: (0, 0)
    in_specs = [
        pl.BlockSpec((nb, tt, d), lambda i, j: (i, j, 0)),
        pl.BlockSpec((nb, 1, d), lambda i, j: (i, 0, 1)),
        pl.BlockSpec((nb, 1, d), lambda i, j: (i, 0, 0)),
        pl.BlockSpec((1, d), full),
        pl.BlockSpec(w_p.shape, full),
        pl.BlockSpec(bd.shape, full),
        pl.BlockSpec(tri.shape, full),
        pl.BlockSpec(gains.shape, full),
        pl.BlockSpec(bff.shape, full),
    ]
    out_shape = [
        jax.ShapeDtypeStruct((n, 512), BF16),
        jax.ShapeDtypeStruct((n, 256), F32),
        jax.ShapeDtypeStruct((n, 512), BF16),
        jax.ShapeDtypeStruct((n, 512), F32),
        jax.ShapeDtypeStruct((n, 256), F32),
        jax.ShapeDtypeStruct((n, 768), BF16),
        jax.ShapeDtypeStruct((n, LANES), F32),
        jax.ShapeDtypeStruct((n, LANES), F32),
    ]
    out_specs = [pl.BlockSpec((tm, s.shape[1]), tok) for s in out_shape]
    scratch = []
    if do_cum:
        out_shape += [jax.ShapeDtypeStruct((n, LANES), F32),
                      jax.ShapeDtypeStruct((N_HEADS, n), F32)]
        out_specs += [pl.BlockSpec((tm, LANES), tok),
                      pl.BlockSpec((N_HEADS, tm), lambda i, j: (0, i * gj + j))]
        scratch = [pltpu.VMEM((8, LANES), F32)]
    return pl.pallas_call(
        functools.partial(_pre_kernel, do_cum=do_cum),
        grid=(gi, gj),
        in_specs=in_specs,
        out_specs=out_specs,
        out_shape=out_shape,
        scratch_shapes=scratch,
        compiler_params=_cparams(("arbitrary", "arbitrary")),
        name="pre_mixer",
    )(x3, mod3, mod3, g1, w_p, bd, tri, gains, bff)


def _flash_step(qp, k, v, bias, m, l, acc):
    s = _dot_nt(qp, k) + bias
    m_new = jnp.maximum(m, jnp.max(s, axis=-1, keepdims=True))
    alpha = jnp.exp(m - m_new)
    p = jnp.exp(s - m_new)
    l = alpha * l + jnp.sum(p, axis=-1, keepdims=True)
    acc = alpha * acc + _dot(p.astype(BF16), v)
    return m_new, l, acc


def _flash_init(rows):
    return (jnp.full((rows, 1), NEG, F32), jnp.zeros((rows, 1), F32), jnp.zeros((rows, LANES), F32))


def _half_mask(rows, g):
    lane = lax.broadcasted_iota(jnp.int32, (rows, LANES), 1)
    return (lane < HEAD_DIM) if g == 0 else (lane >= HEAD_DIM)


def _slope(h):
    return 2.0 ** (-8.0 * (h + 1) / N_HEADS)


def _select_blocks(imp, cur):
    blk = lax.broadcasted_iota(jnp.int32, imp.shape, 1)
    forced = (blk == 0) | (blk == cur) | (blk == cur - 1)
    score = jnp.where(blk <= cur, imp + jnp.where(forced, FORCE_BONUS, 0.0), -jnp.inf)
    blkf = blk.astype(F32)

    def body(_, carry):
        sc, sel = carry
        mx = jnp.max(sc, axis=-1, keepdims=True)
        idx = jnp.min(jnp.where(sc == mx, blkf, 1e9), axis=-1, keepdims=True)
        pick = blkf == idx
        return jnp.where(pick, -jnp.inf, sc), jnp.where(pick, 1.0, sel)

    _, sel = lax.fori_loop(0, N_SELECT, body, (score, jnp.zeros(imp.shape, F32)))
    return jnp.where(blk <= cur, sel, 0.0)


def _importance_matrix(n_cmp_rows, n_sel_cols, n_cmp):
    r = SEL_BLOCK // CMP_STRIDE
    a = np.zeros((n_cmp_rows, n_sel_cols), np.float32)
    for n in range(n_cmp):
        for j in range(n_sel_cols):
            off = n - r * j
            if off in (-1, r - 1):
                a[n, j] = 0.5
            elif 0 <= off <= r - 2:
                a[n, j] = 1.0
    return a


def _fox_p_kernel(q_ref, kv_ref, c_ref, ct_ref, o_ref, *, tq):
    qi = pl.program_id(1)
    rows = lax.broadcasted_iota(jnp.int32, (tq, tq), 0)
    cols = lax.broadcasted_iota(jnp.int32, (tq, tq), 1)
    causal = cols <= rows
    for r in range(N_REP):
        qc = q_ref[:, r * LANES:(r + 1) * LANES]
        outs = []
        for g in range(N_GROUPS):
            h = g * N_REP + r
            qp = jnp.where(_half_mask(tq, g), qc, jnp.zeros_like(qc))
            cq = c_ref[:, LOGF_LANE0 + h:LOGF_LANE0 + h + 1]

            def tile(j, h=h):
                st = pl.multiple_of(j * tq, tq)
                return (kv_ref[pl.ds(st, tq), 0:LANES], kv_ref[pl.ds(st, tq), LANES:2 * LANES],
                        ct_ref[h:h + 1, pl.ds(st, tq)])

            k, v, ck = tile(qi)
            carry = _flash_step(qp, k, v, jnp.where(causal, cq - ck, NEG), *_flash_init(tq))

            def body(j, carry, qp=qp, cq=cq, tile=tile):
                k, v, ck = tile(j)
                return _flash_step(qp, k, v, cq - ck, *carry)

            m, l, acc = lax.fori_loop(0, qi, body, carry)
            outs.append(acc / l)
        o_ref[:, r * LANES:(r + 1) * LANES] = jnp.where(_half_mask(tq, 0), outs[0], outs[1]).astype(BF16)


def _fox_p_call(qf, kvb, c_tm, ct, B, T, tq):
    n = B * T
    nq = T // tq
    return pl.pallas_call(
        functools.partial(_fox_p_kernel, tq=tq),
        grid=(B, nq),
        in_specs=[pl.BlockSpec((tq, 512), lambda b, i: (b * nq + i, 0)),
                  pl.BlockSpec((T, 256), lambda b, i: (b, 2)),
                  pl.BlockSpec((tq, LANES), lambda b, i: (b * nq + i, 0)),
                  pl.BlockSpec((N_HEADS, T), lambda b, i: (0, b))],
        out_specs=pl.BlockSpec((tq, 512), lambda b, i: (b * nq + i, 0)),
        out_shape=jax.ShapeDtypeStruct((n, 512), BF16),
        compiler_params=_cparams(("arbitrary", "arbitrary")),
        name="fox_prompt",
    )(qf, kvb, c_tm, ct)


def _compress(xk_ref, xv_ref, n, w1k_ref, w1v_ref, posk_ref, posv_ref, w2k_ref, w2v_ref, bd, gk):
    acc_k = jnp.zeros((n, 4 * LANES), F32)
    acc_v = jnp.zeros((n, 4 * LANES), F32)
    ck = jnp.zeros((8, 4 * LANES), F32)
    cv = jnp.zeros((8, 4 * LANES), F32)
    for c in range(CMP_STRIDE):
        xk = xk_ref[pl.ds(c, n, stride=CMP_STRIDE), :]
        xv = xv_ref[pl.ds(c, n, stride=CMP_STRIDE), :]
        acc_k = acc_k + _dot(xk.astype(BF16), w1k_ref[c])
        acc_v = acc_v + _dot(xv.astype(BF16), w1v_ref[c])
        ck = ck + _dot(posk_ref[c], w1k_ref[c])
        cv = cv + _dot(posv_ref[c], w1v_ref[c])

    def finish(acc, cst, w2_ref):
        lead = acc[:, 0:2 * LANES] + cst[0:1, 0:2 * LANES]
        trail = acc[:, 2 * LANES:4 * LANES] + cst[1:2, 2 * LANES:4 * LANES]
        hid = lead + pltpu.roll(trail, n - 1, 0)
        act = hid * _sigmoid(hid)
        return _dot(act.astype(BF16), w2_ref[...])

    kc = _head_rms(finish(acc_k, ck, w2k_ref), bd) * gk
    vc = finish(acc_v, cv, w2v_ref)
    return kc, vc


def _cmp_p_kernel(xk_ref, xv_ref, w1k_ref, w1v_ref, posk_ref, posv_ref, w2k_ref, w2v_ref, bd_ref, gk_ref,
                  kc_ref, vc_ref):
    n = kc_ref.shape[0]
    kc, vc = _compress(xk_ref, xv_ref, n, w1k_ref, w1v_ref, posk_ref, posv_ref, w2k_ref, w2v_ref,
                       bd_ref[...], gk_ref[...])
    kc_ref[...] = kc.astype(BF16)
    vc_ref[...] = vc.astype(BF16)


def _cmp_p_call(nrow, cw, B, T):
    nc = T // CMP_STRIDE
    consts = [cw["w1k"], cw["w1v"], cw["posk"], cw["posv"], cw["w2k"], cw["w2v"], cw["bd"], cw["gk"]]
    cspecs = [pl.BlockSpec(a.shape, (lambda b, nd=a.ndim: (0,) * nd)) for a in consts]
    return pl.pallas_call(
        _cmp_p_kernel,
        grid=(B,),
        in_specs=[pl.BlockSpec((T, LANES), lambda b: (b, 0)),
                  pl.BlockSpec((T, LANES), lambda b: (b, 1))] + cspecs,
        out_specs=[pl.BlockSpec((nc, LANES), lambda b: (b, 0))] * 2,
        out_shape=[jax.ShapeDtypeStruct((B * nc, LANES), BF16)] * 2,
        compiler_params=_cparams(("arbitrary",)),
        name="compress_prompt",
    )(nrow, nrow, *consts)


def _nsa_p_kernel(q_ref, kv_ref, kc_ref, vc_ref, gate_ref, a_ref, e_ref, o_ref, maskb_ref, *, tq, T):
    qi = pl.program_id(1)
    t0 = qi * tq
    nc = kc_ref.shape[0]
    rows = lax.broadcasted_iota(jnp.int32, (tq, tq), 0)
    cols = lax.broadcasted_iota(jnp.int32, (tq, tq), 1)
    causal = cols <= rows
    rc = (rows - cols).astype(F32)
    tpos = t0 + lax.broadcasted_iota(jnp.int32, (tq, 1), 0)

    nid = lax.broadcasted_iota(jnp.int32, (tq, nc), 1)
    dc = tpos - (nid * CMP_STRIDE + (2 * CMP_STRIDE - 1))
    cvalid = (dc >= 0) & (nid < nc - 1)
    dcf = dc.astype(F32)
    kc = kc_ref[...]
    vc = vc_ref[...]
    psum = [jnp.zeros((tq, nc), F32) for _ in range(N_GROUPS)]
    o_c = {}
    for r in range(N_REP):
        qc = q_ref[:, r * LANES:(r + 1) * LANES]
        for g in range(N_GROUPS):
            h = g * N_REP + r
            qp = jnp.where(_half_mask(tq, g), qc, jnp.zeros_like(qc))
            s = jnp.where(cvalid, _dot_nt(qp, kc) - _slope(h) * dcf, NEG)
            mx = jnp.max(s, axis=-1, keepdims=True)
            p = jnp.where(cvalid, jnp.exp(s - mx), 0.0)
            den = jnp.sum(p, axis=-1, keepdims=True)
            p = p / jnp.where(den > 0, den, 1.0)
            psum[g] = psum[g] + p
            o_c[h] = _dot(p.astype(BF16), vc)

    cur = tpos // SEL_BLOCK
    for g in range(N_GROUPS):
        imp = _dot_exact_r(psum[g], a_ref[...])
        sel = _select_blocks(imp, cur)
        maskb_ref[g] = _dot(((sel - 1.0) * MASK_BIG).astype(BF16), e_ref[...])

    for r in range(N_REP):
        qc = q_ref[:, r * LANES:(r + 1) * LANES]
        outs = []
        for g in range(N_GROUPS):
            h = g * N_REP + r
            slope = _slope(h)
            qp = jnp.where(_half_mask(tq, g), qc, jnp.zeros_like(qc))

            def kv_tile(j, c0):
                st = pl.multiple_of(j * tq, tq)
                return kv_ref[pl.ds(st, tq), c0:c0 + LANES], kv_ref[pl.ds(st, tq), c0 + LANES:c0 + 2 * LANES]

            k, v = kv_tile(qi, 0)
            mb = maskb_ref[g, :, pl.ds(pl.multiple_of(t0, tq), tq)]
            carry = _flash_step(qp, k, v, jnp.where(causal, mb - slope * rc, NEG), *_flash_init(tq))

            def sel_body(j, carry, qp=qp, g=g, slope=slope):
                k, v = kv_tile(j, 0)
                mb = maskb_ref[g, :, pl.ds(pl.multiple_of(j * tq, tq), tq)]
                dist = rc + (t0 - j * tq).astype(F32)
                return _flash_step(qp, k, v, mb - slope * dist, *carry)

            m, l, acc = lax.fori_loop(0, qi, sel_body, carry)
            o_s = acc / l

            k, v = kv_tile(qi, 2 * LANES)
            carry = _flash_step(qp, k, v, jnp.where(causal, -slope * rc, NEG), *_flash_init(tq))

            def win_body(j, carry, qp=qp, slope=slope):
                k, v = kv_tile(j, 2 * LANES)
                dist = rc + (t0 - j * tq).astype(F32)
                return _flash_step(qp, k, v, jnp.where(dist < WINDOW, -slope * dist, NEG), *carry)

            lo = jnp.maximum(qi - (WINDOW + tq - 1) // tq, 0)
            m, l, acc = lax.fori_loop(lo, qi, win_body, carry)
            o_w = acc / l

            gl = GATE_LANE0 + 3 * h
            outs.append(gate_ref[:, gl:gl + 1] * o_c[h] + gate_ref[:, gl + 1:gl + 2] * o_s
                        + gate_ref[:, gl + 2:gl + 3] * o_w)
        o_ref[:, r * LANES:(r + 1) * LANES] = jnp.where(_half_mask(tq, 0), outs[0], outs[1]).astype(BF16)


def _nsa_p_call(qn, kvb, kc, vc, gates, a_mat, e_mat, B, T, tq):
    n = B * T
    nq = T // tq
    nc = T // CMP_STRIDE
    return pl.pallas_call(
        functools.partial(_nsa_p_kernel, tq=tq, T=T),
        grid=(B, nq),
        in_specs=[pl.BlockSpec((tq, 512), lambda b, i: (b * nq + i, 0)),
                  pl.BlockSpec((T, 512), lambda b, i: (b, 0)),
                  pl.BlockSpec((nc, LANES), lambda b, i: (b, 0)),
                  pl.BlockSpec((nc, LANES), lambda b, i: (b, 0)),
                  pl.BlockSpec((tq, LANES), lambda b, i: (b * nq + i, 0)),
                  pl.BlockSpec(a_mat.shape, lambda b, i: (0, 0)),
                  pl.BlockSpec(e_mat.shape, lambda b, i: (0, 0))],
        out_specs=pl.BlockSpec((tq, 512), lambda b, i: (b * nq + i, 0)),
        out_shape=jax.ShapeDtypeStruct((n, 512), BF16),
        scratch_shapes=[pltpu.VMEM((N_GROUPS, tq, T), F32)],
        compiler_params=_cparams(("arbitrary", "arbitrary")),
        name="nsa_prompt",
    )(qn, kvb, kc, vc, gates, a_mat, e_mat)


def _mix_kernel(x_ref, of_ref, on_ref, sc_ref, sh_ref, gt_ref, g1_ref, wmg_ref, wbf_ref, wbn_ref,
                wout_ref, o_ref):
    nb, tt, d = x_ref.shape
    tm = nb * tt
    x = x_ref[...]
    ms = jnp.mean(x * x, axis=-1, keepdims=True)
    h = x * lax.rsqrt(ms + RMS_EPS) * g1_ref[...] * (1.0 + sc_ref[...]) + sh_ref[...]
    h = h.reshape(tm, d).astype(BF16)
    g_fox = _sigmoid(_dot(h, wmg_ref[:, 0:d]))
    g_nsa = _sigmoid(_dot(h, wmg_ref[:, d:2 * d]))
    mix = g_fox * _dot(of_ref[...], wbf_ref[...]) + g_nsa * _dot(on_ref[...], wbn_ref[...])
    y = _dot(mix.astype(BF16), wout_ref[...]).reshape(nb, tt, d)
    o_ref[...] = x + gt_ref[...] * y


def _mix_call(x3, o_fox, o_nsa, mod3, g1, wmg, wbf, wbn, wout, nb, tt):
    NB, TT, d = x3.shape
    tm = nb * tt
    gi, gj = NB // nb, TT // tt
    full = lambda i, j: (0, 0)
    tok = lambda i, j: (i * gj + j, 0)
    return pl.pallas_call(
        _mix_kernel,
        grid=(gi, gj),
        in_specs=[pl.BlockSpec((nb, tt, d), lambda i, j: (i, j, 0)),
                  pl.BlockSpec((tm, 512), tok),
                  pl.BlockSpec((tm, 512), tok),
                  pl.BlockSpec((nb, 1, d), lambda i, j: (i, 0, 1)),
                  pl.BlockSpec((nb, 1, d), lambda i, j: (i, 0, 0)),
                  pl.BlockSpec((nb, 1, d), lambda i, j: (i, 0, 2)),
                  pl.BlockSpec((1, d), full),
                  pl.BlockSpec(wmg.shape, full),
                  pl.BlockSpec(wbf.shape, full),
                  pl.BlockSpec(wbn.shape, full),
                  pl.BlockSpec(wout.shape, full)],
        out_specs=pl.BlockSpec((nb, tt, d), lambda i, j: (i, j, 0)),
        out_shape=jax.ShapeDtypeStruct((NB, TT, d), F32),
        compiler_params=_cparams(("arbitrary", "arbitrary")),
        name="post_mix",
    )(x3, o_fox, o_nsa, mod3, mod3, mod3, g1, wmg, wbf, wbn, wout)


def _ffn_kernel(x_ref, sc_ref, sh_ref, gt_ref, g2_ref, wup_ref, wdn_ref, o_ref, *, fc):
    nb, tt, d = x_ref.shape
    tm = nb * tt
    x = x_ref[...]
    ms = jnp.mean(x * x, axis=-1, keepdims=True)
    h = x * lax.rsqrt(ms + RMS_EPS) * g2_ref[...] * (1.0 + sc_ref[...]) + sh_ref[...]
    h = h.reshape(tm, d).astype(BF16)
    acc = jnp.zeros((tm, d), F32)
    for c in range(D_FF // fc):
        u = jnp.maximum(_dot(h, wup_ref[:, c * fc:(c + 1) * fc]), 0.0)
        acc = acc + _dot((u * u).astype(BF16), wdn_ref[c * fc:(c + 1) * fc, :])
    o_ref[...] = x + gt_ref[...] * acc.reshape(nb, tt, d)


def _ffn_call(x3, mod3, g2, wup, wdn, nb, tt):
    NB, TT, d = x3.shape
    gi, gj = NB // nb, TT // tt
    full = lambda i, j: (0, 0)
    return pl.pallas_call(
        functools.partial(_ffn_kernel, fc=1024),
        grid=(gi, gj),
        in_specs=[pl.BlockSpec((nb, tt, d), lambda i, j: (i, j, 0)),
                  pl.BlockSpec((nb, 1, d), lambda i, j: (i, 0, 4)),
                  pl.BlockSpec((nb, 1, d), lambda i, j: (i, 0, 3)),
                  pl.BlockSpec((nb, 1, d), lambda i, j: (i, 0, 5)),
                  pl.BlockSpec((1, d), full),
                  pl.BlockSpec(wup.shape, full, pipeline_mode=pl.Buffered(1)),
                  pl.BlockSpec(wdn.shape, full, pipeline_mode=pl.Buffered(1))],
        out_specs=pl.BlockSpec((nb, tt, d), lambda i, j: (i, j, 0)),
        out_shape=jax.ShapeDtypeStruct((NB, TT, d), F32),
        compiler_params=_cparams(("arbitrary", "arbitrary")),
        name="ffn",
    )(x3, mod3, mod3, mod3, g2, wup, wdn)


def _gather_copy(pool_ref, page, lane0, buf_ref, slot, p, sem_ref):
    return pltpu.make_async_copy(
        pool_ref.at[page, :, pl.ds(lane0, buf_ref.shape[2])],
        buf_ref.at[slot, pl.ds(p * PAGE_SIZE, PAGE_SIZE), :],
        sem_ref.at[slot])


def _gather_start(pt_ref, b, pool_ref, parts, slot, sem_ref, n_pages):
    def body(p, _):
        for lane0, buf_ref in parts:
            _gather_copy(pool_ref, pt_ref[b, p], lane0, buf_ref, slot, p, sem_ref).start()
        return 0
    lax.fori_loop(0, n_pages, body, 0)


def _gather_wait(pool_ref, parts, slot, sem_ref, n_pages):
    def body(p, _):
        for lane0, buf_ref in parts:
            _gather_copy(pool_ref, 0, lane0, buf_ref, slot, p, sem_ref).wait()
        return 0
    lax.fori_loop(0, n_pages, body, 0)


def _gather_pipeline(pt_ref, pool_ref, parts, sem_ref, n_pages):
    b = pl.program_id(0)
    nb = pl.num_programs(0)
    slot = lax.rem(b, 2)

    @pl.when(b == 0)
    def _():
        _gather_start(pt_ref, 0, pool_ref, parts, 0, sem_ref, n_pages)

    @pl.when(b + 1 < nb)
    def _():
        _gather_start(pt_ref, b + 1, pool_ref, parts, 1 - slot, sem_ref, n_pages)

    _gather_wait(pool_ref, parts, slot, sem_ref, n_pages)
    return slot


def _cum_s_kernel(pt_ref, pool_ref, lfn_ref, cp_ref, cn_ref, buf_ref, sem_ref, *, n_pages):
    slot = _gather_pipeline_rows(pt_ref, pool_ref, buf_ref, sem_ref, n_pages)
    x = buf_ref[slot]
    w = x.shape[1]
    lane = lax.broadcasted_iota(jnp.int32, x.shape, 1)
    row = lax.broadcasted_iota(jnp.int32, x.shape, 0)
    s = N_HEADS
    while s < w:
        x = x + jnp.where(lane >= s, pltpu.roll(x, s, 1), 0.0)
        s *= 2
    tot = jnp.where(lane >= w - N_HEADS, x, 0.0)
    s = N_HEADS
    while s < w:
        tot = tot + pltpu.roll(tot, w - s, 1)
        s *= 2
    inc = tot
    s = 1
    while s < n_pages:
        inc = inc + jnp.where(row >= s, pltpu.roll(inc, s, 0), 0.0)
        s *= 2
    c_past = x + (inc - tot)
    cp_ref[0] = c_past
    y = lfn_ref[0]
    lane1 = lax.broadcasted_iota(jnp.int32, y.shape, 1)
    s = N_HEADS
    while s < LANES:
        y = y + jnp.where(lane1 >= s, pltpu.roll(y, s, 1), 0.0)
        s *= 2
    end = inc[n_pages - 1:n_pages, 0:LANES]
    cn_ref[0] = y + end


def _gather_pipeline_rows(pt_ref, pool_ref, buf_ref, sem_ref, n_pages):
    b = pl.program_id(0)
    nb = pl.num_programs(0)
    slot = lax.rem(b, 2)

    def copy(page, sl, p):
        return pltpu.make_async_copy(pool_ref.at[pl.ds(page, 1), :], buf_ref.at[sl, pl.ds(p, 1), :],
                                     sem_ref.at[sl])

    def start(bb, sl):
        def body(p, _):
            copy(pt_ref[bb, p], sl, p).start()
            return 0
        lax.fori_loop(0, n_pages, body, 0)

    @pl.when(b == 0)
    def _():
        start(0, 0)

    @pl.when(b + 1 < nb)
    def _():
        start(b + 1, 1 - slot)

    def wbody(p, _):
        copy(0, slot, p).wait()
        return 0
    lax.fori_loop(0, n_pages, wbody, 0)
    return slot


def _cum_s_call(page_table, logf_pool_flat, lf_new):
    B, n_pages = page_table.shape
    w = logf_pool_flat.shape[1]
    grid_spec = pltpu.PrefetchScalarGridSpec(
        num_scalar_prefetch=1,
        grid=(B,),
        in_specs=[pl.BlockSpec(memory_space=pl.ANY),
                  pl.BlockSpec((1, 1, LANES), lambda b, pt: (b, 0, 0))],
        out_specs=[pl.BlockSpec((1, n_pages, w), lambda b, pt: (b, 0, 0)),
                   pl.BlockSpec((1, 1, LANES), lambda b, pt: (b, 0, 0))],
        scratch_shapes=[pltpu.VMEM((2, n_pages, w), F32), pltpu.SemaphoreType.DMA((2,))],
    )
    return pl.pallas_call(
        functools.partial(_cum_s_kernel, n_pages=n_pages),
        grid_spec=grid_spec,
        out_shape=[jax.ShapeDtypeStruct((B, n_pages, w), F32),
                   jax.ShapeDtypeStruct((B, 1, LANES), F32)],
        compiler_params=_cparams(("arbitrary",)),
        name="cum_sample",
    )(page_table, logf_pool_flat, lf_new)


def _stack_q(q_ref, tn):
    parts = []
    for g in range(N_GROUPS):
        for r in range(N_REP):
            qc = q_ref[:, r * LANES:(r + 1) * LANES]
            parts.append(jnp.where(_half_mask(tn, g), qc, jnp.zeros_like(qc)))
    return jnp.concatenate(parts, axis=0)


def _unstack_o(o, tn):
    chunks = []
    for r in range(N_REP):
        a = o[r * tn:(r + 1) * tn]
        b = o[(N_REP + r) * tn:(N_REP + r + 1) * tn]
        chunks.append(jnp.where(_half_mask(tn, 0), a, b))
    return jnp.concatenate(chunks, axis=1)


def _per_head_rows(x8, tn):
    n = x8.shape[1]
    return jnp.broadcast_to(x8[:, None, :], (N_HEADS, tn, n)).reshape(N_HEADS * tn, n)


def _slope_col(tn):
    hrow = lax.broadcasted_iota(jnp.int32, (N_HEADS * tn, 1), 0) // tn
    col = jnp.zeros((N_HEADS * tn, 1), F32)
    for h in range(N_HEADS):
        col = jnp.where(hrow == h, _slope(h), col)
    return col


def _tok_col(tn):
    return lax.rem(lax.broadcasted_iota(jnp.int32, (N_HEADS * tn, 1), 0), tn)


def _pad_keys(x, dtype):
    tn = x.shape[0]
    return jnp.concatenate([x, jnp.zeros((LANES - tn, LANES), x.dtype)], axis=0).astype(dtype)


def _fox_s_kernel(pt_ref, pool_ref, q_ref, new_ref, ck_ref, cn_ref, o_ref, buf_ref, sem_ref,
                  *, n_pages, tn, kc):
    slot = _gather_pipeline(pt_ref, pool_ref, [(0, buf_ref)], sem_ref, n_pages)
    past = n_pages * PAGE_SIZE
    R = N_HEADS * tn
    qs = _stack_q(q_ref, tn)
    cn = cn_ref[0]
    tcol = _tok_col(tn)
    tl = lax.broadcasted_iota(jnp.int32, (R, LANES), 1)
    cn_rows = _per_head_rows(cn, tn)
    cq = jnp.sum(jnp.where(tl == tcol, cn_rows, 0.0), axis=-1, keepdims=True)
    knew = _pad_keys(new_ref[:, 0:LANES], BF16)
    vnew = _pad_keys(new_ref[:, LANES:2 * LANES], BF16)
    bias = jnp.where(tl <= tcol, cq - cn_rows, NEG)
    carry = _flash_step(qs, knew, vnew, bias, *_flash_init(R))
    for c in range(past // kc):
        k = buf_ref[slot, c * kc:(c + 1) * kc, 0:LANES].astype(BF16)
        v = buf_ref[slot, c * kc:(c + 1) * kc, LANES:2 * LANES].astype(BF16)
        ck = _per_head_rows(ck_ref[0, :, c * kc:(c + 1) * kc], tn)
        carry = _flash_step(qs, k, v, cq - ck, *carry)
    m, l, acc = carry
    o_ref[...] = _unstack_o(acc / l, tn).astype(BF16)


def _fox_s_call(page_table, fox_pool, qf, frow, ck, cn, tn):
    B, n_pages = page_table.shape
    past = n_pages * PAGE_SIZE
    kc = min(past, 2048)
    grid_spec = pltpu.PrefetchScalarGridSpec(
        num_scalar_prefetch=1,
        grid=(B,),
        in_specs=[pl.BlockSpec(memory_space=pl.ANY),
                  pl.BlockSpec((tn, 512), lambda b, pt: (b, 0)),
                  pl.BlockSpec((tn, 256), lambda b, pt: (b, 0)),
                  pl.BlockSpec((1, N_HEADS, past), lambda b, pt: (b, 0, 0)),
                  pl.BlockSpec((1, N_HEADS, LANES), lambda b, pt: (b, 0, 0))],
        out_specs=pl.BlockSpec((tn, 512), lambda b, pt: (b, 0)),
        scratch_shapes=[pltpu.VMEM((2, past, 256), F32), pltpu.SemaphoreType.DMA((2,))],
    )
    return pl.pallas_call(
        functools.partial(_fox_s_kernel, n_pages=n_pages, tn=tn, kc=kc),
        grid_spec=grid_spec,
        out_shape=jax.ShapeDtypeStruct((B * tn, 512), BF16),
        compiler_params=_cparams(("arbitrary",)),
        name="fox_sample",
    )(page_table, fox_pool, qf, frow, ck, cn)


def _cmp_s_kernel(pt_ref, pool_ref, q_ref, w1k_ref, w1v_ref, posk_ref, posv_ref, w2k_ref, w2v_ref,
                  bd_ref, gk_ref, a_ref, oc_ref, imp_ref, bufk_ref, bufv_ref, sem_ref, *, n_pages, tn):
    slot = _gather_pipeline(pt_ref, pool_ref, [(0, bufk_ref), (LANES, bufv_ref)], sem_ref, n_pages)
    past = n_pages * PAGE_SIZE
    nc = past // CMP_STRIDE
    R = N_HEADS * tn
    kc, vc = _compress(bufk_ref.at[slot], bufv_ref.at[slot], nc, w1k_ref, w1v_ref, posk_ref, posv_ref,
                       w2k_ref, w2v_ref, bd_ref[...], gk_ref[...])
    qs = _stack_q(q_ref, tn)
    nid = lax.broadcasted_iota(jnp.int32, (R, nc), 1)
    tpos = past + _tok_col(tn)
    dc = tpos - (nid * CMP_STRIDE + (2 * CMP_STRIDE - 1))
    valid = (dc >= 0) & (nid < nc - 1)
    s = jnp.where(valid, _dot_nt(qs, kc.astype(BF16)) - _slope_col(tn) * dc.astype(F32), NEG)
    mx = jnp.max(s, axis=-1, keepdims=True)
    p = jnp.where(valid, jnp.exp(s - mx), 0.0)
    den = jnp.sum(p, axis=-1, keepdims=True)
    p = p / jnp.where(den > 0, den, 1.0)
    oc_ref[0] = _dot(p.astype(BF16), vc.astype(BF16))
    p4 = p.reshape(N_GROUPS, N_REP, tn, nc)
    psum = (p4[:, 0] + p4[:, 1] + p4[:, 2] + p4[:, 3]).reshape(N_GROUPS * tn, nc)
    imp_ref[0] = _dot_exact_r(psum, a_ref[...])


def _cmp_s_call(page_table, nsa_pool, qn, cw, a_mat, tn):
    B, n_pages = page_table.shape
    past = n_pages * PAGE_SIZE
    consts = [cw["w1k"], cw["w1v"], cw["posk"], cw["posv"], cw["w2k"], cw["w2v"], cw["bd"], cw["gk"], a_mat]
    cspecs = [pl.BlockSpec(a.shape, (lambda b, pt, nd=a.ndim: (0,) * nd)) for a in consts]
    nsl = a_mat.shape[1]
    grid_spec = pltpu.PrefetchScalarGridSpec(
        num_scalar_prefetch=1,
        grid=(B,),
        in_specs=[pl.BlockSpec(memory_space=pl.ANY),
                  pl.BlockSpec((tn, 512), lambda b, pt: (b, 0))] + cspecs,
        out_specs=[pl.BlockSpec((1, N_HEADS * tn, LANES), lambda b, pt: (b, 0, 0)),
                   pl.BlockSpec((1, N_GROUPS * tn, nsl), lambda b, pt: (b, 0, 0))],
        scratch_shapes=[pltpu.VMEM((2, past, LANES), F32), pltpu.VMEM((2, past, LANES), F32),
                        pltpu.SemaphoreType.DMA((2,))],
    )
    return pl.pallas_call(
        functools.partial(_cmp_s_kernel, n_pages=n_pages, tn=tn),
        grid_spec=grid_spec,
        out_shape=[jax.ShapeDtypeStruct((B, N_HEADS * tn, LANES), F32),
                   jax.ShapeDtypeStruct((B, N_GROUPS * tn, nsl), F32)],
        compiler_params=_cparams(("arbitrary",)),
        name="compress_sample",
    )(page_table, nsa_pool, qn, *consts)


def _topk_s_kernel(imp_ref, o_ref, *, tn, past):
    rows = imp_ref.shape[0]
    t = past + lax.rem(lax.broadcasted_iota(jnp.int32, (rows, 1), 0), tn)
    sel = _select_blocks(imp_ref[...], t // SEL_BLOCK)
    o_ref[...] = ((sel - 1.0) * MASK_BIG).astype(BF16)


def _topk_s_call(imp2, tn, past):
    rows, nsl = imp2.shape
    tr = min(rows, 256)
    return pl.pallas_call(
        functools.partial(_topk_s_kernel, tn=tn, past=past),
        grid=(rows // tr,),
        in_specs=[pl.BlockSpec((tr, nsl), lambda i: (i, 0))],
        out_specs=pl.BlockSpec((tr, nsl), lambda i: (i, 0)),
        out_shape=jax.ShapeDtypeStruct((rows, nsl), BF16),
        compiler_params=_cparams(("arbitrary",)),
        name="select_sample",
    )(imp2)


def _sel_s_kernel(pt_ref, pool_ref, q_ref, nnew_ref, wnew_ref, win_ref, seln_ref, oc_ref, gate_ref, e_ref,
                  o_ref, buf_ref, sem_ref, *, n_pages, tn, kc):
    slot = _gather_pipeline(pt_ref, pool_ref, [(2 * LANES, buf_ref)], sem_ref, n_pages)
    past = n_pages * PAGE_SIZE
    R = N_HEADS * tn
    bpc = kc // SEL_BLOCK
    qs = _stack_q(q_ref, tn)
    tcol = _tok_col(tn)
    slope = _slope_col(tn)
    tl = lax.broadcasted_iota(jnp.int32, (R, LANES), 1)
    newbias = jnp.where(tl <= tcol, -slope * (tcol - tl).astype(F32), NEG)

    def group_rows(x):
        n = x.shape[1]
        x4 = jnp.broadcast_to(x.reshape(N_GROUPS, 1, tn, n), (N_GROUPS, N_REP, tn, n))
        return x4.reshape(R, n)

    n_chunks = past // kc
    seln_new = seln_ref[0, n_chunks]
    mb_new = group_rows(_dot(seln_new, e_ref[...])[:, 0:LANES])
    knew = _pad_keys(nnew_ref[:, 2 * LANES:3 * LANES], BF16)
    vnew = _pad_keys(nnew_ref[:, 3 * LANES:4 * LANES], BF16)
    carry = _flash_step(qs, knew, vnew, jnp.where(tl <= tcol, newbias + mb_new, NEG), *_flash_init(R))
    for c in range(n_chunks):
        k = buf_ref[slot, c * kc:(c + 1) * kc, 0:LANES].astype(BF16)
        v = buf_ref[slot, c * kc:(c + 1) * kc, LANES:2 * LANES].astype(BF16)
        mb = group_rows(_dot(seln_ref[0, c], e_ref[...]))
        kpos = c * kc + lax.broadcasted_iota(jnp.int32, (1, kc), 1)
        dist = ((past + tcol) - kpos).astype(F32)
        carry = _flash_step(qs, k, v, mb - slope * dist, *carry)
    m, l, acc = carry
    o_s = acc / l

    wlen = win_ref.shape[1]
    knew = _pad_keys(wnew_ref[:, 0:LANES], BF16)
    vnew = _pad_keys(wnew_ref[:, LANES:2 * LANES], BF16)
    carry = _flash_step(qs, knew, vnew, newbias, *_flash_init(R))
    k = win_ref[0, :, 0:LANES].astype(BF16)
    v = win_ref[0, :, LANES:2 * LANES].astype(BF16)
    wpos = (past - wlen) + lax.broadcasted_iota(jnp.int32, (1, wlen), 1)
    dw = (past + tcol) - wpos
    bias = jnp.where((dw < WINDOW) & (wpos >= 0), -slope * dw.astype(F32), NEG)
    m, l, acc = _flash_step(qs, k, v, bias, *carry)
    o_w = acc / l

    gl = lax.broadcasted_iota(jnp.int32, (R, LANES), 1)
    hrow = lax.broadcasted_iota(jnp.int32, (R, LANES), 0) // tn
    gt = jnp.broadcast_to(gate_ref[...][None], (N_HEADS, tn, LANES)).reshape(R, LANES)

    def gcol(j):
        return jnp.sum(jnp.where(gl == GATE_LANE0 + 3 * hrow + j, gt, 0.0), axis=-1, keepdims=True)

    o = gcol(0) * oc_ref[0] + gcol(1) * o_s + gcol(2) * o_w
    o_ref[...] = _unstack_o(o, tn).astype(BF16)


def _sel_s_call(page_table, nsa_pool, qn, nrow, wrow, win_buf, seln, oc, gates, e_mat, tn, kc):
    B, n_pages = page_table.shape
    past = n_pages * PAGE_SIZE
    wlen = win_buf.shape[1]
    grid_spec = pltpu.PrefetchScalarGridSpec(
        num_scalar_prefetch=1,
        grid=(B,),
        in_specs=[pl.BlockSpec(memory_space=pl.ANY),
                  pl.BlockSpec((tn, 512), lambda b, pt: (b, 0)),
                  pl.BlockSpec((tn, 512), lambda b, pt: (b, 0)),
                  pl.BlockSpec((tn, 256), lambda b, pt: (b, 0)),
                  pl.BlockSpec((1, wlen, 256), lambda b, pt: (b, 0, 0)),
                  pl.BlockSpec((1,) + seln.shape[1:], lambda b, pt: (b, 0, 0, 0)),
                  pl.BlockSpec((1, N_HEADS * tn, LANES), lambda b, pt: (b, 0, 0)),
                  pl.BlockSpec((tn, LANES), lambda b, pt: (b, 0)),
                  pl.BlockSpec(e_mat.shape, lambda b, pt: (0, 0))],
        out_specs=pl.BlockSpec((tn, 512), lambda b, pt: (b, 0)),
        scratch_shapes=[pltpu.VMEM((2, past, 256), F32), pltpu.SemaphoreType.DMA((2,))],
    )
    return pl.pallas_call(
        functools.partial(_sel_s_kernel, n_pages=n_pages, tn=tn, kc=kc),
        grid_spec=grid_spec,
        out_shape=jax.ShapeDtypeStruct((B * tn, 512), BF16),
        compiler_params=_cparams(("arbitrary",)),
        name="select_attend_sample",
    )(page_table, nsa_pool, qn, nrow, wrow, win_buf, seln, oc, gates, e_mat)


def _pair_cols(base):
    idx = []
    for r in range(N_REP):
        for g in range(N_GROUPS):
            h = g * N_REP + r
            idx.extend(range(base + h * HEAD_DIM, base + (h + 1) * HEAD_DIM))
    return np.asarray(idx, np.int32)


def _prep_weights(w_in, b_fox_f, fox_qn_g, fox_kn_g, nsa_qn_g, nsa_kn_slc_g, nsa_kn_win_g):
    o_fq, o_fk, o_fv, o_ff, o_nq, o_nkv, o_ng, o_mg = 0, 512, 640, 768, 776, 1288, 2056, 2080
    cols = np.concatenate([
        _pair_cols(o_fq), np.arange(o_fk, o_fk + 128), np.arange(o_fv, o_fv + 128),
        _pair_cols(o_nq), np.arange(o_nkv, o_nkv + 768),
        np.arange(o_ng, o_ng + 24), np.arange(o_ff, o_ff + 8)]).astype(np.int32)
    w_p = jnp.take(w_in, cols, axis=1)
    w_p = jnp.pad(w_p, ((0, 0), (0, _C_END - w_p.shape[1]))).astype(BF16)
    w_mg = w_in[:, o_mg:o_mg + 2 * D_MODEL].astype(BF16)
    tile2 = lambda g: jnp.tile(g, 2)
    gains = jnp.stack([tile2(fox_qn_g), tile2(fox_kn_g), tile2(nsa_qn_g), tile2(nsa_kn_slc_g),
                       tile2(nsa_kn_win_g)] + [jnp.zeros((LANES,), F32)] * 3)
    bff = jnp.zeros((1, LANES), F32).at[0, LOGF_LANE0:LOGF_LANE0 + N_HEADS].set(b_fox_f)
    return w_p, w_mg, gains, bff


def _prep_compress(pos, w1, w2):
    w1r = w1.reshape(2, CMP_STRIDE, HEAD_DIM, CMP_HIDDEN)
    z = jnp.zeros((CMP_STRIDE, HEAD_DIM, CMP_HIDDEN), F32)
    top = jnp.concatenate([w1r[0], z, w1r[1], z], axis=-1)
    bot = jnp.concatenate([z, w1r[0], z, w1r[1]], axis=-1)
    w1b = jnp.concatenate([top, bot], axis=1).astype(BF16)
    pr = pos.reshape(2, CMP_STRIDE, HEAD_DIM)
    posb = jnp.concatenate([jnp.tile(pr, (1, 1, 2)).transpose(1, 0, 2),
                            jnp.zeros((CMP_STRIDE, 6, LANES), F32)], axis=1).astype(BF16)
    zz = jnp.zeros((CMP_HIDDEN, HEAD_DIM), F32)
    w2b = jnp.concatenate([jnp.concatenate([w2, zz], axis=1),
                           jnp.concatenate([zz, w2], axis=1)], axis=0).astype(BF16)
    return w1b, posb, w2b


def kernel(x_prompt, x_sample, cache_fox_kv, cache_fox_logf, cache_nsa_kv, state_win_kv, page_table,
           c_prompt, c_sample, norm1_g, norm2_g, w_ada, b_ada, w_in, b_fox_f, fox_qn_g, fox_kn_g,
           nsa_qn_g, nsa_kn_cmp_g, nsa_kn_slc_g, nsa_kn_win_g, cmp_pos_k, cmp_w1_k, cmp_w2_k,
           cmp_pos_v, cmp_w1_v, cmp_w2_v, w_br_fox, w_br_nsa, w_out, w_up, w_down):
    assert norm1_g.shape[0] == 1
    B, T, d = x_prompt.shape
    DB, TN, _ = x_sample.shape
    n_phys = cache_fox_kv.shape[1]
    n_pages = page_table.shape[1]
    past = n_pages * PAGE_SIZE
    wlen = state_win_kv.shape[2]
    assert T % 256 == 0 and T >= WINDOW and wlen == WINDOW and TN == 8 and past % CMP_STRIDE == 0

    w_p, w_mg, gains, bff = _prep_weights(w_in[0], b_fox_f[0], fox_qn_g[0], fox_kn_g[0], nsa_qn_g[0],
                                          nsa_kn_slc_g[0], nsa_kn_win_g[0])
    w1k, posk, w2k = _prep_compress(cmp_pos_k[0], cmp_w1_k[0], cmp_w2_k[0])
    w1v, posv, w2v = _prep_compress(cmp_pos_v[0], cmp_w1_v[0], cmp_w2_v[0])
    bd = jnp.asarray(np.kron(np.eye(2), np.ones((HEAD_DIM, HEAD_DIM))), BF16)
    cw = dict(w1k=w1k, w1v=w1v, posk=posk, posv=posv, w2k=w2k, w2v=w2v, bd=bd,
              gk=jnp.tile(nsa_kn_cmp_g[0], 2).reshape(1, LANES))
    pair_rows = _pair_cols(0)
    wbf = jnp.take(w_br_fox[0], pair_rows, axis=0).astype(BF16)
    wbn = jnp.take(w_br_nsa[0], pair_rows, axis=0).astype(BF16)
    wout = w_out[0].astype(BF16)
    wup = w_up[0].astype(BF16)
    wdn = w_down[0].astype(BF16)
    g1 = norm1_g[0].reshape(1, d)
    g2 = norm2_g[0].reshape(1, d)
    tm_p = 512
    tri = jnp.asarray(np.tril(np.ones((tm_p, tm_p), np.float32)), BF16)

    mod = _ada_call(jnp.concatenate([c_prompt, c_sample], axis=0), w_ada[0], b_ada[0])
    mod_p = mod[:B].reshape(B, 1, 6 * d)
    mod_s = mod[B:].reshape(DB, 1, 6 * d)

    (qf, frow, qn, nrow, wrow, kvb, gates, lf, c_tm, ct) = _pre_call(
        x_prompt, mod_p, g1, w_p, bd, tri, gains, bff, nb=1, tt=tm_p, do_cum=True)
    tq = 256
    o_fox = _fox_p_call(qf, kvb, c_tm, ct, B, T, tq)
    kc_p, vc_p = _cmp_p_call(nrow, cw, B, T)
    nc_p = T // CMP_STRIDE
    nsel_p = -(-T // SEL_BLOCK)
    a_p = jnp.asarray(_importance_matrix(nc_p, LANES, nc_p - 1)[:, :LANES] *
                      (np.arange(LANES) < nsel_p)[None, :], BF16)
    e_p = jnp.asarray((np.arange(T)[None, :] // SEL_BLOCK) == np.arange(LANES)[:, None], BF16)
    o_nsa = _nsa_p_call(qn, kvb, kc_p, vc_p, gates, a_p, e_p, B, T, tq)
    x1 = _mix_call(x_prompt, o_fox, o_nsa, mod_p, g1, w_mg, wbf, wbn, wout, nb=1, tt=tm_p)
    y_prompt = _ffn_call(x1, mod_p, g2, wup, wdn, nb=1, tt=tm_p)

    nb_s = min(DB, 32)
    (qf_s, frow_s, qn_s, nrow_s, wrow_s, _, gates_s, lf_s) = _pre_call(
        x_sample, mod_s, g1, w_p, bd, tri, gains, bff, nb=nb_s, tt=TN, do_cum=False)

    lf_new = lf_s[:, LOGF_LANE0:LOGF_LANE0 + N_HEADS].reshape(DB, 1, TN * N_HEADS)
    lf_new = jnp.pad(lf_new, ((0, 0), (0, 0), (0, LANES - TN * N_HEADS)))
    logf_flat = cache_fox_logf[0].reshape(n_phys, PAGE_SIZE * N_HEADS)
    c_past, c_new = _cum_s_call(page_table, logf_flat, lf_new)
    ck_s = c_past.reshape(DB, past, N_HEADS).transpose(0, 2, 1)
    cn_s = c_new[:, 0, :TN * N_HEADS].reshape(DB, TN, N_HEADS).transpose(0, 2, 1)
    cn_s = jnp.pad(cn_s, ((0, 0), (0, 0), (0, LANES - TN)))

    fox_pool = cache_fox_kv[0].reshape(n_phys, PAGE_SIZE, 256)
    o_fox_s = _fox_s_call(page_table, fox_pool, qf_s, frow_s, ck_s, cn_s, TN)

    nsa_pool = cache_nsa_kv[0].reshape(n_phys, PAGE_SIZE, 512)
    nc_s = past // CMP_STRIDE
    nsel_s = -(-(past + TN) // SEL_BLOCK)
    nsl = -(-nsel_s // LANES) * LANES
    a_s = jnp.asarray(_importance_matrix(nc_s, nsl, nc_s - 1) * (np.arange(nsl) < nsel_s)[None, :], BF16)
    oc_s, imp_s = _cmp_s_call(page_table, nsa_pool, qn_s, cw, a_s, TN)
    seln = _topk_s_call(imp_s.reshape(DB * N_GROUPS * TN, nsl), TN, past)
    kc_keys = min(past, 2048)
    bpc = kc_keys // SEL_BLOCK
    seln = seln.reshape(DB, N_GROUPS * TN, nsl // bpc, bpc).transpose(0, 2, 1, 3)
    e_s = jnp.asarray((np.arange(kc_keys)[None, :] // SEL_BLOCK) == np.arange(bpc)[:, None], BF16)
    win_buf = state_win_kv[0].reshape(DB, wlen, 256)
    o_nsa_s = _sel_s_call(page_table, nsa_pool, qn_s, nrow_s, wrow_s, win_buf, seln, oc_s, gates_s, e_s,
                          TN, kc_keys)
    x1_s = _mix_call(x_sample, o_fox_s, o_nsa_s, mod_s, g1, w_mg, wbf, wbn, wout, nb=nb_s, tt=TN)
    y_sample = _ffn_call(x1_s, mod_s, g2, wup, wdn, nb=nb_s, tt=TN)

    lf_p = lf[:, LOGF_LANE0:LOGF_LANE0 + N_HEADS]
    win_p = wrow.reshape(B, T, 256)[:, T - wlen:]
    win_s = jnp.concatenate([win_buf[:, TN:], wrow_s.reshape(DB, TN, 256)], axis=1)
    return (y_prompt, y_sample,
            frow.reshape(1, B, T, 2, N_GROUPS, HEAD_DIM),
            lf_p.reshape(1, B, T, N_HEADS),
            nrow.reshape(1, B, T, 4, N_GROUPS, HEAD_DIM),
            win_p.reshape(1, B, wlen, 2, N_GROUPS, HEAD_DIM),
            frow_s.reshape(1, DB, TN, 2, N_GROUPS, HEAD_DIM),
            lf_s[:, LOGF_LANE0:LOGF_LANE0 + N_HEADS].reshape(1, DB, TN, N_HEADS),
            nrow_s.reshape(1, DB, TN, 4, N_GROUPS, HEAD_DIM),
            win_s.reshape(1, DB, wlen, 2, N_GROUPS, HEAD_DIM))
```

```python
import functools

import numpy as np
import jax
import jax.numpy as jnp
from jax import lax
from jax.experimental import pallas as pl
from jax.experimental.pallas import tpu as pltpu

F32 = jnp.float32
BF16 = jnp.bfloat16

D_MODEL = 1024
HEAD_DIM = 64
N_HEADS = 8
N_GROUPS = 2
N_REP = N_HEADS // N_GROUPS
PAGE_SIZE = 128
CMP_STRIDE = 16
CMP_HIDDEN = 2 * HEAD_DIM
SEL_BLOCK = 64
N_SELECT = 16
WINDOW = 512
D_FF = 4 * D_MODEL
RMS_EPS = 1e-6
FORCE_BONUS = 1.0e4
LANES = 128
NEG = -1e30
MASK_BIG = 2.0 ** 100
GATE_LANE0 = 0
LOGF_LANE0 = 24
VMEM_LIMIT = 56 * 1024 * 1024


def _dot(a, b):
    return jnp.dot(a, b, preferred_element_type=F32)


def _dot_nt(a, b):
    return lax.dot_general(a, b, (((1,), (1,)), ((), ())), preferred_element_type=F32)


def _split3(x):
    x1 = x.astype(BF16)
    r = x - x1.astype(F32)
    x2 = r.astype(BF16)
    x3 = (r - x2.astype(F32)).astype(BF16)
    return x1, x2, x3


def _dot_exact_r(x, m):
    a, b, c = _split3(x)
    return _dot(a, m) + _dot(b, m) + _dot(c, m)


def _dot_exact_l(m, x):
    a, b, c = _split3(x)
    return _dot(m, a) + _dot(m, b) + _dot(m, c)


def _sigmoid(x):
    return 1.0 / (1.0 + jnp.exp(-x))


def _head_rms(zc, bd):
    a = zc * zc
    a1 = a.astype(BF16)
    a2 = (a - a1.astype(F32)).astype(BF16)
    ss = _dot(a1, bd) + _dot(a2, bd)
    return zc * lax.rsqrt(ss * (1.0 / HEAD_DIM) + RMS_EPS)


def _cparams(sem, vmem=VMEM_LIMIT):
    return pltpu.CompilerParams(dimension_semantics=sem, vmem_limit_bytes=vmem)


def _ada_kernel(c_ref, w_ref, b_ref, o_ref):
    c = c_ref[...]
    a = c * _sigmoid(c)
    o_ref[...] = _dot(a.astype(BF16), w_ref[...].astype(BF16)) + b_ref[...]


def _ada_call(c, w_ada, b_ada):
    r, d = c.shape
    n = w_ada.shape[1]
    tn = 1024
    return pl.pallas_call(
        _ada_kernel,
        grid=(n // tn,),
        in_specs=[pl.BlockSpec((r, d), lambda j: (0, 0)),
                  pl.BlockSpec((d, tn), lambda j: (0, j)),
                  pl.BlockSpec((1, tn), lambda j: (0, j))],
        out_specs=pl.BlockSpec((r, tn), lambda j: (0, j)),
        out_shape=jax.ShapeDtypeStruct((r, n), F32),
        compiler_params=_cparams(("arbitrary",)),
        name="ada_mod",
    )(c, w_ada, b_ada.reshape(1, n))


_C_FQ, _C_FK, _C_FV, _C_NQ, _C_NKV, _C_SMALL, _C_END = 0, 512, 640, 768, 1280, 2048, 2176


def _pre_kernel(x_ref, sc_ref, sh_ref, g1_ref, w_ref, bd_ref, tri_ref, gains_ref, bff_ref,
                qf_ref, frow_ref, qn_ref, nrow_ref, wrow_ref, kvb_ref, gate_ref, lf_ref,
                *rest, do_cum):
    nb, tt, d = x_ref.shape
    tm = nb * tt
    x = x_ref[...]
    ms = jnp.mean(x * x, axis=-1, keepdims=True)
    h = x * lax.rsqrt(ms + RMS_EPS) * g1_ref[...] * (1.0 + sc_ref[...]) + sh_ref[...]
    h = h.reshape(tm, d).astype(BF16)
    bd = bd_ref[...]

    def proj(c0):
        return _dot(h, w_ref[:, c0:c0 + LANES])

    scale = HEAD_DIM ** -0.5
    for r in range(N_REP):
        qf_ref[:, r * LANES:(r + 1) * LANES] = (
            _head_rms(proj(_C_FQ + r * LANES), bd) * (gains_ref[0:1, :] * scale)).astype(BF16)
        qn_ref[:, r * LANES:(r + 1) * LANES] = (
            _head_rms(proj(_C_NQ + r * LANES), bd) * (gains_ref[2:3, :] * scale)).astype(BF16)

    fk = _head_rms(proj(_C_FK), bd) * gains_ref[1:2, :]
    fv = proj(_C_FV)
    frow_ref[:, 0:LANES] = fk
    frow_ref[:, LANES:2 * LANES] = fv
    kvb_ref[:, 4 * LANES:5 * LANES] = fk.astype(BF16)
    kvb_ref[:, 5 * LANES:6 * LANES] = fv.astype(BF16)

    ck = proj(_C_NKV)
    cv = proj(_C_NKV + LANES)
    sk = _head_rms(proj(_C_NKV + 2 * LANES), bd) * gains_ref[3:4, :]
    sv = proj(_C_NKV + 3 * LANES)
    wk = _head_rms(proj(_C_NKV + 4 * LANES), bd) * gains_ref[4:5, :]
    wv = proj(_C_NKV + 5 * LANES)
    nrow_ref[:, 0:LANES] = ck
    nrow_ref[:, LANES:2 * LANES] = cv
    nrow_ref[:, 2 * LANES:3 * LANES] = sk
    nrow_ref[:, 3 * LANES:4 * LANES] = sv
    wrow_ref[:, 0:LANES] = wk
    wrow_ref[:, LANES:2 * LANES] = wv
    kvb_ref[:, 0:LANES] = sk.astype(BF16)
    kvb_ref[:, LANES:2 * LANES] = sv.astype(BF16)
    kvb_ref[:, 2 * LANES:3 * LANES] = wk.astype(BF16)
    kvb_ref[:, 3 * LANES:4 * LANES] = wv.astype(BF16)

    zl = proj(_C_SMALL)
    gate_ref[...] = _sigmoid(zl)
    xl = zl + bff_ref[...]
    lf = jnp.minimum(xl, 0.0) - jnp.log1p(jnp.exp(-jnp.abs(xl)))
    lf_ref[...] = lf

    if do_cum:
        c_ref, ct_ref, carry_ref = rest

        @pl.when(pl.program_id(1) == 0)
        def _():
            carry_ref[...] = jnp.zeros_like(carry_ref)

        c = _dot_exact_l(tri_ref[...], lf) + carry_ref[0:1, :]
        c_ref[...] = c
        carry_ref[...] = jnp.broadcast_to(c[tm - 1:tm, :], carry_ref.shape)
        ct_ref[...] = c.T[LOGF_LANE0:LOGF_LANE0 + N_HEADS, :]


def _pre_call(x3, mod3, g1, w_p, bd, tri, gains, bff, nb, tt, do_cum):
    NB, TT, d = x3.shape
    tm = nb * tt
    n = NB * TT
    gi, gj = NB // nb, TT // tt
    tok = lambda i, j: (i * gj + j, 0)
    full = lambda i, j: (0, 0)
    in_specs = [
        pl.BlockSpec((nb, tt, d), lambda i, j: (i, j, 0)),
        pl.BlockSpec((nb, 1, d), lambda i, j: (i, 0, 1)),
        pl.BlockSpec((nb, 1, d), lambda i, j: (i, 0, 0)),
        pl.BlockSpec((1, d), full),
        pl.BlockSpec(w_p.shape, full),
        pl.BlockSpec(bd.shape, full),
        pl.BlockSpec(tri.shape, full),
        pl.BlockSpec(gains.shape, full),
        pl.BlockSpec(bff.shape, full),
    ]
    out_shape = [
        jax.ShapeDtypeStruct((n, 512), BF16),
        jax.ShapeDtypeStruct((n, 256), F32),
        jax.ShapeDtypeStruct((n, 512), BF16),
        jax.ShapeDtypeStruct((n, 512), F32),
        jax.ShapeDtypeStruct((n, 256), F32),
        jax.ShapeDtypeStruct((n, 768), BF16),
        jax.ShapeDtypeStruct((n, LANES), F32),
        jax.ShapeDtypeStruct((n, LANES), F32),
    ]
    out_specs = [pl.BlockSpec((tm, s.shape[1]), tok) for s in out_shape]
    scratch = []
    if do_cum:
        out_shape += [jax.ShapeDtypeStruct((n, LANES), F32),
                      jax.ShapeDtypeStruct((N_HEADS, n), F32)]
        out_specs += [pl.BlockSpec((tm, LANES), tok),
                      pl.BlockSpec((N_HEADS, tm), lambda i, j: (0, i * gj + j))]
        scratch = [pltpu.VMEM((8, LANES), F32)]
    return pl.pallas_call(
        functools.partial(_pre_kernel, do_cum=do_cum),
        grid=(gi, gj),
        in_specs=in_specs,
        out_specs=out_specs,
        out_shape=out_shape,
        scratch_shapes=scratch,
        compiler_params=_cparams(("arbitrary", "arbitrary")),
        name="pre_mixer",
    )(x3, mod3, mod3, g1, w_p, bd, tri, gains, bff)


def _flash_step(qp, k, v, bias, m, l, acc):
    s = _dot_nt(qp, k) + bias
    m_new = jnp.maximum(m, jnp.max(s, axis=-1, keepdims=True))
    alpha = jnp.exp(m - m_new)
    p = jnp.exp(s - m_new)
    l = alpha * l + jnp.sum(p, axis=-1, keepdims=True)
    acc = alpha * acc + _dot(p.astype(BF16), v)
    return m_new, l, acc


def _flash_init(rows):
    return (jnp.full((rows, 1), NEG, F32), jnp.zeros((rows, 1), F32), jnp.zeros((rows, LANES), F32))


def _half_mask(rows, g):
    lane = lax.broadcasted_iota(jnp.int32, (rows, LANES), 1)
    return (lane < HEAD_DIM) if g == 0 else (lane >= HEAD_DIM)


def _slope(h):
    return 2.0 ** (-8.0 * (h + 1) / N_HEADS)


def _select_blocks(imp, cur):
    blk = lax.broadcasted_iota(jnp.int32, imp.shape, 1)
    forced = (blk == 0) | (blk == cur) | (blk == cur - 1)
    score = jnp.where(blk <= cur, imp + jnp.where(forced, FORCE_BONUS, 0.0), -jnp.inf)
    blkf = blk.astype(F32)

    def body(_, carry):
        sc, sel = carry
        mx = jnp.max(sc, axis=-1, keepdims=True)
        idx = jnp.min(jnp.where(sc == mx, blkf, 1e9), axis=-1, keepdims=True)
        pick = blkf == idx
        return jnp.where(pick, -jnp.inf, sc), jnp.where(pick, 1.0, sel)

    _, sel = lax.fori_loop(0, N_SELECT, body, (score, jnp.zeros(imp.shape, F32)))
    return jnp.where(blk <= cur, sel, 0.0)


def _importance_matrix(n_cmp_rows, n_sel_cols, n_cmp):
    r = SEL_BLOCK // CMP_STRIDE
    a = np.zeros((n_cmp_rows, n_sel_cols), np.float32)
    for n in range(n_cmp):
        for j in range(n_sel_cols):
            off = n - r * j
            if off in (-1, r - 1):
                a[n, j] = 0.5
            elif 0 <= off <= r - 2:
                a[n, j] = 1.0
    return a


def _fox_p_kernel(q_ref, kv_ref, c_ref, ct_ref, o_ref, *, tq):
    qi = pl.program_id(1)
    rows = lax.broadcasted_iota(jnp.int32, (tq, tq), 0)
    cols = lax.broadcasted_iota(jnp.int32, (tq, tq), 1)
    causal = cols <= rows
    for r in range(N_REP):
        qc = q_ref[:, r * LANES:(r + 1) * LANES]
        outs = []
        for g in range(N_GROUPS):
            h = g * N_REP + r
            qp = jnp.where(_half_mask(tq, g), qc, jnp.zeros_like(qc))
            cq = c_ref[:, LOGF_LANE0 + h:LOGF_LANE0 + h + 1]

            def tile(j, h=h):
                st = pl.multiple_of(j * tq, tq)
                return (kv_ref[pl.ds(st, tq), 0:LANES], kv_ref[pl.ds(st, tq), LANES:2 * LANES],
                        ct_ref[h:h + 1, pl.ds(st, tq)])

            k, v, ck = tile(qi)
            carry = _flash_step(qp, k, v, jnp.where(causal, cq - ck, NEG), *_flash_init(tq))

            def body(j, carry, qp=qp, cq=cq, tile=tile):
                k, v, ck = tile(j)
                return _flash_step(qp, k, v, cq - ck, *carry)

            m, l, acc = lax.fori_loop(0, qi, body, carry)
            outs.append(acc / l)
        o_ref[:, r * LANES:(r + 1) * LANES] = jnp.where(_half_mask(tq, 0), outs[0], outs[1]).astype(BF16)


def _fox_p_call(qf, kvb, c_tm, ct, B, T, tq):
    n = B * T
    nq = T // tq
    return pl.pallas_call(
        functools.partial(_fox_p_kernel, tq=tq),
        grid=(B, nq),
        in_specs=[pl.BlockSpec((tq, 512), lambda b, i: (b * nq + i, 0)),
                  pl.BlockSpec((T, 256), lambda b, i: (b, 2)),
                  pl.BlockSpec((tq, LANES), lambda b, i: (b * nq + i, 0)),
                  pl.BlockSpec((N_HEADS, T), lambda b, i: (0, b))],
        out_specs=pl.BlockSpec((tq, 512), lambda b, i: (b * nq + i, 0)),
        out_shape=jax.ShapeDtypeStruct((n, 512), BF16),
        compiler_params=_cparams(("arbitrary", "arbitrary")),
        name="fox_prompt",
    )(qf, kvb, c_tm, ct)


def _compress(xk_ref, xv_ref, n, w1k_ref, w1v_ref, posk_ref, posv_ref, w2k_ref, w2v_ref, bd, gk):
    acc_k = jnp.zeros((n, 4 * LANES), F32)
    acc_v = jnp.zeros((n, 4 * LANES), F32)
    ck = jnp.zeros((8, 4 * LANES), F32)
    cv = jnp.zeros((8, 4 * LANES), F32)
    def rows(x_ref, c):
        return x_ref[pl.ds(c, n, stride=CMP_STRIDE), :]

    for cp in range(CMP_STRIDE // 2):
        xk = jnp.concatenate([rows(xk_ref, 2 * cp), rows(xk_ref, 2 * cp + 1)], axis=1)
        xv = jnp.concatenate([rows(xv_ref, 2 * cp), rows(xv_ref, 2 * cp + 1)], axis=1)
        acc_k = acc_k + _dot(xk.astype(BF16), w1k_ref[cp])
        acc_v = acc_v + _dot(xv.astype(BF16), w1v_ref[cp])
        ck = ck + _dot(posk_ref[cp], w1k_ref[cp])
        cv = cv + _dot(posv_ref[cp], w1v_ref[cp])

    def finish(acc, cst, w2_ref):
        lead = acc[:, 0:2 * LANES] + cst[0:1, 0:2 * LANES]
        trail = acc[:, 2 * LANES:4 * LANES] + cst[1:2, 2 * LANES:4 * LANES]
        hid = lead + pltpu.roll(trail, n - 1, 0)
        act = hid * _sigmoid(hid)
        return _dot(act.astype(BF16), w2_ref[...])

    kc = _head_rms(finish(acc_k, ck, w2k_ref), bd) * gk
    vc = finish(acc_v, cv, w2v_ref)
    return kc, vc


def _cmp_p_kernel(xk_ref, xv_ref, w1k_ref, w1v_ref, posk_ref, posv_ref, w2k_ref, w2v_ref, bd_ref, gk_ref,
                  kc_ref, vc_ref):
    n = kc_ref.shape[0]
    kc, vc = _compress(xk_ref, xv_ref, n, w1k_ref, w1v_ref, posk_ref, posv_ref, w2k_ref, w2v_ref,
                       bd_ref[...], gk_ref[...])
    kc_ref[...] = kc.astype(BF16)
    vc_ref[...] = vc.astype(BF16)


def _cmp_p_call(nrow, cw, B, T):
    nc = T // CMP_STRIDE
    consts = [cw["w1k"], cw["w1v"], cw["posk"], cw["posv"], cw["w2k"], cw["w2v"], cw["bd"], cw["gk"]]
    cspecs = [pl.BlockSpec(a.shape, (lambda b, nd=a.ndim: (0,) * nd)) for a in consts]
    return pl.pallas_call(
        _cmp_p_kernel,
        grid=(B,),
        in_specs=[pl.BlockSpec((T, LANES), lambda b: (b, 0)),
                  pl.BlockSpec((T, LANES), lambda b: (b, 1))] + cspecs,
        out_specs=[pl.BlockSpec((nc, LANES), lambda b: (b, 0))] * 2,
        out_shape=[jax.ShapeDtypeStruct((B * nc, LANES), BF16)] * 2,
        compiler_params=_cparams(("arbitrary",)),
        name="compress_prompt",
    )(nrow, nrow, *consts)


def _nsa_p_kernel(q_ref, kv_ref, kc_ref, vc_ref, gate_ref, a_ref, e_ref, o_ref, maskb_ref, *, tq, T):
    qi = pl.program_id(1)
    t0 = qi * tq
    nc = kc_ref.shape[0]
    rows = lax.broadcasted_iota(jnp.int32, (tq, tq), 0)
    cols = lax.broadcasted_iota(jnp.int32, (tq, tq), 1)
    causal = cols <= rows
    rc = (rows - cols).astype(F32)
    tpos = t0 + lax.broadcasted_iota(jnp.int32, (tq, 1), 0)

    nid = lax.broadcasted_iota(jnp.int32, (tq, nc), 1)
    dc = tpos - (nid * CMP_STRIDE + (2 * CMP_STRIDE - 1))
    cvalid = (dc >= 0) & (nid < nc - 1)
    dcf = dc.astype(F32)
    kc = kc_ref[...]
    vc = vc_ref[...]
    psum = [jnp.zeros((tq, nc), F32) for _ in range(N_GROUPS)]
    o_c = {}
    for r in range(N_REP):
        qc = q_ref[:, r * LANES:(r + 1) * LANES]
        for g in range(N_GROUPS):
            h = g * N_REP + r
            qp = jnp.where(_half_mask(tq, g), qc, jnp.zeros_like(qc))
            s = jnp.where(cvalid, _dot_nt(qp, kc) - _slope(h) * dcf, NEG)
            mx = jnp.max(s, axis=-1, keepdims=True)
            p = jnp.where(cvalid, jnp.exp(s - mx), 0.0)
            den = jnp.sum(p, axis=-1, keepdims=True)
            p = p / jnp.where(den > 0, den, 1.0)
            psum[g] = psum[g] + p
            o_c[h] = _dot(p.astype(BF16), vc)

    cur = tpos // SEL_BLOCK
    for g in range(N_GROUPS):
        imp = _dot_exact_r(psum[g], a_ref[...])
        sel = _select_blocks(imp, cur)
        maskb_ref[g] = _dot(((sel - 1.0) * MASK_BIG).astype(BF16), e_ref[...])

    for r in range(N_REP):
        qc = q_ref[:, r * LANES:(r + 1) * LANES]
        outs = []
        for g in range(N_GROUPS):
            h = g * N_REP + r
            slope = _slope(h)
            qp = jnp.where(_half_mask(tq, g), qc, jnp.zeros_like(qc))

            def kv_tile(j, c0):
                st = pl.multiple_of(j * tq, tq)
                return kv_ref[pl.ds(st, tq), c0:c0 + LANES], kv_ref[pl.ds(st, tq), c0 + LANES:c0 + 2 * LANES]

            k, v = kv_tile(qi, 0)
            mb = maskb_ref[g, :, pl.ds(pl.multiple_of(t0, tq), tq)]
            carry = _flash_step(qp, k, v, jnp.where(causal, mb - slope * rc, NEG), *_flash_init(tq))

            def sel_body(j, carry, qp=qp, g=g, slope=slope):
                k, v = kv_tile(j, 0)
                mb = maskb_ref[g, :, pl.ds(pl.multiple_of(j * tq, tq), tq)]
                dist = rc + (t0 - j * tq).astype(F32)
                return _flash_step(qp, k, v, mb - slope * dist, *carry)

            m, l, acc = lax.fori_loop(0, qi, sel_body, carry)
            o_s = acc / l

            k, v = kv_tile(qi, 2 * LANES)
            carry = _flash_step(qp, k, v, jnp.where(causal, -slope * rc, NEG), *_flash_init(tq))

            def win_body(j, carry, qp=qp, slope=slope):
                k, v = kv_tile(j, 2 * LANES)
                dist = rc + (t0 - j * tq).astype(F32)
                return _flash_step(qp, k, v, jnp.where(dist < WINDOW, -slope * dist, NEG), *carry)

            lo = jnp.maximum(qi - (WINDOW + tq - 1) // tq, 0)
            m, l, acc = lax.fori_loop(lo, qi, win_body, carry)
            o_w = acc / l

            gl = GATE_LANE0 + 3 * h
            outs.append(gate_ref[:, gl:gl + 1] * o_c[h] + gate_ref[:, gl + 1:gl + 2] * o_s
                        + gate_ref[:, gl + 2:gl + 3] * o_w)
        o_ref[:, r * LANES:(r + 1) * LANES] = jnp.where(_half_mask(tq, 0), outs[0], outs[1]).astype(BF16)


def _nsa_p_call(qn, kvb, kc, vc, gates, a_mat, e_mat, B, T, tq):
    n = B * T
    nq = T // tq
    nc = T // CMP_STRIDE
    return pl.pallas_call(
        functools.partial(_nsa_p_kernel, tq=tq, T=T),
        grid=(B, nq),
        in_specs=[pl.BlockSpec((tq, 512), lambda b, i: (b * nq + i, 0)),
                  pl.BlockSpec((T, 512), lambda b, i: (b, 0)),
                  pl.BlockSpec((nc, LANES), lambda b, i: (b, 0)),
                  pl.BlockSpec((nc, LANES), lambda b, i: (b, 0)),
                  pl.BlockSpec((tq, LANES), lambda b, i: (b * nq + i, 0)),
                  pl.BlockSpec(a_mat.shape, lambda b, i: (0, 0)),
                  pl.BlockSpec(e_mat.shape, lambda b, i: (0, 0))],
        out_specs=pl.BlockSpec((tq, 512), lambda b, i: (b * nq + i, 0)),
        out_shape=jax.ShapeDtypeStruct((n, 512), BF16),
        scratch_shapes=[pltpu.VMEM((N_GROUPS, tq, T), F32)],
        compiler_params=_cparams(("arbitrary", "arbitrary")),
        name="nsa_prompt",
    )(qn, kvb, kc, vc, gates, a_mat, e_mat)


def _mix_kernel(x_ref, of_ref, on_ref, sc_ref, sh_ref, gt_ref, g1_ref, wmg_ref, wbf_ref, wbn_ref,
                wout_ref, o_ref):
    nb, tt, d = x_ref.shape
    tm = nb * tt
    x = x_ref[...]
    ms = jnp.mean(x * x, axis=-1, keepdims=True)
    h = x * lax.rsqrt(ms + RMS_EPS) * g1_ref[...] * (1.0 + sc_ref[...]) + sh_ref[...]
    h = h.reshape(tm, d).astype(BF16)
    g_fox = _sigmoid(_dot(h, wmg_ref[:, 0:d]))
    g_nsa = _sigmoid(_dot(h, wmg_ref[:, d:2 * d]))
    mix = g_fox * _dot(of_ref[...], wbf_ref[...]) + g_nsa * _dot(on_ref[...], wbn_ref[...])
    y = _dot(mix.astype(BF16), wout_ref[...]).reshape(nb, tt, d)
    o_ref[...] = x + gt_ref[...] * y


def _mix_call(x3, o_fox, o_nsa, mod3, g1, wmg, wbf, wbn, wout, nb, tt):
    NB, TT, d = x3.shape
    tm = nb * tt
    gi, gj = NB // nb, TT // tt
    full = lambda i, j: (0, 0)
    tok = lambda i, j: (i * gj + j, 0)
    return pl.pallas_call(
        _mix_kernel,
        grid=(gi, gj),
        in_specs=[pl.BlockSpec((nb, tt, d), lambda i, j: (i, j, 0)),
                  pl.BlockSpec((tm, 512), tok),
                  pl.BlockSpec((tm, 512), tok),
                  pl.BlockSpec((nb, 1, d), lambda i, j: (i, 0, 1)),
                  pl.BlockSpec((nb, 1, d), lambda i, j: (i, 0, 0)),
                  pl.BlockSpec((nb, 1, d), lambda i, j: (i, 0, 2)),
                  pl.BlockSpec((1, d), full),
                  pl.BlockSpec(wmg.shape, full),
                  pl.BlockSpec(wbf.shape, full),
                  pl.BlockSpec(wbn.shape, full),
                  pl.BlockSpec(wout.shape, full)],
        out_specs=pl.BlockSpec((nb, tt, d), lambda i, j: (i, j, 0)),
        out_shape=jax.ShapeDtypeStruct((NB, TT, d), F32),
        compiler_params=_cparams(("arbitrary", "arbitrary")),
        name="post_mix",
    )(x3, o_fox, o_nsa, mod3, mod3, mod3, g1, wmg, wbf, wbn, wout)


def _ffn_kernel(x_ref, sc_ref, sh_ref, gt_ref, g2_ref, wup_ref, wdn_ref, o_ref, *, fc):
    nb, tt, d = x_ref.shape
    tm = nb * tt
    x = x_ref[...]
    ms = jnp.mean(x * x, axis=-1, keepdims=True)
    h = x * lax.rsqrt(ms + RMS_EPS) * g2_ref[...] * (1.0 + sc_ref[...]) + sh_ref[...]
    h = h.reshape(tm, d).astype(BF16)
    acc = jnp.zeros((tm, d), F32)
    for c in range(D_FF // fc):
        u = jnp.maximum(_dot(h, wup_ref[:, c * fc:(c + 1) * fc]), 0.0)
        acc = acc + _dot((u * u).astype(BF16), wdn_ref[c * fc:(c + 1) * fc, :])
    o_ref[...] = x + gt_ref[...] * acc.reshape(nb, tt, d)


def _ffn_call(x3, mod3, g2, wup, wdn, nb, tt):
    NB, TT, d = x3.shape
    gi, gj = NB // nb, TT // tt
    full = lambda i, j: (0, 0)
    return pl.pallas_call(
        functools.partial(_ffn_kernel, fc=1024),
        grid=(gi, gj),
        in_specs=[pl.BlockSpec((nb, tt, d), lambda i, j: (i, j, 0)),
                  pl.BlockSpec((nb, 1, d), lambda i, j: (i, 0, 4)),
                  pl.BlockSpec((nb, 1, d), lambda i, j: (i, 0, 3)),
                  pl.BlockSpec((nb, 1, d), lambda i, j: (i, 0, 5)),
                  pl.BlockSpec((1, d), full),
                  pl.BlockSpec(wup.shape, full, pipeline_mode=pl.Buffered(1)),
                  pl.BlockSpec(wdn.shape, full, pipeline_mode=pl.Buffered(1))],
        out_specs=pl.BlockSpec((nb, tt, d), lambda i, j: (i, j, 0)),
        out_shape=jax.ShapeDtypeStruct((NB, TT, d), F32),
        compiler_params=_cparams(("arbitrary", "arbitrary")),
        name="ffn",
    )(x3, mod3, mod3, mod3, g2, wup, wdn)


def _page_copy(pool_ref, page, row0, buf_ref, slot, p, sem_ref):
    return pltpu.make_async_copy(
        pool_ref.at[page, pl.ds(row0, buf_ref.shape[1]), :],
        buf_ref.at[slot, :, pl.ds(pl.multiple_of(p * PAGE_SIZE, PAGE_SIZE), PAGE_SIZE)],
        sem_ref.at[slot])


def _gather_start(pt_ref, b, parts, slot, n_pages):
    def body(p, _):
        page = pt_ref[b, p]
        for pool_ref, row0, buf_ref, sem_ref in parts:
            _page_copy(pool_ref, page, row0, buf_ref, slot, p, sem_ref).start()
        return 0
    lax.fori_loop(0, n_pages, body, 0)


def _gather_wait(parts, slot, n_pages):
    def body(p, _):
        for pool_ref, row0, buf_ref, sem_ref in parts:
            _page_copy(pool_ref, 0, row0, buf_ref, slot, p, sem_ref).wait()
        return 0
    lax.fori_loop(0, n_pages, body, 0)


def _gather_pipeline(pt_ref, parts, n_pages):
    b = pl.program_id(0)
    nb = pl.num_programs(0)
    slot = lax.rem(b, 2)

    @pl.when(b == 0)
    def _():
        _gather_start(pt_ref, 0, parts, 0, n_pages)

    @pl.when(b + 1 < nb)
    def _():
        _gather_start(pt_ref, b + 1, parts, 1 - slot, n_pages)

    _gather_wait(parts, slot, n_pages)
    return slot


def _lane_cumsum(x, n):
    lane = lax.broadcasted_iota(jnp.int32, x.shape, 1)
    s = 1
    while s < n:
        x = x + jnp.where(lane >= s, pltpu.roll(x, s, 1), 0.0)
        s *= 2
    return x


def _stack_q(q_ref, tn):
    parts = []
    for g in range(N_GROUPS):
        for r in range(N_REP):
            qc = q_ref[:, r * LANES:(r + 1) * LANES]
            parts.append(jnp.where(_half_mask(tn, g), qc, jnp.zeros_like(qc)))
    return jnp.concatenate(parts, axis=0)


def _unstack_o(o, tn):
    chunks = []
    for r in range(N_REP):
        a = o[r * tn:(r + 1) * tn]
        b = o[(N_REP + r) * tn:(N_REP + r + 1) * tn]
        chunks.append(jnp.where(_half_mask(tn, 0), a, b))
    return jnp.concatenate(chunks, axis=1)


def _per_head_rows(x8, tn):
    n = x8.shape[1]
    return jnp.broadcast_to(x8[:, None, :], (N_HEADS, tn, n)).reshape(N_HEADS * tn, n)


def _slope_col(tn):
    hrow = lax.broadcasted_iota(jnp.int32, (N_HEADS * tn, 1), 0) // tn
    col = jnp.zeros((N_HEADS * tn, 1), F32)
    for h in range(N_HEADS):
        col = jnp.where(hrow == h, _slope(h), col)
    return col


def _tok_col(tn):
    return lax.rem(lax.broadcasted_iota(jnp.int32, (N_HEADS * tn, 1), 0), tn)


def _pad_keys(x, dtype):
    tn = x.shape[0]
    return jnp.concatenate([x, jnp.zeros((LANES - tn, LANES), x.dtype)], axis=0).astype(dtype)


def _flash_step_t(qp, kt, vt, bias, m, l, acc):
    s = _dot(qp, kt) + bias
    m_new = jnp.maximum(m, jnp.max(s, axis=-1, keepdims=True))
    alpha = jnp.exp(m - m_new)
    p = jnp.exp(s - m_new)
    l = alpha * l + jnp.sum(p, axis=-1, keepdims=True)
    acc = alpha * acc + _dot_nt(p.astype(BF16), vt)
    return m_new, l, acc


def _fox_s_kernel(pt_ref, pool_ref, lpool_ref, q_ref, new_ref, lfn_ref, o_ref,
                  kt_ref, vt_ref, lf_ref, sem_ref, lsem_ref, *, n_pages, tn, kc):
    parts = [(pool_ref, 0, kt_ref, sem_ref), (pool_ref, LANES, vt_ref, sem_ref),
             (lpool_ref, 0, lf_ref, lsem_ref)]
    slot = _gather_pipeline(pt_ref, parts, n_pages)
    past = n_pages * PAGE_SIZE
    R = N_HEADS * tn
    qs = _stack_q(q_ref, tn)
    cpast = _lane_cumsum(lf_ref[slot], past)
    cn = _lane_cumsum(lfn_ref[0], tn) + cpast[:, past - 1:past]
    tcol = _tok_col(tn)
    tl = lax.broadcasted_iota(jnp.int32, (R, LANES), 1)
    cn_rows = _per_head_rows(cn, tn)
    cq = jnp.sum(jnp.where(tl == tcol, cn_rows, 0.0), axis=-1, keepdims=True)
    knew = _pad_keys(new_ref[:, 0:LANES], BF16)
    vnew = _pad_keys(new_ref[:, LANES:2 * LANES], BF16)
    bias = jnp.where(tl <= tcol, cq - cn_rows, NEG)
    carry = _flash_step(qs, knew, vnew, bias, *_flash_init(R))
    for c in range(past // kc):
        kt = kt_ref[slot, :, c * kc:(c + 1) * kc].astype(BF16)
        vt = vt_ref[slot, :, c * kc:(c + 1) * kc].astype(BF16)
        ck = _per_head_rows(cpast[:, c * kc:(c + 1) * kc], tn)
        carry = _flash_step_t(qs, kt, vt, cq - ck, *carry)
    m, l, acc = carry
    o_ref[...] = _unstack_o(acc / l, tn).astype(BF16)


def _fox_s_call(page_table, fox_t, logf_t, qf, frow, lf_new, tn):
    B, n_pages = page_table.shape
    past = n_pages * PAGE_SIZE
    kc = min(past, 2048)
    grid_spec = pltpu.PrefetchScalarGridSpec(
        num_scalar_prefetch=1,
        grid=(B,),
        in_specs=[pl.BlockSpec(memory_space=pl.ANY),
                  pl.BlockSpec(memory_space=pl.ANY),
                  pl.BlockSpec((tn, 512), lambda b, pt: (b, 0)),
                  pl.BlockSpec((tn, 256), lambda b, pt: (b, 0)),
                  pl.BlockSpec((1, N_HEADS, LANES), lambda b, pt: (b, 0, 0))],
        out_specs=pl.BlockSpec((tn, 512), lambda b, pt: (b, 0)),
        scratch_shapes=[pltpu.VMEM((2, LANES, past), F32), pltpu.VMEM((2, LANES, past), F32),
                        pltpu.VMEM((2, N_HEADS, past), F32),
                        pltpu.SemaphoreType.DMA((2,)), pltpu.SemaphoreType.DMA((2,))],
    )
    return pl.pallas_call(
        functools.partial(_fox_s_kernel, n_pages=n_pages, tn=tn, kc=kc),
        grid_spec=grid_spec,
        out_shape=jax.ShapeDtypeStruct((B * tn, 512), BF16),
        compiler_params=_cparams(("arbitrary",)),
        name="fox_sample",
    )(page_table, fox_t, logf_t, qf, frow, lf_new)


def _cmp_s_kernel(pt_ref, pool_ref, q_ref, w1k_ref, w1v_ref, posk_ref, posv_ref, w2k_ref, w2v_ref,
                  bd_ref, gk_ref, a_ref, oc_ref, imp_ref, kt_ref, vt_ref, xk_ref, xv_ref, sem_ref,
                  *, n_pages, tn):
    parts = [(pool_ref, 0, kt_ref, sem_ref), (pool_ref, LANES, vt_ref, sem_ref)]
    slot = _gather_pipeline(pt_ref, parts, n_pages)
    past = n_pages * PAGE_SIZE
    nc = past // CMP_STRIDE
    R = N_HEADS * tn

    def to_token_major(p, _):
        st = pl.multiple_of(p * PAGE_SIZE, PAGE_SIZE)
        xk_ref[pl.ds(st, PAGE_SIZE), :] = kt_ref[slot, :, pl.ds(st, PAGE_SIZE)].T
        xv_ref[pl.ds(st, PAGE_SIZE), :] = vt_ref[slot, :, pl.ds(st, PAGE_SIZE)].T
        return 0

    lax.fori_loop(0, n_pages, to_token_major, 0)
    kc, vc = _compress(xk_ref, xv_ref, nc, w1k_ref, w1v_ref, posk_ref, posv_ref,
                       w2k_ref, w2v_ref, bd_ref[...], gk_ref[...])
    qs = _stack_q(q_ref, tn)
    nid = lax.broadcasted_iota(jnp.int32, (R, nc), 1)
    tpos = past + _tok_col(tn)
    dc = tpos - (nid * CMP_STRIDE + (2 * CMP_STRIDE - 1))
    valid = (dc >= 0) & (nid < nc - 1)
    s = jnp.where(valid, _dot_nt(qs, kc.astype(BF16)) - _slope_col(tn) * dc.astype(F32), NEG)
    mx = jnp.max(s, axis=-1, keepdims=True)
    p = jnp.where(valid, jnp.exp(s - mx), 0.0)
    den = jnp.sum(p, axis=-1, keepdims=True)
    p = p / jnp.where(den > 0, den, 1.0)
    oc_ref[0] = _dot(p.astype(BF16), vc.astype(BF16))
    p4 = p.reshape(N_GROUPS, N_REP, tn, nc)
    psum = (p4[:, 0] + p4[:, 1] + p4[:, 2] + p4[:, 3]).reshape(N_GROUPS * tn, nc)
    imp_ref[0] = _dot_exact_r(psum, a_ref[...])


def _cmp_s_call(page_table, nsa_pool, qn, cw, a_mat, tn):
    B, n_pages = page_table.shape
    past = n_pages * PAGE_SIZE
    consts = [cw["w1k"], cw["w1v"], cw["posk"], cw["posv"], cw["w2k"], cw["w2v"], cw["bd"], cw["gk"], a_mat]
    cspecs = [pl.BlockSpec(a.shape, (lambda b, pt, nd=a.ndim: (0,) * nd)) for a in consts]
    nsl = a_mat.shape[1]
    grid_spec = pltpu.PrefetchScalarGridSpec(
        num_scalar_prefetch=1,
        grid=(B,),
        in_specs=[pl.BlockSpec(memory_space=pl.ANY),
                  pl.BlockSpec((tn, 512), lambda b, pt: (b, 0))] + cspecs,
        out_specs=[pl.BlockSpec((1, N_HEADS * tn, LANES), lambda b, pt: (b, 0, 0)),
                   pl.BlockSpec((1, N_GROUPS * tn, nsl), lambda b, pt: (b, 0, 0))],
        scratch_shapes=[pltpu.VMEM((2, LANES, past), F32), pltpu.VMEM((2, LANES, past), F32),
                        pltpu.VMEM((past, LANES), F32), pltpu.VMEM((past, LANES), F32),
                        pltpu.SemaphoreType.DMA((2,))],
    )
    return pl.pallas_call(
        functools.partial(_cmp_s_kernel, n_pages=n_pages, tn=tn),
        grid_spec=grid_spec,
        out_shape=[jax.ShapeDtypeStruct((B, N_HEADS * tn, LANES), F32),
                   jax.ShapeDtypeStruct((B, N_GROUPS * tn, nsl), F32)],
        compiler_params=_cparams(("arbitrary",)),
        name="compress_sample",
    )(page_table, nsa_pool, qn, *consts)


def _topk_s_kernel(imp_ref, o_ref, *, tn, past):
    rows = imp_ref.shape[0]
    t = past + lax.rem(lax.broadcasted_iota(jnp.int32, (rows, 1), 0), tn)
    sel = _select_blocks(imp_ref[...], t // SEL_BLOCK)
    o_ref[...] = ((sel - 1.0) * MASK_BIG).astype(BF16)


def _topk_s_call(imp2, tn, past):
    rows, nsl = imp2.shape
    tr = min(rows, 256)
    return pl.pallas_call(
        functools.partial(_topk_s_kernel, tn=tn, past=past),
        grid=(rows // tr,),
        in_specs=[pl.BlockSpec((tr, nsl), lambda i: (i, 0))],
        out_specs=pl.BlockSpec((tr, nsl), lambda i: (i, 0)),
        out_shape=jax.ShapeDtypeStruct((rows, nsl), BF16),
        compiler_params=_cparams(("arbitrary",)),
        name="select_sample",
    )(imp2)


def _sel_s_kernel(pt_ref, pool_ref, q_ref, nnew_ref, wnew_ref, win_ref, seln_ref, oc_ref, gate_ref, e_ref,
                  o_ref, kt_ref, vt_ref, sem_ref, *, n_pages, tn, kc):
    parts = [(pool_ref, 2 * LANES, kt_ref, sem_ref), (pool_ref, 3 * LANES, vt_ref, sem_ref)]
    slot = _gather_pipeline(pt_ref, parts, n_pages)
    past = n_pages * PAGE_SIZE
    R = N_HEADS * tn
    bpc = kc // SEL_BLOCK
    qs = _stack_q(q_ref, tn)
    tcol = _tok_col(tn)
    slope = _slope_col(tn)
    tl = lax.broadcasted_iota(jnp.int32, (R, LANES), 1)
    newbias = jnp.where(tl <= tcol, -slope * (tcol - tl).astype(F32), NEG)

    def group_rows(x):
        n = x.shape[1]
        x4 = jnp.broadcast_to(x.reshape(N_GROUPS, 1, tn, n), (N_GROUPS, N_REP, tn, n))
        return x4.reshape(R, n)

    n_chunks = past // kc
    seln_new = seln_ref[0, n_chunks]
    mb_new = group_rows(_dot(seln_new, e_ref[...])[:, 0:LANES])
    knew = _pad_keys(nnew_ref[:, 2 * LANES:3 * LANES], BF16)
    vnew = _pad_keys(nnew_ref[:, 3 * LANES:4 * LANES], BF16)
    carry = _flash_step(qs, knew, vnew, jnp.where(tl <= tcol, newbias + mb_new, NEG), *_flash_init(R))
    for c in range(n_chunks):
        kt = kt_ref[slot, :, c * kc:(c + 1) * kc].astype(BF16)
        vt = vt_ref[slot, :, c * kc:(c + 1) * kc].astype(BF16)
        mb = group_rows(_dot(seln_ref[0, c], e_ref[...]))
        kpos = c * kc + lax.broadcasted_iota(jnp.int32, (1, kc), 1)
        dist = ((past + tcol) - kpos).astype(F32)
        carry = _flash_step_t(qs, kt, vt, mb - slope * dist, *carry)
    m, l, acc = carry
    o_s = acc / l

    wlen = win_ref.shape[2]
    knew = _pad_keys(wnew_ref[:, 0:LANES], BF16)
    vnew = _pad_keys(wnew_ref[:, LANES:2 * LANES], BF16)
    carry = _flash_step(qs, knew, vnew, newbias, *_flash_init(R))
    kt = win_ref[0, 0:LANES, :].astype(BF16)
    vt = win_ref[0, LANES:2 * LANES, :].astype(BF16)
    wpos = (past - wlen) + lax.broadcasted_iota(jnp.int32, (1, wlen), 1)
    dw = (past + tcol) - wpos
    bias = jnp.where((dw < WINDOW) & (wpos >= 0), -slope * dw.astype(F32), NEG)
    m, l, acc = _flash_step_t(qs, kt, vt, bias, *carry)
    o_w = acc / l

    gl = lax.broadcasted_iota(jnp.int32, (R, LANES), 1)
    hrow = lax.broadcasted_iota(jnp.int32, (R, LANES), 0) // tn
    gt = jnp.broadcast_to(gate_ref[...][None], (N_HEADS, tn, LANES)).reshape(R, LANES)

    def gcol(j):
        return jnp.sum(jnp.where(gl == GATE_LANE0 + 3 * hrow + j, gt, 0.0), axis=-1, keepdims=True)

    o = gcol(0) * oc_ref[0] + gcol(1) * o_s + gcol(2) * o_w
    o_ref[...] = _unstack_o(o, tn).astype(BF16)


def _sel_s_call(page_table, nsa_pool, qn, nrow, wrow, win_t, seln, oc, gates, e_mat, tn, kc):
    B, n_pages = page_table.shape
    past = n_pages * PAGE_SIZE
    wlen = win_t.shape[2]
    grid_spec = pltpu.PrefetchScalarGridSpec(
        num_scalar_prefetch=1,
        grid=(B,),
        in_specs=[pl.BlockSpec(memory_space=pl.ANY),
                  pl.BlockSpec((tn, 512), lambda b, pt: (b, 0)),
                  pl.BlockSpec((tn, 512), lambda b, pt: (b, 0)),
                  pl.BlockSpec((tn, 256), lambda b, pt: (b, 0)),
                  pl.BlockSpec((1, 2 * LANES, wlen), lambda b, pt: (b, 0, 0)),
                  pl.BlockSpec((1,) + seln.shape[1:], lambda b, pt: (b, 0, 0, 0)),
                  pl.BlockSpec((1, N_HEADS * tn, LANES), lambda b, pt: (b, 0, 0)),
                  pl.BlockSpec((tn, LANES), lambda b, pt: (b, 0)),
                  pl.BlockSpec(e_mat.shape, lambda b, pt: (0, 0))],
        out_specs=pl.BlockSpec((tn, 512), lambda b, pt: (b, 0)),
        scratch_shapes=[pltpu.VMEM((2, LANES, past), F32), pltpu.VMEM((2, LANES, past), F32),
                        pltpu.SemaphoreType.DMA((2,))],
    )
    return pl.pallas_call(
        functools.partial(_sel_s_kernel, n_pages=n_pages, tn=tn, kc=kc),
        grid_spec=grid_spec,
        out_shape=jax.ShapeDtypeStruct((B * tn, 512), BF16),
        compiler_params=_cparams(("arbitrary",)),
        name="select_attend_sample",
    )(page_table, nsa_pool, qn, nrow, wrow, win_t, seln, oc, gates, e_mat)


def _pair_cols(base):
    idx = []
    for r in range(N_REP):
        for g in range(N_GROUPS):
            h = g * N_REP + r
            idx.extend(range(base + h * HEAD_DIM, base + (h + 1) * HEAD_DIM))
    return np.asarray(idx, np.int32)


def _prep_weights(w_in, b_fox_f, fox_qn_g, fox_kn_g, nsa_qn_g, nsa_kn_slc_g, nsa_kn_win_g):
    o_fq, o_fk, o_fv, o_ff, o_nq, o_nkv, o_ng, o_mg = 0, 512, 640, 768, 776, 1288, 2056, 2080
    cols = np.concatenate([
        _pair_cols(o_fq), np.arange(o_fk, o_fk + 128), np.arange(o_fv, o_fv + 128),
        _pair_cols(o_nq), np.arange(o_nkv, o_nkv + 768),
        np.arange(o_ng, o_ng + 24), np.arange(o_ff, o_ff + 8)]).astype(np.int32)
    w_p = jnp.take(w_in, cols, axis=1)
    w_p = jnp.pad(w_p, ((0, 0), (0, _C_END - w_p.shape[1]))).astype(BF16)
    w_mg = w_in[:, o_mg:o_mg + 2 * D_MODEL].astype(BF16)
    tile2 = lambda g: jnp.tile(g, 2)
    gains = jnp.stack([tile2(fox_qn_g), tile2(fox_kn_g), tile2(nsa_qn_g), tile2(nsa_kn_slc_g),
                       tile2(nsa_kn_win_g)] + [jnp.zeros((LANES,), F32)] * 3)
    bff = jnp.zeros((1, LANES), F32).at[0, LOGF_LANE0:LOGF_LANE0 + N_HEADS].set(b_fox_f)
    return w_p, w_mg, gains, bff


def _prep_compress(pos, w1, w2):
    w1r = w1.reshape(2, CMP_STRIDE, HEAD_DIM, CMP_HIDDEN)
    z = jnp.zeros((CMP_STRIDE, HEAD_DIM, CMP_HIDDEN), F32)
    top = jnp.concatenate([w1r[0], z, w1r[1], z], axis=-1)
    bot = jnp.concatenate([z, w1r[0], z, w1r[1]], axis=-1)
    w1b = jnp.concatenate([top, bot], axis=1).astype(BF16)
    w1b = w1b.reshape(CMP_STRIDE // 2, 2 * LANES, 4 * LANES)
    pr = pos.reshape(2, CMP_STRIDE, HEAD_DIM)
    posb = jnp.concatenate([jnp.tile(pr, (1, 1, 2)).transpose(1, 0, 2),
                            jnp.zeros((CMP_STRIDE, 6, LANES), F32)], axis=1).astype(BF16)
    posb = posb.reshape(CMP_STRIDE // 2, 2, 8, LANES).transpose(0, 2, 1, 3).reshape(CMP_STRIDE // 2, 8, 2 * LANES)
    zz = jnp.zeros((CMP_HIDDEN, HEAD_DIM), F32)
    w2b = jnp.concatenate([jnp.concatenate([w2, zz], axis=1),
                           jnp.concatenate([zz, w2], axis=1)], axis=0).astype(BF16)
    return w1b, posb, w2b


def kernel(x_prompt, x_sample, cache_fox_kv, cache_fox_logf, cache_nsa_kv, state_win_kv, page_table,
           c_prompt, c_sample, norm1_g, norm2_g, w_ada, b_ada, w_in, b_fox_f, fox_qn_g, fox_kn_g,
           nsa_qn_g, nsa_kn_cmp_g, nsa_kn_slc_g, nsa_kn_win_g, cmp_pos_k, cmp_w1_k, cmp_w2_k,
           cmp_pos_v, cmp_w1_v, cmp_w2_v, w_br_fox, w_br_nsa, w_out, w_up, w_down):
    assert norm1_g.shape[0] == 1
    B, T, d = x_prompt.shape
    DB, TN, _ = x_sample.shape
    n_phys = cache_fox_kv.shape[1]
    n_pages = page_table.shape[1]
    past = n_pages * PAGE_SIZE
    wlen = state_win_kv.shape[2]
    assert T % 256 == 0 and T >= WINDOW and wlen == WINDOW and TN == 8 and past % CMP_STRIDE == 0

    w_p, w_mg, gains, bff = _prep_weights(w_in[0], b_fox_f[0], fox_qn_g[0], fox_kn_g[0], nsa_qn_g[0],
                                          nsa_kn_slc_g[0], nsa_kn_win_g[0])
    w1k, posk, w2k = _prep_compress(cmp_pos_k[0], cmp_w1_k[0], cmp_w2_k[0])
    w1v, posv, w2v = _prep_compress(cmp_pos_v[0], cmp_w1_v[0], cmp_w2_v[0])
    bd = jnp.asarray(np.kron(np.eye(2), np.ones((HEAD_DIM, HEAD_DIM))), BF16)
    cw = dict(w1k=w1k, w1v=w1v, posk=posk, posv=posv, w2k=w2k, w2v=w2v, bd=bd,
              gk=jnp.tile(nsa_kn_cmp_g[0], 2).reshape(1, LANES))
    pair_rows = _pair_cols(0)
    wbf = jnp.take(w_br_fox[0], pair_rows, axis=0).astype(BF16)
    wbn = jnp.take(w_br_nsa[0], pair_rows, axis=0).astype(BF16)
    wout = w_out[0].astype(BF16)
    wup = w_up[0].astype(BF16)
    wdn = w_down[0].astype(BF16)
    g1 = norm1_g[0].reshape(1, d)
    g2 = norm2_g[0].reshape(1, d)
    tm_p = 512
    tri = jnp.asarray(np.tril(np.ones((tm_p, tm_p), np.float32)), BF16)

    mod = _ada_call(jnp.concatenate([c_prompt, c_sample], axis=0), w_ada[0], b_ada[0])
    mod_p = mod[:B].reshape(B, 1, 6 * d)
    mod_s = mod[B:].reshape(DB, 1, 6 * d)

    (qf, frow, qn, nrow, wrow, kvb, gates, lf, c_tm, ct) = _pre_call(
        x_prompt, mod_p, g1, w_p, bd, tri, gains, bff, nb=1, tt=tm_p, do_cum=True)
    tq = 256
    o_fox = _fox_p_call(qf, kvb, c_tm, ct, B, T, tq)
    kc_p, vc_p = _cmp_p_call(nrow, cw, B, T)
    nc_p = T // CMP_STRIDE
    nsel_p = -(-T // SEL_BLOCK)
    a_p = jnp.asarray(_importance_matrix(nc_p, LANES, nc_p - 1)[:, :LANES] *
                      (np.arange(LANES) < nsel_p)[None, :], BF16)
    e_p = jnp.asarray((np.arange(T)[None, :] // SEL_BLOCK) == np.arange(LANES)[:, None], BF16)
    o_nsa = _nsa_p_call(qn, kvb, kc_p, vc_p, gates, a_p, e_p, B, T, tq)
    x1 = _mix_call(x_prompt, o_fox, o_nsa, mod_p, g1, w_mg, wbf, wbn, wout, nb=1, tt=tm_p)
    y_prompt = _ffn_call(x1, mod_p, g2, wup, wdn, nb=1, tt=tm_p)

    nb_s = min(DB, 32)
    (qf_s, frow_s, qn_s, nrow_s, wrow_s, _, gates_s, lf_s) = _pre_call(
        x_sample, mod_s, g1, w_p, bd, tri, gains, bff, nb=nb_s, tt=TN, do_cum=False)

    fox_t = jnp.transpose(cache_fox_kv[0], (0, 2, 3, 4, 1)).reshape(n_phys, 2 * LANES, PAGE_SIZE)
    nsa_t = jnp.transpose(cache_nsa_kv[0], (0, 2, 3, 4, 1)).reshape(n_phys, 4 * LANES, PAGE_SIZE)
    logf_t = jnp.transpose(cache_fox_logf[0], (0, 2, 1))
    win_t = jnp.transpose(state_win_kv[0], (0, 2, 3, 4, 1)).reshape(DB, 2 * LANES, wlen)

    lf_new = lf_s[:, LOGF_LANE0:LOGF_LANE0 + N_HEADS].reshape(DB, TN, N_HEADS).transpose(0, 2, 1)
    lf_new = jnp.pad(lf_new, ((0, 0), (0, 0), (0, LANES - TN)))
    o_fox_s = _fox_s_call(page_table, fox_t, logf_t, qf_s, frow_s, lf_new, TN)

    nsa_pool = nsa_t
    nc_s = past // CMP_STRIDE
    nsel_s = -(-(past + TN) // SEL_BLOCK)
    nsl = -(-nsel_s // LANES) * LANES
    a_s = jnp.asarray(_importance_matrix(nc_s, nsl, nc_s - 1) * (np.arange(nsl) < nsel_s)[None, :], BF16)
    oc_s, imp_s = _cmp_s_call(page_table, nsa_pool, qn_s, cw, a_s, TN)
    seln = _topk_s_call(imp_s.reshape(DB * N_GROUPS * TN, nsl), TN, past)
    kc_keys = min(past, 2048)
    bpc = kc_keys // SEL_BLOCK
    seln = seln.reshape(DB, N_GROUPS * TN, nsl // bpc, bpc).transpose(0, 2, 1, 3)
    e_s = jnp.asarray((np.arange(kc_keys)[None, :] // SEL_BLOCK) == np.arange(bpc)[:, None], BF16)
    o_nsa_s = _sel_s_call(page_table, nsa_pool, qn_s, nrow_s, wrow_s, win_t, seln, oc_s, gates_s, e_s,
                          TN, kc_keys)
    x1_s = _mix_call(x_sample, o_fox_s, o_nsa_s, mod_s, g1, w_mg, wbf, wbn, wout, nb=nb_s, tt=TN)
    y_sample = _ffn_call(x1_s, mod_s, g2, wup, wdn, nb=nb_s, tt=TN)

    lf_p = lf[:, LOGF_LANE0:LOGF_LANE0 + N_HEADS]
    win_p = wrow.reshape(B, T, 256)[:, T - wlen:]
    win_s = jnp.concatenate([state_win_kv[0, :, TN:],
                             wrow_s.reshape(DB, TN, 2, N_GROUPS, HEAD_DIM)], axis=1)
    return (y_prompt, y_sample,
            frow.reshape(1, B, T, 2, N_GROUPS, HEAD_DIM),
            lf_p.reshape(1, B, T, N_HEADS),
            nrow.reshape(1, B, T, 4, N_GROUPS, HEAD_DIM),
            win_p.reshape(1, B, wlen, 2, N_GROUPS, HEAD_DIM),
            frow_s.reshape(1, DB, TN, 2, N_GROUPS, HEAD_DIM),
            lf_s[:, LOGF_LANE0:LOGF_LANE0 + N_HEADS].reshape(1, DB, TN, N_HEADS),
            nrow_s.reshape(1, DB, TN, 4, N_GROUPS, HEAD_DIM),
            win_s[None])
```

```python
import functools

import numpy as np
import jax
import jax.numpy as jnp
from jax import lax
from jax.experimental import pallas as pl
from jax.experimental.pallas import tpu as pltpu

F32 = jnp.float32
BF16 = jnp.bfloat16

D_MODEL = 1024
HEAD_DIM = 64
N_HEADS = 8
N_GROUPS = 2
N_REP = N_HEADS // N_GROUPS
PAGE_SIZE = 128
CMP_STRIDE = 16
CMP_HIDDEN = 2 * HEAD_DIM
SEL_BLOCK = 64
N_SELECT = 16
WINDOW = 512
D_FF = 4 * D_MODEL
RMS_EPS = 1e-6
FORCE_BONUS = 1.0e4
LANES = 128
NEG = -1e30
MASK_BIG = 2.0 ** 100
LOGF_LANE0 = 0
GATE_LANE0 = 8
VMEM_LIMIT = 56 * 1024 * 1024


def _dot(a, b):
    return jnp.dot(a, b, preferred_element_type=F32)


def _dot_nt(a, b):
    return lax.dot_general(a, b, (((1,), (1,)), ((), ())), preferred_element_type=F32)


def _split3(x):
    x1 = x.astype(BF16)
    r = x - x1.astype(F32)
    x2 = r.astype(BF16)
    x3 = (r - x2.astype(F32)).astype(BF16)
    return x1, x2, x3


def _dot_exact_r(x, m):
    a, b, c = _split3(x)
    return _dot(a, m) + _dot(b, m) + _dot(c, m)


def _dot_exact_l(m, x):
    a, b, c = _split3(x)
    return _dot(m, a) + _dot(m, b) + _dot(m, c)


def _sigmoid(x):
    return 1.0 / (1.0 + jnp.exp(-x))


def _head_rms(zc, bd):
    a = zc * zc
    a1 = a.astype(BF16)
    a2 = (a - a1.astype(F32)).astype(BF16)
    ss = _dot(a1, bd) + _dot(a2, bd)
    return zc * lax.rsqrt(ss * (1.0 / HEAD_DIM) + RMS_EPS)


def _cparams(sem, vmem=VMEM_LIMIT):
    return pltpu.CompilerParams(dimension_semantics=sem, vmem_limit_bytes=vmem)


def _ada_kernel(c_ref, w_ref, b_ref, o_ref):
    c = c_ref[...]
    a = c * _sigmoid(c)
    o_ref[...] = _dot(a.astype(BF16), w_ref[...].astype(BF16)) + b_ref[...]


def _ada_call(c, w_ada, b_ada):
    r, d = c.shape
    n = w_ada.shape[1]
    tn = 1024
    return pl.pallas_call(
        _ada_kernel,
        grid=(n // tn,),
        in_specs=[pl.BlockSpec((r, d), lambda j: (0, 0)),
                  pl.BlockSpec((d, tn), lambda j: (0, j)),
                  pl.BlockSpec((1, tn), lambda j: (0, j))],
        out_specs=pl.BlockSpec((r, tn), lambda j: (0, j)),
        out_shape=jax.ShapeDtypeStruct((r, n), F32),
        compiler_params=_cparams(("arbitrary",)),
        name="ada_mod",
    )(c, w_ada, b_ada.reshape(1, n))


_C_FQ, _C_FK, _C_FV, _C_NQ, _C_NKV, _C_SMALL, _C_END = 0, 512, 640, 768, 1280, 2048, 2176


def _v_with_ones(v):
    lane = lax.broadcasted_iota(jnp.int32, v.shape, 1)
    return jnp.concatenate([jnp.where(lane < HEAD_DIM, v, 1.0), jnp.where(lane < HEAD_DIM, 1.0, v)],
                           axis=1).astype(BF16)


def _pos_hi_lo(pos):
    return (pos >> 8).astype(F32), (pos & 255).astype(F32)


_AUG_SEL0 = 4


def _key_pos_features(pos, lane):
    hi, lo = _pos_hi_lo(pos)
    return jnp.where(lane < 2, 1.0,
                     jnp.where(lane == 2, hi,
                               jnp.where(lane == 3, lo,
                                         jnp.where(lane - _AUG_SEL0 == pos // SEL_BLOCK, 1.0, 0.0))))


def _query_pos_features(pos, slope, lane):
    hi, lo = _pos_hi_lo(pos)
    return jnp.where(lane == 0, -(slope * 256.0) * hi,
                     jnp.where(lane == 1, -slope * lo,
                               jnp.where(lane == 2, slope * 256.0,
                                         jnp.where(lane == 3, slope, 0.0))))


def _pre_kernel(x_ref, sc_ref, sh_ref, g1_ref, w_ref, bd_ref, tri_ref, gains_ref, bff_ref,
                qf_ref, frow_ref, qn_ref, nrow_ref, wrow_ref, gate_ref, lf_ref,
                *rest, do_cum):
    nb, tt, d = x_ref.shape
    tm = nb * tt
    x = x_ref[...]
    ms = jnp.mean(x * x, axis=-1, keepdims=True)
    h = x * lax.rsqrt(ms + RMS_EPS) * g1_ref[...] * (1.0 + sc_ref[...]) + sh_ref[...]
    h = h.reshape(tm, d).astype(BF16)
    bd = bd_ref[...]

    def proj(c0):
        return _dot(h, w_ref[:, c0:c0 + LANES])

    scale = HEAD_DIM ** -0.5
    for r in range(N_REP):
        qf_ref[:, r * LANES:(r + 1) * LANES] = (
            _head_rms(proj(_C_FQ + r * LANES), bd) * (gains_ref[0:1, :] * scale)).astype(BF16)
        qn_ref[:, r * LANES:(r + 1) * LANES] = (
            _head_rms(proj(_C_NQ + r * LANES), bd) * (gains_ref[2:3, :] * scale)).astype(BF16)

    fk = _head_rms(proj(_C_FK), bd) * gains_ref[1:2, :]
    fv = proj(_C_FV)
    frow_ref[:, 0:LANES] = fk
    frow_ref[:, LANES:2 * LANES] = fv

    ck = proj(_C_NKV)
    cv = proj(_C_NKV + LANES)
    sk = _head_rms(proj(_C_NKV + 2 * LANES), bd) * gains_ref[3:4, :]
    sv = proj(_C_NKV + 3 * LANES)
    wk = _head_rms(proj(_C_NKV + 4 * LANES), bd) * gains_ref[4:5, :]
    wv = proj(_C_NKV + 5 * LANES)
    nrow_ref[:, 0:LANES] = ck
    nrow_ref[:, LANES:2 * LANES] = cv
    nrow_ref[:, 2 * LANES:3 * LANES] = sk
    nrow_ref[:, 3 * LANES:4 * LANES] = sv
    wrow_ref[:, 0:LANES] = wk
    wrow_ref[:, LANES:2 * LANES] = wv

    zl = proj(_C_SMALL)
    gate_ref[...] = _sigmoid(zl)
    xl = zl + bff_ref[...]
    lf = jnp.minimum(xl, 0.0) - jnp.log1p(jnp.exp(-jnp.abs(xl)))
    lf_ref[...] = lf

    if do_cum:
        c_ref, kf_ref, vf_ref, ks_ref, vs_ref, kw_ref, vw_ref, carry_ref = rest

        @pl.when(pl.program_id(1) == 0)
        def _():
            carry_ref[...] = jnp.zeros_like(carry_ref)

        c = _dot_exact_l(tri_ref[...], lf) + carry_ref[0:1, :]
        c_ref[...] = c
        carry_ref[...] = jnp.broadcast_to(c[tm - 1:tm, :], carry_ref.shape)

        lane = lax.broadcasted_iota(jnp.int32, (tm, LANES), 1)
        c1, c2, c3 = (p.astype(F32) for p in _split3(c))
        aug_f = jnp.where(lane < N_HEADS, -c1,
                          jnp.where(lane < 2 * N_HEADS, -pltpu.roll(c2, N_HEADS, 1),
                                    jnp.where(lane < 3 * N_HEADS, -pltpu.roll(c3, 2 * N_HEADS, 1),
                                              jnp.where(lane < 3 * N_HEADS + 3, 1.0, 0.0))))
        kf_ref[:, 0:LANES] = fk.astype(BF16)
        kf_ref[:, LANES:2 * LANES] = aug_f.astype(BF16)
        vf_ref[...] = _v_with_ones(fv)

        pos = pl.program_id(1) * tm + lax.broadcasted_iota(jnp.int32, (tm, 1), 0)
        aug_p = _key_pos_features(pos, lane).astype(BF16)
        ks_ref[:, 0:LANES] = sk.astype(BF16)
        ks_ref[:, LANES:2 * LANES] = aug_p
        vs_ref[...] = _v_with_ones(sv)
        kw_ref[:, 0:LANES] = wk.astype(BF16)
        kw_ref[:, LANES:2 * LANES] = aug_p
        vw_ref[...] = _v_with_ones(wv)


def _pre_call(x3, mod3, g1, w_p, bd, tri, gains, bff, nb, tt, do_cum):
    NB, TT, d = x3.shape
    tm = nb * tt
    n = NB * TT
    gi, gj = NB // nb, TT // tt
    tok = lambda i, j: (i * gj + j, 0)
    full = lambda i, j: (0, 0)
    in_specs = [
        pl.BlockSpec((nb, tt, d), lambda i, j: (i, j, 0)),
        pl.BlockSpec((nb, 1, d), lambda i, j: (i, 0, 1)),
        pl.BlockSpec((nb, 1, d), lambda i, j: (i, 0, 0)),
        pl.BlockSpec((1, d), full),
        pl.BlockSpec(w_p.shape, full),
        pl.BlockSpec(bd.shape, full),
        pl.BlockSpec(tri.shape, full),
        pl.BlockSpec(gains.shape, full),
        pl.BlockSpec(bff.shape, full),
    ]
    out_shape = [
        jax.ShapeDtypeStruct((n, 512), BF16),
        jax.ShapeDtypeStruct((n, 256), F32),
        jax.ShapeDtypeStruct((n, 512), BF16),
        jax.ShapeDtypeStruct((n, 512), F32),
        jax.ShapeDtypeStruct((n, 256), F32),
        jax.ShapeDtypeStruct((n, LANES), F32),
        jax.ShapeDtypeStruct((n, LANES), F32),
    ]
    scratch = []
    if do_cum:
        assert nb == 1
        out_shape += [jax.ShapeDtypeStruct((n, LANES), F32)]
        out_shape += [jax.ShapeDtypeStruct((n, 256), BF16)] * 6
        scratch = [pltpu.VMEM((8, LANES), F32)]
    out_specs = [pl.BlockSpec((tm, s.shape[1]), tok) for s in out_shape]
    return pl.pallas_call(
        functools.partial(_pre_kernel, do_cum=do_cum),
        grid=(gi, gj),
        in_specs=in_specs,
        out_specs=out_specs,
        out_shape=out_shape,
        scratch_shapes=scratch,
        compiler_params=_cparams(("arbitrary", "arbitrary")),
        name="pre_mixer",
    )(x3, mod3, mod3, g1, w_p, bd, tri, gains, bff)


def _flash_step(qp, k, v, bias, m, l, acc):
    s = _dot_nt(qp, k) + bias
    m_new = jnp.maximum(m, jnp.max(s, axis=-1, keepdims=True))
    alpha = jnp.exp(m - m_new)
    p = jnp.exp(s - m_new)
    l = alpha * l + jnp.sum(p, axis=-1, keepdims=True)
    acc = alpha * acc + _dot(p.astype(BF16), v)
    return m_new, l, acc


def _flash_init(rows):
    return (jnp.full((rows, 1), NEG, F32), jnp.zeros((rows, 1), F32), jnp.zeros((rows, LANES), F32))


def _half_mask(rows, g):
    lane = lax.broadcasted_iota(jnp.int32, (rows, LANES), 1)
    return (lane < HEAD_DIM) if g == 0 else (lane >= HEAD_DIM)


def _slope(h):
    return 2.0 ** (-8.0 * (h + 1) / N_HEADS)


def _select_blocks(imp, cur):
    blk = lax.broadcasted_iota(jnp.int32, imp.shape, 1)
    forced = (blk == 0) | (blk == cur) | (blk == cur - 1)
    score = jnp.where(blk <= cur, imp + jnp.where(forced, FORCE_BONUS, 0.0), -jnp.inf)
    blkf = blk.astype(F32)

    def body(_, carry):
        sc, sel = carry
        mx = jnp.max(sc, axis=-1, keepdims=True)
        idx = jnp.min(jnp.where(sc == mx, blkf, 1e9), axis=-1, keepdims=True)
        pick = blkf == idx
        return jnp.where(pick, -jnp.inf, sc), jnp.where(pick, 1.0, sel)

    _, sel = lax.fori_loop(0, N_SELECT, body, (score, jnp.zeros(imp.shape, F32)))
    return jnp.where(blk <= cur, sel, 0.0)


def _select_blocks_by_rank(imp, cur, n_blocks):
    blk = lax.broadcasted_iota(jnp.int32, imp.shape, 1)
    forced = (blk == 0) | (blk == cur) | (blk == cur - 1)
    score = jnp.where(blk <= cur, imp + jnp.where(forced, FORCE_BONUS, 0.0), -jnp.inf)
    beaten_by = jnp.zeros(imp.shape, F32)
    for i in range(n_blocks):
        si = score[:, i:i + 1]
        beaten_by = beaten_by + jnp.where((si > score) | ((si == score) & (blk > i)), 1.0, 0.0)
    return jnp.where((beaten_by < N_SELECT) & (blk <= cur), 1.0, 0.0)


def _importance_matrix(n_cmp_rows, n_sel_cols, n_cmp):
    r = SEL_BLOCK // CMP_STRIDE
    a = np.zeros((n_cmp_rows, n_sel_cols), np.float32)
    for n in range(n_cmp):
        for j in range(n_sel_cols):
            off = n - r * j
            if off in (-1, r - 1):
                a[n, j] = 0.5
            elif 0 <= off <= r - 2:
                a[n, j] = 1.0
    return a


def _stack_q_features(q_ref, qa_ref, tq, feat_fn):
    for g in range(N_GROUPS):
        for r in range(N_REP):
            h = g * N_REP + r
            qc = q_ref[:, r * LANES:(r + 1) * LANES]
            qa_ref[h * tq:(h + 1) * tq, 0:LANES] = jnp.where(_half_mask(tq, g), qc, jnp.zeros_like(qc))
            qa_ref[h * tq:(h + 1) * tq, LANES:2 * LANES] = feat_fn(h).astype(BF16)


def _stacked_step(qa, ka, va, fix, m, acc):
    half = qa.shape[0] // 2
    s = _dot_nt(qa, ka)
    if fix is not None:
        s = fix(s)
    m_new = jnp.maximum(m, jnp.max(s, axis=-1, keepdims=True))
    alpha = jnp.exp(m - m_new)
    p = jnp.exp(s - m_new).astype(BF16)
    pv = jnp.concatenate([_dot(p[:half], va[:, 0:LANES]), _dot(p[half:], va[:, LANES:2 * LANES])], axis=0)
    return m_new, alpha * acc + pv


def _stacked_init(rows):
    return jnp.full((rows, 1), NEG, F32), jnp.zeros((rows, LANES), F32)


def _stacked_finish(acc):
    return acc / pltpu.roll(acc, HEAD_DIM, 1)


def _row_pos(t0, tq):
    return t0 + lax.rem(lax.broadcasted_iota(jnp.int32, (N_HEADS * tq, 1), 0), tq)


def _fox_p_kernel(q_ref, ka_ref, va_ref, c_ref, o_ref, qa_ref, *, tq, tk):
    qi = pl.program_id(1)
    t0 = qi * tq
    R = N_HEADS * tq
    lane = lax.broadcasted_iota(jnp.int32, (tq, LANES), 1)
    c1, c2, c3 = (p.astype(F32) for p in _split3(c_ref[...]))

    def q_features(h):
        own = (lane == h) | (lane == N_HEADS + h) | (lane == 2 * N_HEADS + h)
        return jnp.where(lane == 3 * N_HEADS, c1[:, h:h + 1],
                         jnp.where(lane == 3 * N_HEADS + 1, c2[:, h:h + 1],
                                   jnp.where(lane == 3 * N_HEADS + 2, c3[:, h:h + 1],
                                             jnp.where(own, 1.0, 0.0))))

    _stack_q_features(q_ref, qa_ref, tq, q_features)
    qa = qa_ref[...]
    tpos = _row_pos(t0, tq)

    def tile(j):
        st = pl.multiple_of(j * tk, tk)
        return ka_ref[pl.ds(st, tk), :], va_ref[pl.ds(st, tk), :]

    jd = t0 // tk
    kpos = jd * tk + lax.broadcasted_iota(jnp.int32, (1, tk), 1)
    ka, va = tile(jd)
    carry = _stacked_step(qa, ka, va, lambda s: jnp.where(kpos <= tpos, s, NEG), *_stacked_init(R))

    def body(j, carry):
        ka, va = tile(j)
        return _stacked_step(qa, ka, va, None, *carry)

    m, acc = lax.fori_loop(0, jd, body, carry)
    o_ref[...] = _unstack_o(_stacked_finish(acc), tq).astype(BF16)


def _fox_p_call(qf, ka, va, c_tm, B, T, tq, tk):
    n = B * T
    nq = T // tq
    return pl.pallas_call(
        functools.partial(_fox_p_kernel, tq=tq, tk=tk),
        grid=(B, nq),
        in_specs=[pl.BlockSpec((tq, 512), lambda b, i: (b * nq + i, 0)),
                  pl.BlockSpec((T, 256), lambda b, i: (b, 0)),
                  pl.BlockSpec((T, 256), lambda b, i: (b, 0)),
                  pl.BlockSpec((tq, LANES), lambda b, i: (b * nq + i, 0))],
        out_specs=pl.BlockSpec((tq, 512), lambda b, i: (b * nq + i, 0)),
        out_shape=jax.ShapeDtypeStruct((n, 512), BF16),
        scratch_shapes=[pltpu.VMEM((N_HEADS * tq, 2 * LANES), BF16)],
        compiler_params=_cparams(("arbitrary", "arbitrary")),
        name="fox_prompt",
    )(qf, ka, va, c_tm)


def _compress(xk_ref, xv_ref, n, w1k_ref, w1v_ref, posk_ref, posv_ref, w2k_ref, w2v_ref, bd, gk):
    def chunk_rows(x_ref):
        return jnp.concatenate([x_ref[pl.ds(c, n, stride=CMP_STRIDE), :].astype(BF16)
                                for c in range(CMP_STRIDE)], axis=1)

    acc_k = _dot(chunk_rows(xk_ref), w1k_ref[...])
    acc_v = _dot(chunk_rows(xv_ref), w1v_ref[...])
    ck = _dot(posk_ref[...], w1k_ref[...])
    cv = _dot(posv_ref[...], w1v_ref[...])

    def finish(acc, cst, w2_ref):
        lead = acc[:, 0:2 * LANES] + cst[0:1, 0:2 * LANES]
        trail = acc[:, 2 * LANES:4 * LANES] + cst[1:2, 2 * LANES:4 * LANES]
        hid = lead + pltpu.roll(trail, n - 1, 0)
        act = hid * _sigmoid(hid)
        return _dot(act.astype(BF16), w2_ref[...])

    kc = _head_rms(finish(acc_k, ck, w2k_ref), bd) * gk
    vc = finish(acc_v, cv, w2v_ref)
    return kc, vc


def _cmp_p_kernel(xk_ref, xv_ref, w1k_ref, w1v_ref, posk_ref, posv_ref, w2k_ref, w2v_ref, bd_ref, gk_ref,
                  kc_ref, vc_ref):
    n = kc_ref.shape[0]
    kc, vc = _compress(xk_ref, xv_ref, n, w1k_ref, w1v_ref, posk_ref, posv_ref, w2k_ref, w2v_ref,
                       bd_ref[...], gk_ref[...])
    kc_ref[:, 0:LANES] = kc.astype(BF16)
    c_end = lax.broadcasted_iota(jnp.int32, (n, 1), 0) * CMP_STRIDE + (2 * CMP_STRIDE - 1)
    lane = lax.broadcasted_iota(jnp.int32, (n, LANES), 1)
    kc_ref[:, LANES:2 * LANES] = _key_pos_features(c_end, lane).astype(BF16)
    vc_ref[...] = vc.astype(BF16)


def _cmp_p_call(nrow, cw, B, T):
    nc = T // CMP_STRIDE
    consts = [cw["w1k"], cw["w1v"], cw["posk"], cw["posv"], cw["w2k"], cw["w2v"], cw["bd"], cw["gk"]]
    cspecs = [pl.BlockSpec(a.shape, (lambda b, nd=a.ndim: (0,) * nd)) for a in consts]
    return pl.pallas_call(
        _cmp_p_kernel,
        grid=(B,),
        in_specs=[pl.BlockSpec((T, LANES), lambda b: (b, 0)),
                  pl.BlockSpec((T, LANES), lambda b: (b, 1))] + cspecs,
        out_specs=[pl.BlockSpec((nc, 2 * LANES), lambda b: (b, 0)),
                   pl.BlockSpec((nc, LANES), lambda b: (b, 0))],
        out_shape=[jax.ShapeDtypeStruct((B * nc, 2 * LANES), BF16),
                   jax.ShapeDtypeStruct((B * nc, LANES), BF16)],
        compiler_params=_cparams(("arbitrary",)),
        name="compress_prompt",
    )(nrow, nrow, *consts)


def _gate_cols(gate_ref, tn):
    R = N_HEADS * tn
    gl = lax.broadcasted_iota(jnp.int32, (R, LANES), 1)
    hrow = lax.broadcasted_iota(jnp.int32, (R, LANES), 0) // tn
    gt = jnp.broadcast_to(gate_ref[...][None], (N_HEADS, tn, LANES)).reshape(R, LANES)
    return [jnp.sum(jnp.where(gl == GATE_LANE0 + 3 * hrow + j, gt, 0.0), axis=-1, keepdims=True)
            for j in range(3)]


def _nsa_p_kernel(q_ref, ks_ref, vs_ref, kw_ref, vw_ref, kc_ref, vc_ref, gate_ref, a_ref, o_ref, qa_ref,
                  *, tq, tk, tw, n_sel):
    qi = pl.program_id(1)
    t0 = qi * tq
    R = N_HEADS * tq
    nc = kc_ref.shape[0]
    lane = lax.broadcasted_iota(jnp.int32, (tq, LANES), 1)
    pos_q = t0 + lax.broadcasted_iota(jnp.int32, (tq, 1), 0)
    _stack_q_features(q_ref, qa_ref, tq, lambda h: _query_pos_features(pos_q, _slope(h), lane))
    qa = qa_ref[...]
    tpos = _row_pos(t0, tq)

    nid = lax.broadcasted_iota(jnp.int32, (1, nc), 1)
    cvalid = (nid * CMP_STRIDE + (2 * CMP_STRIDE - 1) <= tpos) & (nid < nc - 1)
    s = jnp.where(cvalid, _dot_nt(qa, kc_ref[...]), NEG)
    mx = jnp.max(s, axis=-1, keepdims=True)
    p = jnp.where(cvalid, jnp.exp(s - mx), 0.0)
    den = jnp.sum(p, axis=-1, keepdims=True)
    p = p / jnp.where(den > 0, den, 1.0)
    o_c = _dot(p.astype(BF16), vc_ref[...])
    p4 = p.reshape(N_GROUPS, N_REP, tq, nc)
    psum = (p4[:, 0] + p4[:, 1] + p4[:, 2] + p4[:, 3]).reshape(N_GROUPS * tq, nc)

    def wtile(j):
        st = pl.multiple_of(j * tw, tw)
        return kw_ref[pl.ds(st, tw), :], vw_ref[pl.ds(st, tw), :]

    def in_window(j):
        kpos = j * tw + lax.broadcasted_iota(jnp.int32, (1, tw), 1)
        return lambda s: jnp.where((kpos <= tpos) & (tpos - kpos < WINDOW), s, NEG)

    jw = t0 // tw
    ka, va = wtile(jw)
    carry = _stacked_step(qa, ka, va, in_window(jw), *_stacked_init(R))

    def wbody(j, carry):
        ka, va = wtile(j)
        return _stacked_step(qa, ka, va, in_window(j), *carry)

    m, acc = lax.fori_loop(jnp.maximum(t0 - (WINDOW - 1), 0) // tw, jw, wbody, carry)
    o_w = _stacked_finish(acc)

    imp = _dot_exact_r(psum, a_ref[...])
    cur = (t0 + lax.rem(lax.broadcasted_iota(jnp.int32, (N_GROUPS * tq, 1), 0), tq)) // SEL_BLOCK
    seln = pltpu.roll((_select_blocks_by_rank(imp, cur, n_sel) - 1.0) * MASK_BIG, _AUG_SEL0, 1)
    for g in range(N_GROUPS):
        for r in range(N_REP):
            h = g * N_REP + r
            feat = jnp.where(lane >= _AUG_SEL0, seln[g * tq:(g + 1) * tq],
                             _query_pos_features(pos_q, _slope(h), lane))
            qa_ref[h * tq:(h + 1) * tq, LANES:2 * LANES] = feat.astype(BF16)
    qa = qa_ref[...]

    def stile(j):
        st = pl.multiple_of(j * tk, tk)
        return ks_ref[pl.ds(st, tk), :], vs_ref[pl.ds(st, tk), :]

    js = t0 // tk
    kpos = js * tk + lax.broadcasted_iota(jnp.int32, (1, tk), 1)
    ka, va = stile(js)
    carry = _stacked_step(qa, ka, va, lambda s: jnp.where(kpos <= tpos, s, NEG), *_stacked_init(R))

    def sbody(j, carry):
        ka, va = stile(j)
        return _stacked_step(qa, ka, va, None, *carry)

    m, acc = lax.fori_loop(0, js, sbody, carry)
    o_s = _stacked_finish(acc)

    gc, gs, gw = _gate_cols(gate_ref, tq)
    o_ref[...] = _unstack_o(gc * o_c + gs * o_s + gw * o_w, tq).astype(BF16)


def _nsa_p_call(qn, ks, vs, kw, vw, kc, vc, gates, a_mat, B, T, tq, tk, tw):
    n = B * T
    nq = T // tq
    nc = T // CMP_STRIDE
    seq = lambda b, i: (b, 0)
    blk = lambda b, i: (b * nq + i, 0)
    return pl.pallas_call(
        functools.partial(_nsa_p_kernel, tq=tq, tk=tk, tw=tw, n_sel=-(-T // SEL_BLOCK)),
        grid=(B, nq),
        in_specs=[pl.BlockSpec((tq, 512), blk),
                  pl.BlockSpec((T, 256), seq), pl.BlockSpec((T, 256), seq),
                  pl.BlockSpec((T, 256), seq), pl.BlockSpec((T, 256), seq),
                  pl.BlockSpec((nc, 2 * LANES), seq),
                  pl.BlockSpec((nc, LANES), seq),
                  pl.BlockSpec((tq, LANES), blk),
                  pl.BlockSpec(a_mat.shape, lambda b, i: (0, 0))],
        out_specs=pl.BlockSpec((tq, 512), blk),
        out_shape=jax.ShapeDtypeStruct((n, 512), BF16),
        scratch_shapes=[pltpu.VMEM((N_HEADS * tq, 2 * LANES), BF16)],
        compiler_params=_cparams(("arbitrary", "arbitrary")),
        name="nsa_prompt",
    )(qn, ks, vs, kw, vw, kc, vc, gates, a_mat)


def _mix_kernel(x_ref, of_ref, on_ref, sc_ref, sh_ref, gt_ref, g1_ref, wmg_ref, wbf_ref, wbn_ref,
                wout_ref, o_ref):
    nb, tt, d = x_ref.shape
    tm = nb * tt
    x = x_ref[...]
    ms = jnp.mean(x * x, axis=-1, keepdims=True)
    h = x * lax.rsqrt(ms + RMS_EPS) * g1_ref[...] * (1.0 + sc_ref[...]) + sh_ref[...]
    h = h.reshape(tm, d).astype(BF16)
    g_fox = _sigmoid(_dot(h, wmg_ref[:, 0:d]))
    g_nsa = _sigmoid(_dot(h, wmg_ref[:, d:2 * d]))
    mix = g_fox * _dot(of_ref[...], wbf_ref[...]) + g_nsa * _dot(on_ref[...], wbn_ref[...])
    y = _dot(mix.astype(BF16), wout_ref[...]).reshape(nb, tt, d)
    o_ref[...] = x + gt_ref[...] * y


def _mix_call(x3, o_fox, o_nsa, mod3, g1, wmg, wbf, wbn, wout, nb, tt):
    NB, TT, d = x3.shape
    tm = nb * tt
    gi, gj = NB // nb, TT // tt
    full = lambda i, j: (0, 0)
    tok = lambda i, j: (i * gj + j, 0)
    return pl.pallas_call(
        _mix_kernel,
        grid=(gi, gj),
        in_specs=[pl.BlockSpec((nb, tt, d), lambda i, j: (i, j, 0)),
                  pl.BlockSpec((tm, 512), tok),
                  pl.BlockSpec((tm, 512), tok),
                  pl.BlockSpec((nb, 1, d), lambda i, j: (i, 0, 1)),
                  pl.BlockSpec((nb, 1, d), lambda i, j: (i, 0, 0)),
                  pl.BlockSpec((nb, 1, d), lambda i, j: (i, 0, 2)),
                  pl.BlockSpec((1, d), full),
                  pl.BlockSpec(wmg.shape, full),
                  pl.BlockSpec(wbf.shape, full),
                  pl.BlockSpec(wbn.shape, full),
                  pl.BlockSpec(wout.shape, full)],
        out_specs=pl.BlockSpec((nb, tt, d), lambda i, j: (i, j, 0)),
        out_shape=jax.ShapeDtypeStruct((NB, TT, d), F32),
        compiler_params=_cparams(("arbitrary", "arbitrary")),
        name="post_mix",
    )(x3, o_fox, o_nsa, mod3, mod3, mod3, g1, wmg, wbf, wbn, wout)


def _ffn_kernel(x_ref, sc_ref, sh_ref, gt_ref, g2_ref, wup_ref, wdn_ref, o_ref, *, fc):
    nb, tt, d = x_ref.shape
    tm = nb * tt
    x = x_ref[...]
    ms = jnp.mean(x * x, axis=-1, keepdims=True)
    h = x * lax.rsqrt(ms + RMS_EPS) * g2_ref[...] * (1.0 + sc_ref[...]) + sh_ref[...]
    h = h.reshape(tm, d).astype(BF16)
    acc = jnp.zeros((tm, d), F32)
    for c in range(D_FF // fc):
        u = jnp.maximum(_dot(h, wup_ref[:, c * fc:(c + 1) * fc]), 0.0)
        acc = acc + _dot((u * u).astype(BF16), wdn_ref[c * fc:(c + 1) * fc, :])
    o_ref[...] = x + gt_ref[...] * acc.reshape(nb, tt, d)


def _ffn_call(x3, mod3, g2, wup, wdn, nb, tt):
    NB, TT, d = x3.shape
    gi, gj = NB // nb, TT // tt
    full = lambda i, j: (0, 0)
    return pl.pallas_call(
        functools.partial(_ffn_kernel, fc=1024),
        grid=(gi, gj),
        in_specs=[pl.BlockSpec((nb, tt, d), lambda i, j: (i, j, 0)),
                  pl.BlockSpec((nb, 1, d), lambda i, j: (i, 0, 4)),
                  pl.BlockSpec((nb, 1, d), lambda i, j: (i, 0, 3)),
                  pl.BlockSpec((nb, 1, d), lambda i, j: (i, 0, 5)),
                  pl.BlockSpec((1, d), full),
                  pl.BlockSpec(wup.shape, full, pipeline_mode=pl.Buffered(1)),
                  pl.BlockSpec(wdn.shape, full, pipeline_mode=pl.Buffered(1))],
        out_specs=pl.BlockSpec((nb, tt, d), lambda i, j: (i, j, 0)),
        out_shape=jax.ShapeDtypeStruct((NB, TT, d), F32),
        compiler_params=_cparams(("arbitrary", "arbitrary")),
        name="ffn",
    )(x3, mod3, mod3, mod3, g2, wup, wdn)


def _page_copy(pool_ref, page, row0, buf_ref, slot, p, sem_ref):
    return pltpu.make_async_copy(
        pool_ref.at[page, pl.ds(row0, buf_ref.shape[1]), :],
        buf_ref.at[slot, :, pl.ds(pl.multiple_of(p * PAGE_SIZE, PAGE_SIZE), PAGE_SIZE)],
        sem_ref.at[slot])


def _gather_start(pt_ref, b, parts, slot, n_pages):
    def body(p, _):
        page = pt_ref[b, p]
        for pool_ref, row0, buf_ref, sem_ref in parts:
            _page_copy(pool_ref, page, row0, buf_ref, slot, p, sem_ref).start()
        return 0
    lax.fori_loop(0, n_pages, body, 0)


def _gather_wait(parts, slot, n_pages):
    def body(p, _):
        for pool_ref, row0, buf_ref, sem_ref in parts:
            _page_copy(pool_ref, 0, row0, buf_ref, slot, p, sem_ref).wait()
        return 0
    lax.fori_loop(0, n_pages, body, 0)


def _gather_pipeline(pt_ref, parts, n_pages):
    b = pl.program_id(0)
    nb = pl.num_programs(0)
    slot = lax.rem(b, 2)

    @pl.when(b == 0)
    def _():
        _gather_start(pt_ref, 0, parts, 0, n_pages)

    @pl.when(b + 1 < nb)
    def _():
        _gather_start(pt_ref, b + 1, parts, 1 - slot, n_pages)

    _gather_wait(parts, slot, n_pages)
    return slot


def _lane_cumsum(x, n):
    lane = lax.broadcasted_iota(jnp.int32, x.shape, 1)
    s = 1
    while s < n:
        x = x + jnp.where(lane >= s, pltpu.roll(x, s, 1), 0.0)
        s *= 2
    return x


def _stack_q(q_ref, tn):
    parts = []
    for g in range(N_GROUPS):
        for r in range(N_REP):
            qc = q_ref[:, r * LANES:(r + 1) * LANES]
            parts.append(jnp.where(_half_mask(tn, g), qc, jnp.zeros_like(qc)))
    return jnp.concatenate(parts, axis=0)


def _unstack_o(o, tn):
    chunks = []
    for r in range(N_REP):
        a = o[r * tn:(r + 1) * tn]
        b = o[(N_REP + r) * tn:(N_REP + r + 1) * tn]
        chunks.append(jnp.where(_half_mask(tn, 0), a, b))
    return jnp.concatenate(chunks, axis=1)


def _per_head_rows(x8, tn):
    n = x8.shape[1]
    return jnp.broadcast_to(x8[:, None, :], (N_HEADS, tn, n)).reshape(N_HEADS * tn, n)


def _slope_col(tn):
    hrow = lax.broadcasted_iota(jnp.int32, (N_HEADS * tn, 1), 0) // tn
    col = jnp.zeros((N_HEADS * tn, 1), F32)
    for h in range(N_HEADS):
        col = jnp.where(hrow == h, _slope(h), col)
    return col


def _tok_col(tn):
    return lax.rem(lax.broadcasted_iota(jnp.int32, (N_HEADS * tn, 1), 0), tn)


def _pad_keys(x, dtype):
    tn = x.shape[0]
    return jnp.concatenate([x, jnp.zeros((LANES - tn, LANES), x.dtype)], axis=0).astype(dtype)


def _flash_step_t(qp, kt, vt, bias, m, l, acc):
    s = _dot(qp, kt) + bias
    m_new = jnp.maximum(m, jnp.max(s, axis=-1, keepdims=True))
    alpha = jnp.exp(m - m_new)
    p = jnp.exp(s - m_new)
    l = alpha * l + jnp.sum(p, axis=-1, keepdims=True)
    acc = alpha * acc + _dot_nt(p.astype(BF16), vt)
    return m_new, l, acc


def _fox_s_kernel(pt_ref, pool_ref, lpool_ref, q_ref, new_ref, lfn_ref, o_ref,
                  kv_ref, lf_ref, sem_ref, lsem_ref, *, n_pages, tn, kc):
    parts = [(pool_ref, 0, kv_ref, sem_ref), (lpool_ref, 0, lf_ref, lsem_ref)]
    slot = _gather_pipeline(pt_ref, parts, n_pages)
    past = n_pages * PAGE_SIZE
    R = N_HEADS * tn
    qs = _stack_q(q_ref, tn)
    cpast = _lane_cumsum(lf_ref[slot], past)
    cn = _lane_cumsum(lfn_ref[0], tn) + cpast[:, past - 1:past]
    tcol = _tok_col(tn)
    tl = lax.broadcasted_iota(jnp.int32, (R, LANES), 1)
    cn_rows = _per_head_rows(cn, tn)
    cq = jnp.sum(jnp.where(tl == tcol, cn_rows, 0.0), axis=-1, keepdims=True)
    knew = _pad_keys(new_ref[:, 0:LANES], BF16)
    vnew = _pad_keys(new_ref[:, LANES:2 * LANES], BF16)
    bias = jnp.where(tl <= tcol, cq - cn_rows, NEG)
    carry = _flash_step(qs, knew, vnew, bias, *_flash_init(R))
    for c in range(past // kc):
        kt = kv_ref[slot, 0:LANES, c * kc:(c + 1) * kc].astype(BF16)
        vt = kv_ref[slot, LANES:2 * LANES, c * kc:(c + 1) * kc].astype(BF16)
        ck = _per_head_rows(cpast[:, c * kc:(c + 1) * kc], tn)
        carry = _flash_step_t(qs, kt, vt, cq - ck, *carry)
    m, l, acc = carry
    o_ref[...] = _unstack_o(acc / l, tn).astype(BF16)


def _fox_s_call(page_table, fox_t, logf_t, qf, frow, lf_new, tn):
    B, n_pages = page_table.shape
    past = n_pages * PAGE_SIZE
    kc = min(past, 2048)
    grid_spec = pltpu.PrefetchScalarGridSpec(
        num_scalar_prefetch=1,
        grid=(B,),
        in_specs=[pl.BlockSpec(memory_space=pl.ANY),
                  pl.BlockSpec(memory_space=pl.ANY),
                  pl.BlockSpec((tn, 512), lambda b, pt: (b, 0)),
                  pl.BlockSpec((tn, 256), lambda b, pt: (b, 0)),
                  pl.BlockSpec((1, N_HEADS, LANES), lambda b, pt: (b, 0, 0))],
        out_specs=pl.BlockSpec((tn, 512), lambda b, pt: (b, 0)),
        scratch_shapes=[pltpu.VMEM((2, 2 * LANES, past), F32), pltpu.VMEM((2, N_HEADS, past), F32),
                        pltpu.SemaphoreType.DMA((2,)), pltpu.SemaphoreType.DMA((2,))],
    )
    return pl.pallas_call(
        functools.partial(_fox_s_kernel, n_pages=n_pages, tn=tn, kc=kc),
        grid_spec=grid_spec,
        out_shape=jax.ShapeDtypeStruct((B * tn, 512), BF16),
        compiler_params=_cparams(("arbitrary",)),
        name="fox_sample",
    )(page_table, fox_t, logf_t, qf, frow, lf_new)


def _cmp_s_kernel(pt_ref, pool_ref, q_ref, w1k_ref, w1v_ref, posk_ref, posv_ref, w2k_ref, w2v_ref,
                  bd_ref, gk_ref, a_ref, oc_ref, imp_ref, kv_ref, xk_ref, xv_ref, sem_ref,
                  *, n_pages, tn):
    slot = _gather_pipeline(pt_ref, [(pool_ref, 0, kv_ref, sem_ref)], n_pages)
    past = n_pages * PAGE_SIZE
    nc = past // CMP_STRIDE
    R = N_HEADS * tn

    unroll = 4 if n_pages % 4 == 0 else 1

    def to_token_major(i, _):
        for u in range(unroll):
            st = pl.multiple_of((i * unroll + u) * PAGE_SIZE, PAGE_SIZE)
            xk_ref[pl.ds(st, PAGE_SIZE), :] = kv_ref[slot, 0:LANES, pl.ds(st, PAGE_SIZE)].T
            xv_ref[pl.ds(st, PAGE_SIZE), :] = kv_ref[slot, LANES:2 * LANES, pl.ds(st, PAGE_SIZE)].T
        return 0

    lax.fori_loop(0, n_pages // unroll, to_token_major, 0)
    kc, vc = _compress(xk_ref, xv_ref, nc, w1k_ref, w1v_ref, posk_ref, posv_ref,
                       w2k_ref, w2v_ref, bd_ref[...], gk_ref[...])
    qs = _stack_q(q_ref, tn)
    nid = lax.broadcasted_iota(jnp.int32, (R, nc), 1)
    tpos = past + _tok_col(tn)
    dc = tpos - (nid * CMP_STRIDE + (2 * CMP_STRIDE - 1))
    valid = (dc >= 0) & (nid < nc - 1)
    s = jnp.where(valid, _dot_nt(qs, kc.astype(BF16)) - _slope_col(tn) * dc.astype(F32), NEG)
    mx = jnp.max(s, axis=-1, keepdims=True)
    p = jnp.where(valid, jnp.exp(s - mx), 0.0)
    den = jnp.sum(p, axis=-1, keepdims=True)
    p = p / jnp.where(den > 0, den, 1.0)
    oc_ref[0] = _dot(p.astype(BF16), vc.astype(BF16))
    p4 = p.reshape(N_GROUPS, N_REP, tn, nc)
    psum = (p4[:, 0] + p4[:, 1] + p4[:, 2] + p4[:, 3]).reshape(N_GROUPS * tn, nc)
    imp_ref[0] = _dot_exact_r(psum, a_ref[...])


def _cmp_s_call(page_table, nsa_pool, qn, cw, a_mat, tn):
    B, n_pages = page_table.shape
    past = n_pages * PAGE_SIZE
    consts = [cw["w1k"], cw["w1v"], cw["posk"], cw["posv"], cw["w2k"], cw["w2v"], cw["bd"], cw["gk"], a_mat]
    cspecs = [pl.BlockSpec(a.shape, (lambda b, pt, nd=a.ndim: (0,) * nd)) for a in consts]
    nsl = a_mat.shape[1]
    grid_spec = pltpu.PrefetchScalarGridSpec(
        num_scalar_prefetch=1,
        grid=(B,),
        in_specs=[pl.BlockSpec(memory_space=pl.ANY),
                  pl.BlockSpec((tn, 512), lambda b, pt: (b, 0))] + cspecs,
        out_specs=[pl.BlockSpec((1, N_HEADS * tn, LANES), lambda b, pt: (b, 0, 0)),
                   pl.BlockSpec((1, N_GROUPS * tn, nsl), lambda b, pt: (b, 0, 0))],
        scratch_shapes=[pltpu.VMEM((2, 2 * LANES, past), F32),
                        pltpu.VMEM((past, LANES), F32), pltpu.VMEM((past, LANES), F32),
                        pltpu.SemaphoreType.DMA((2,))],
    )
    return pl.pallas_call(
        functools.partial(_cmp_s_kernel, n_pages=n_pages, tn=tn),
        grid_spec=grid_spec,
        out_shape=[jax.ShapeDtypeStruct((B, N_HEADS * tn, LANES), F32),
                   jax.ShapeDtypeStruct((B, N_GROUPS * tn, nsl), F32)],
        compiler_params=_cparams(("arbitrary",)),
        name="compress_sample",
    )(page_table, nsa_pool, qn, *consts)


def _topk_s_kernel(imp_ref, o_ref, *, tn, past):
    rows = imp_ref.shape[0]
    t = past + lax.rem(lax.broadcasted_iota(jnp.int32, (rows, 1), 0), tn)
    sel = _select_blocks(imp_ref[...], t // SEL_BLOCK)
    o_ref[...] = ((sel - 1.0) * MASK_BIG).astype(BF16)


def _topk_s_call(imp2, tn, past):
    rows, nsl = imp2.shape
    tr = min(rows, 256)
    return pl.pallas_call(
        functools.partial(_topk_s_kernel, tn=tn, past=past),
        grid=(rows // tr,),
        in_specs=[pl.BlockSpec((tr, nsl), lambda i: (i, 0))],
        out_specs=pl.BlockSpec((tr, nsl), lambda i: (i, 0)),
        out_shape=jax.ShapeDtypeStruct((rows, nsl), BF16),
        compiler_params=_cparams(("arbitrary",)),
        name="select_sample",
    )(imp2)


def _sel_s_kernel(pt_ref, pool_ref, q_ref, nnew_ref, wnew_ref, win_ref, seln_ref, oc_ref, gate_ref, e_ref,
                  o_ref, kv_ref, sem_ref, *, n_pages, tn, kc):
    slot = _gather_pipeline(pt_ref, [(pool_ref, 2 * LANES, kv_ref, sem_ref)], n_pages)
    past = n_pages * PAGE_SIZE
    R = N_HEADS * tn
    bpc = kc // SEL_BLOCK
    qs = _stack_q(q_ref, tn)
    tcol = _tok_col(tn)
    slope = _slope_col(tn)
    tl = lax.broadcasted_iota(jnp.int32, (R, LANES), 1)
    newbias = jnp.where(tl <= tcol, -slope * (tcol - tl).astype(F32), NEG)

    def group_rows(x):
        n = x.shape[1]
        x4 = jnp.broadcast_to(x.reshape(N_GROUPS, 1, tn, n), (N_GROUPS, N_REP, tn, n))
        return x4.reshape(R, n)

    n_chunks = past // kc
    seln_new = seln_ref[0, n_chunks]
    mb_new = group_rows(_dot(seln_new, e_ref[...])[:, 0:LANES])
    knew = _pad_keys(nnew_ref[:, 2 * LANES:3 * LANES], BF16)
    vnew = _pad_keys(nnew_ref[:, 3 * LANES:4 * LANES], BF16)
    carry = _flash_step(qs, knew, vnew, jnp.where(tl <= tcol, newbias + mb_new, NEG), *_flash_init(R))
    for c in range(n_chunks):
        kt = kv_ref[slot, 0:LANES, c * kc:(c + 1) * kc].astype(BF16)
        vt = kv_ref[slot, LANES:2 * LANES, c * kc:(c + 1) * kc].astype(BF16)
        mb = group_rows(_dot(seln_ref[0, c], e_ref[...]))
        kpos = c * kc + lax.broadcasted_iota(jnp.int32, (1, kc), 1)
        dist = ((past + tcol) - kpos).astype(F32)
        carry = _flash_step_t(qs, kt, vt, mb - slope * dist, *carry)
    m, l, acc = carry
    o_s = acc / l

    wlen = win_ref.shape[2]
    knew = _pad_keys(wnew_ref[:, 0:LANES], BF16)
    vnew = _pad_keys(wnew_ref[:, LANES:2 * LANES], BF16)
    carry = _flash_step(qs, knew, vnew, newbias, *_flash_init(R))
    kt = win_ref[0, 0:LANES, :].astype(BF16)
    vt = win_ref[0, LANES:2 * LANES, :].astype(BF16)
    wpos = (past - wlen) + lax.broadcasted_iota(jnp.int32, (1, wlen), 1)
    dw = (past + tcol) - wpos
    bias = jnp.where((dw < WINDOW) & (wpos >= 0), -slope * dw.astype(F32), NEG)
    m, l, acc = _flash_step_t(qs, kt, vt, bias, *carry)
    o_w = acc / l

    gl = lax.broadcasted_iota(jnp.int32, (R, LANES), 1)
    hrow = lax.broadcasted_iota(jnp.int32, (R, LANES), 0) // tn
    gt = jnp.broadcast_to(gate_ref[...][None], (N_HEADS, tn, LANES)).reshape(R, LANES)

    def gcol(j):
        return jnp.sum(jnp.where(gl == GATE_LANE0 + 3 * hrow + j, gt, 0.0), axis=-1, keepdims=True)

    o = gcol(0) * oc_ref[0] + gcol(1) * o_s + gcol(2) * o_w
    o_ref[...] = _unstack_o(o, tn).astype(BF16)


def _sel_s_call(page_table, nsa_pool, qn, nrow, wrow, win_t, seln, oc, gates, e_mat, tn, kc):
    B, n_pages = page_table.shape
    past = n_pages * PAGE_SIZE
    wlen = win_t.shape[2]
    grid_spec = pltpu.PrefetchScalarGridSpec(
        num_scalar_prefetch=1,
        grid=(B,),
        in_specs=[pl.BlockSpec(memory_space=pl.ANY),
                  pl.BlockSpec((tn, 512), lambda b, pt: (b, 0)),
                  pl.BlockSpec((tn, 512), lambda b, pt: (b, 0)),
                  pl.BlockSpec((tn, 256), lambda b, pt: (b, 0)),
                  pl.BlockSpec((1, 2 * LANES, wlen), lambda b, pt: (b, 0, 0)),
                  pl.BlockSpec((1,) + seln.shape[1:], lambda b, pt: (b, 0, 0, 0)),
                  pl.BlockSpec((1, N_HEADS * tn, LANES), lambda b, pt: (b, 0, 0)),
                  pl.BlockSpec((tn, LANES), lambda b, pt: (b, 0)),
                  pl.BlockSpec(e_mat.shape, lambda b, pt: (0, 0))],
        out_specs=pl.BlockSpec((tn, 512), lambda b, pt: (b, 0)),
        scratch_shapes=[pltpu.VMEM((2, 2 * LANES, past), F32), pltpu.SemaphoreType.DMA((2,))],
    )
    return pl.pallas_call(
        functools.partial(_sel_s_kernel, n_pages=n_pages, tn=tn, kc=kc),
        grid_spec=grid_spec,
        out_shape=jax.ShapeDtypeStruct((B * tn, 512), BF16),
        compiler_params=_cparams(("arbitrary",)),
        name="select_attend_sample",
    )(page_table, nsa_pool, qn, nrow, wrow, win_t, seln, oc, gates, e_mat)


def _pair_cols(base):
    idx = []
    for r in range(N_REP):
        for g in range(N_GROUPS):
            h = g * N_REP + r
            idx.extend(range(base + h * HEAD_DIM, base + (h + 1) * HEAD_DIM))
    return np.asarray(idx, np.int32)


def _prep_weights(w_in, b_fox_f, fox_qn_g, fox_kn_g, nsa_qn_g, nsa_kn_slc_g, nsa_kn_win_g):
    o_fq, o_fk, o_fv, o_ff, o_nq, o_nkv, o_ng, o_mg = 0, 512, 640, 768, 776, 1288, 2056, 2080
    cols = np.concatenate([
        _pair_cols(o_fq), np.arange(o_fk, o_fk + 128), np.arange(o_fv, o_fv + 128),
        _pair_cols(o_nq), np.arange(o_nkv, o_nkv + 768),
        np.arange(o_ff, o_ff + 8), np.arange(o_ng, o_ng + 24)]).astype(np.int32)
    w_p = jnp.take(w_in, cols, axis=1)
    w_p = jnp.pad(w_p, ((0, 0), (0, _C_END - w_p.shape[1]))).astype(BF16)
    w_mg = w_in[:, o_mg:o_mg + 2 * D_MODEL].astype(BF16)
    tile2 = lambda g: jnp.tile(g, 2)
    gains = jnp.stack([tile2(fox_qn_g), tile2(fox_kn_g), tile2(nsa_qn_g), tile2(nsa_kn_slc_g),
                       tile2(nsa_kn_win_g)] + [jnp.zeros((LANES,), F32)] * 3)
    bff = jnp.zeros((1, LANES), F32).at[0, LOGF_LANE0:LOGF_LANE0 + N_HEADS].set(b_fox_f)
    return w_p, w_mg, gains, bff


def _prep_compress(pos, w1, w2):
    w1r = w1.reshape(2, CMP_STRIDE, HEAD_DIM, CMP_HIDDEN)
    z = jnp.zeros((CMP_STRIDE, HEAD_DIM, CMP_HIDDEN), F32)
    top = jnp.concatenate([w1r[0], z, w1r[1], z], axis=-1)
    bot = jnp.concatenate([z, w1r[0], z, w1r[1]], axis=-1)
    w1b = jnp.concatenate([top, bot], axis=1).astype(BF16)
    w1b = w1b.reshape(CMP_STRIDE * LANES, 4 * LANES)
    pr = pos.reshape(2, CMP_STRIDE, HEAD_DIM)
    posb = jnp.concatenate([jnp.tile(pr, (1, 1, 2)).transpose(1, 0, 2),
                            jnp.zeros((CMP_STRIDE, 6, LANES), F32)], axis=1).astype(BF16)
    posb = posb.transpose(1, 0, 2).reshape(8, CMP_STRIDE * LANES)
    zz = jnp.zeros((CMP_HIDDEN, HEAD_DIM), F32)
    w2b = jnp.concatenate([jnp.concatenate([w2, zz], axis=1),
                           jnp.concatenate([zz, w2], axis=1)], axis=0).astype(BF16)
    return w1b, posb, w2b


def kernel(x_prompt, x_sample, cache_fox_kv, cache_fox_logf, cache_nsa_kv, state_win_kv, page_table,
           c_prompt, c_sample, norm1_g, norm2_g, w_ada, b_ada, w_in, b_fox_f, fox_qn_g, fox_kn_g,
           nsa_qn_g, nsa_kn_cmp_g, nsa_kn_slc_g, nsa_kn_win_g, cmp_pos_k, cmp_w1_k, cmp_w2_k,
           cmp_pos_v, cmp_w1_v, cmp_w2_v, w_br_fox, w_br_nsa, w_out, w_up, w_down):
    assert norm1_g.shape[0] == 1
    B, T, d = x_prompt.shape
    DB, TN, _ = x_sample.shape
    n_phys = cache_fox_kv.shape[1]
    n_pages = page_table.shape[1]
    past = n_pages * PAGE_SIZE
    wlen = state_win_kv.shape[2]
    assert T % 256 == 0 and T >= WINDOW and wlen == WINDOW and TN == 8 and past % CMP_STRIDE == 0

    w_p, w_mg, gains, bff = _prep_weights(w_in[0], b_fox_f[0], fox_qn_g[0], fox_kn_g[0], nsa_qn_g[0],
                                          nsa_kn_slc_g[0], nsa_kn_win_g[0])
    w1k, posk, w2k = _prep_compress(cmp_pos_k[0], cmp_w1_k[0], cmp_w2_k[0])
    w1v, posv, w2v = _prep_compress(cmp_pos_v[0], cmp_w1_v[0], cmp_w2_v[0])
    bd = jnp.asarray(np.kron(np.eye(2), np.ones((HEAD_DIM, HEAD_DIM))), BF16)
    cw = dict(w1k=w1k, w1v=w1v, posk=posk, posv=posv, w2k=w2k, w2v=w2v, bd=bd,
              gk=jnp.tile(nsa_kn_cmp_g[0], 2).reshape(1, LANES))
    pair_rows = _pair_cols(0)
    wbf = jnp.take(w_br_fox[0], pair_rows, axis=0).astype(BF16)
    wbn = jnp.take(w_br_nsa[0], pair_rows, axis=0).astype(BF16)
    wout = w_out[0].astype(BF16)
    wup = w_up[0].astype(BF16)
    wdn = w_down[0].astype(BF16)
    g1 = norm1_g[0].reshape(1, d)
    g2 = norm2_g[0].reshape(1, d)
    tm_p = 512
    tri = jnp.asarray(np.tril(np.ones((tm_p, tm_p), np.float32)), BF16)

    mod = _ada_call(jnp.concatenate([c_prompt, c_sample], axis=0), w_ada[0], b_ada[0])
    mod_p = mod[:B].reshape(B, 1, 6 * d)
    mod_s = mod[B:].reshape(DB, 1, 6 * d)

    (qf, frow, qn, nrow, wrow, gates, lf, c_tm, kf, vf, ks, vs, kw, vw) = _pre_call(
        x_prompt, mod_p, g1, w_p, bd, tri, gains, bff, nb=1, tt=tm_p, do_cum=True)
    tq, tk, tw = 128, 512, 256
    o_fox = _fox_p_call(qf, kf, vf, c_tm, B, T, tq, tk)
    kc_p, vc_p = _cmp_p_call(nrow, cw, B, T)
    nc_p = T // CMP_STRIDE
    nsel_p = -(-T // SEL_BLOCK)
    assert nsel_p <= LANES - _AUG_SEL0 and T % tk == 0
    a_p = jnp.asarray(_importance_matrix(nc_p, LANES, nc_p - 1)[:, :LANES] *
                      (np.arange(LANES) < nsel_p)[None, :], BF16)
    o_nsa = _nsa_p_call(qn, ks, vs, kw, vw, kc_p, vc_p, gates, a_p, B, T, tq, tk, tw)
    x1 = _mix_call(x_prompt, o_fox, o_nsa, mod_p, g1, w_mg, wbf, wbn, wout, nb=1, tt=tm_p)
    y_prompt = _ffn_call(x1, mod_p, g2, wup, wdn, nb=1, tt=tm_p)

    nb_s = min(DB, 32)
    (qf_s, frow_s, qn_s, nrow_s, wrow_s, gates_s, lf_s) = _pre_call(
        x_sample, mod_s, g1, w_p, bd, tri, gains, bff, nb=nb_s, tt=TN, do_cum=False)

    fox_t = jnp.transpose(cache_fox_kv[0], (0, 2, 3, 4, 1)).reshape(n_phys, 2 * LANES, PAGE_SIZE)
    nsa_t = jnp.transpose(cache_nsa_kv[0], (0, 2, 3, 4, 1)).reshape(n_phys, 4 * LANES, PAGE_SIZE)
    logf_t = jnp.transpose(cache_fox_logf[0], (0, 2, 1))
    win_t = jnp.transpose(state_win_kv[0], (0, 2, 3, 4, 1)).reshape(DB, 2 * LANES, wlen)

    lf_new = lf_s[:, LOGF_LANE0:LOGF_LANE0 + N_HEADS].reshape(DB, TN, N_HEADS).transpose(0, 2, 1)
    lf_new = jnp.pad(lf_new, ((0, 0), (0, 0), (0, LANES - TN)))
    o_fox_s = _fox_s_call(page_table, fox_t, logf_t, qf_s, frow_s, lf_new, TN)

    nsa_pool = nsa_t
    nc_s = past // CMP_STRIDE
    nsel_s = -(-(past + TN) // SEL_BLOCK)
    nsl = -(-nsel_s // LANES) * LANES
    a_s = jnp.asarray(_importance_matrix(nc_s, nsl, nc_s - 1) * (np.arange(nsl) < nsel_s)[None, :], BF16)
    oc_s, imp_s = _cmp_s_call(page_table, nsa_pool, qn_s, cw, a_s, TN)
    seln = _topk_s_call(imp_s.reshape(DB * N_GROUPS * TN, nsl), TN, past)
    kc_keys = min(past, 2048)
    bpc = kc_keys // SEL_BLOCK
    seln = seln.reshape(DB, N_GROUPS * TN, nsl // bpc, bpc).transpose(0, 2, 1, 3)
    e_s = jnp.asarray((np.arange(kc_keys)[None, :] // SEL_BLOCK) == np.arange(bpc)[:, None], BF16)
    o_nsa_s = _sel_s_call(page_table, nsa_pool, qn_s, nrow_s, wrow_s, win_t, seln, oc_s, gates_s, e_s,
                          TN, kc_keys)
    x1_s = _mix_call(x_sample, o_fox_s, o_nsa_s, mod_s, g1, w_mg, wbf, wbn, wout, nb=nb_s, tt=TN)
    y_sample = _ffn_call(x1_s, mod_s, g2, wup, wdn, nb=nb_s, tt=TN)

    lf_p = lf[:, LOGF_LANE0:LOGF_LANE0 + N_HEADS]
    win_p = wrow.reshape(B, T, 256)[:, T - wlen:]
    win_s = jnp.concatenate([state_win_kv[0, :, TN:],
                             wrow_s.reshape(DB, TN, 2, N_GROUPS, HEAD_DIM)], axis=1)
    return (y_prompt, y_sample,
            frow.reshape(1, B, T, 2, N_GROUPS, HEAD_DIM),
            lf_p.reshape(1, B, T, N_HEADS),
            nrow.reshape(1, B, T, 4, N_GROUPS, HEAD_DIM),
            win_p.reshape(1, B, wlen, 2, N_GROUPS, HEAD_DIM),
            frow_s.reshape(1, DB, TN, 2, N_GROUPS, HEAD_DIM),
            lf_s[:, LOGF_LANE0:LOGF_LANE0 + N_HEADS].reshape(1, DB, TN, N_HEADS),
            nrow_s.reshape(1, DB, TN, 4, N_GROUPS, HEAD_DIM),
            win_s[None])
```

```python
import functools

import numpy as np
import jax
import jax.numpy as jnp
from jax import lax
from jax.experimental import pallas as pl
from jax.experimental.pallas import tpu as pltpu

F32 = jnp.float32
BF16 = jnp.bfloat16

D_MODEL = 1024
HEAD_DIM = 64
N_HEADS = 8
N_GROUPS = 2
N_REP = N_HEADS // N_GROUPS
PAGE_SIZE = 128
CMP_STRIDE = 16
CMP_HIDDEN = 2 * HEAD_DIM
SEL_BLOCK = 64
N_SELECT = 16
WINDOW = 512
D_FF = 4 * D_MODEL
RMS_EPS = 1e-6
FORCE_BONUS = 1.0e4
LANES = 128
NEG = -1e30
MASK_BIG = 2.0 ** 100
LOGF_LANE0 = 0
GATE_LANE0 = 8
VMEM_LIMIT = 56 * 1024 * 1024


def _dot(a, b):
    return jnp.dot(a, b, preferred_element_type=F32)


def _dot_nt(a, b):
    return lax.dot_general(a, b, (((1,), (1,)), ((), ())), preferred_element_type=F32)


def _split3(x):
    x1 = x.astype(BF16)
    r = x - x1.astype(F32)
    x2 = r.astype(BF16)
    x3 = (r - x2.astype(F32)).astype(BF16)
    return x1, x2, x3


def _dot_exact_r(x, m):
    a, b, c = _split3(x)
    return _dot(a, m) + _dot(b, m) + _dot(c, m)


def _dot_exact_l(m, x):
    a, b, c = _split3(x)
    return _dot(m, a) + _dot(m, b) + _dot(m, c)


def _sigmoid(x):
    return 1.0 / (1.0 + jnp.exp(-x))


def _head_rms(zc, bd):
    a = zc * zc
    a1 = a.astype(BF16)
    a2 = (a - a1.astype(F32)).astype(BF16)
    ss = _dot(a1, bd) + _dot(a2, bd)
    return zc * lax.rsqrt(ss * (1.0 / HEAD_DIM) + RMS_EPS)


def _cparams(sem, vmem=VMEM_LIMIT):
    return pltpu.CompilerParams(dimension_semantics=sem, vmem_limit_bytes=vmem)


def _ada_kernel(c_ref, w_ref, b_ref, o_ref):
    c = c_ref[...]
    a = c * _sigmoid(c)
    o_ref[...] = _dot(a.astype(BF16), w_ref[...].astype(BF16)) + b_ref[...]


def _ada_call(c, w_ada, b_ada):
    r, d = c.shape
    n = w_ada.shape[1]
    tn = 1024
    return pl.pallas_call(
        _ada_kernel,
        grid=(n // tn,),
        in_specs=[pl.BlockSpec((r, d), lambda j: (0, 0)),
                  pl.BlockSpec((d, tn), lambda j: (0, j)),
                  pl.BlockSpec((1, tn), lambda j: (0, j))],
        out_specs=pl.BlockSpec((r, tn), lambda j: (0, j)),
        out_shape=jax.ShapeDtypeStruct((r, n), F32),
        compiler_params=_cparams(("arbitrary",)),
        name="ada_mod",
    )(c, w_ada, b_ada.reshape(1, n))


_C_FQ, _C_FK, _C_FV, _C_NQ, _C_NKV, _C_SMALL, _C_END = 0, 512, 640, 768, 1280, 2048, 2176


def _v_with_ones(v):
    lane = lax.broadcasted_iota(jnp.int32, v.shape, 1)
    return jnp.concatenate([jnp.where(lane < HEAD_DIM, v, 1.0), jnp.where(lane < HEAD_DIM, 1.0, v)],
                           axis=1).astype(BF16)


def _pos_hi_lo(pos):
    return (pos >> 8).astype(F32), (pos & 255).astype(F32)


_AUG_SEL0 = 4


def _key_pos_features(pos, lane):
    hi, lo = _pos_hi_lo(pos)
    return jnp.where(lane < 2, 1.0,
                     jnp.where(lane == 2, hi,
                               jnp.where(lane == 3, lo,
                                         jnp.where(lane - _AUG_SEL0 == pos // SEL_BLOCK, 1.0, 0.0))))


def _query_pos_features(pos, slope, lane):
    hi, lo = _pos_hi_lo(pos)
    return jnp.where(lane == 0, -(slope * 256.0) * hi,
                     jnp.where(lane == 1, -slope * lo,
                               jnp.where(lane == 2, slope * 256.0,
                                         jnp.where(lane == 3, slope, 0.0))))


def _pre_kernel(x_ref, sc_ref, sh_ref, g1_ref, w_ref, bd_ref, tri_ref, gains_ref, bff_ref,
                qf_ref, frow_ref, qn_ref, nrow_ref, wrow_ref, gate_ref, lf_ref,
                *rest, do_cum):
    nb, tt, d = x_ref.shape
    tm = nb * tt
    x = x_ref[...]
    ms = jnp.mean(x * x, axis=-1, keepdims=True)
    h = x * lax.rsqrt(ms + RMS_EPS) * g1_ref[...] * (1.0 + sc_ref[...]) + sh_ref[...]
    h = h.reshape(tm, d).astype(BF16)
    bd = bd_ref[...]

    def proj(c0):
        return _dot(h, w_ref[:, c0:c0 + LANES])

    scale = HEAD_DIM ** -0.5
    for r in range(N_REP):
        qf_ref[:, r * LANES:(r + 1) * LANES] = (
            _head_rms(proj(_C_FQ + r * LANES), bd) * (gains_ref[0:1, :] * scale)).astype(BF16)
        qn_ref[:, r * LANES:(r + 1) * LANES] = (
            _head_rms(proj(_C_NQ + r * LANES), bd) * (gains_ref[2:3, :] * scale)).astype(BF16)

    def put(ref, i, val):
        if do_cum:
            ref[0, i * LANES:(i + 1) * LANES, :] = val.T
        else:
            ref[:, i * LANES:(i + 1) * LANES] = val

    fk = _head_rms(proj(_C_FK), bd) * gains_ref[1:2, :]
    fv = proj(_C_FV)
    put(frow_ref, 0, fk)
    put(frow_ref, 1, fv)

    ck = proj(_C_NKV)
    cv = proj(_C_NKV + LANES)
    sk = _head_rms(proj(_C_NKV + 2 * LANES), bd) * gains_ref[3:4, :]
    sv = proj(_C_NKV + 3 * LANES)
    wk = _head_rms(proj(_C_NKV + 4 * LANES), bd) * gains_ref[4:5, :]
    wv = proj(_C_NKV + 5 * LANES)
    for i, val in enumerate((ck, cv, sk, sv)):
        put(nrow_ref, i, val)
    put(wrow_ref, 0, wk)
    put(wrow_ref, 1, wv)

    zl = proj(_C_SMALL)
    gate_ref[...] = _sigmoid(zl)
    xl = zl + bff_ref[...]
    lf = jnp.minimum(xl, 0.0) - jnp.log1p(jnp.exp(-jnp.abs(xl)))
    lf_ref[...] = lf

    if do_cum:
        c_ref, kf_ref, vf_ref, ks_ref, vs_ref, kw_ref, vw_ref, nraw_ref, carry_ref = rest
        nraw_ref[:, 0:LANES] = ck
        nraw_ref[:, LANES:2 * LANES] = cv

        @pl.when(pl.program_id(1) == 0)
        def _():
            carry_ref[...] = jnp.zeros_like(carry_ref)

        c = _dot_exact_l(tri_ref[...], lf) + carry_ref[0:1, :]
        c_ref[...] = c
        carry_ref[...] = jnp.broadcast_to(c[tm - 1:tm, :], carry_ref.shape)

        lane = lax.broadcasted_iota(jnp.int32, (tm, LANES), 1)
        c1, c2, c3 = (p.astype(F32) for p in _split3(c))
        aug_f = jnp.where(lane < N_HEADS, -c1,
                          jnp.where(lane < 2 * N_HEADS, -pltpu.roll(c2, N_HEADS, 1),
                                    jnp.where(lane < 3 * N_HEADS, -pltpu.roll(c3, 2 * N_HEADS, 1),
                                              jnp.where(lane < 3 * N_HEADS + 3, 1.0, 0.0))))
        kf_ref[:, 0:LANES] = fk.astype(BF16)
        kf_ref[:, LANES:2 * LANES] = aug_f.astype(BF16)
        vf_ref[...] = _v_with_ones(fv)

        pos = pl.program_id(1) * tm + lax.broadcasted_iota(jnp.int32, (tm, 1), 0)
        aug_p = _key_pos_features(pos, lane).astype(BF16)
        ks_ref[:, 0:LANES] = sk.astype(BF16)
        ks_ref[:, LANES:2 * LANES] = aug_p
        vs_ref[...] = _v_with_ones(sv)
        kw_ref[:, 0:LANES] = wk.astype(BF16)
        kw_ref[:, LANES:2 * LANES] = aug_p
        vw_ref[...] = _v_with_ones(wv)


def _pre_call(x3, mod3, g1, w_p, bd, tri, gains, bff, nb, tt, do_cum):
    NB, TT, d = x3.shape
    tm = nb * tt
    n = NB * TT
    gi, gj = NB // nb, TT // tt
    tok = lambda i, j: (i * gj + j, 0)
    full = lambda i, j: (0, 0)
    in_specs = [
        pl.BlockSpec((nb, tt, d), lambda i, j: (i, j, 0)),
        pl.BlockSpec((nb, 1, d), lambda i, j: (i, 0, 1)),
        pl.BlockSpec((nb, 1, d), lambda i, j: (i, 0, 0)),
        pl.BlockSpec((1, d), full),
        pl.BlockSpec(w_p.shape, full),
        pl.BlockSpec(bd.shape, full),
        pl.BlockSpec(tri.shape, full),
        pl.BlockSpec(gains.shape, full),
        pl.BlockSpec(bff.shape, full),
    ]
    def rows_out(width):
        if do_cum:
            return (jax.ShapeDtypeStruct((NB, width, TT), F32),
                    pl.BlockSpec((1, width, tm), lambda i, j: (i, 0, j)))
        return jax.ShapeDtypeStruct((n, width), F32), pl.BlockSpec((tm, width), tok)

    def tok_out(width, dtype):
        return jax.ShapeDtypeStruct((n, width), dtype), pl.BlockSpec((tm, width), tok)

    outs = [
        tok_out(512, BF16),
        rows_out(256),
        tok_out(512, BF16),
        rows_out(512),
        rows_out(256),
        tok_out(LANES, F32),
        tok_out(LANES, F32),
    ]
    scratch = []
    if do_cum:
        assert nb == 1
        outs += [tok_out(LANES, F32)]
        outs += [tok_out(256, BF16)] * 6
        outs += [tok_out(256, F32)]
        scratch = [pltpu.VMEM((8, LANES), F32)]
    out_shape = [o[0] for o in outs]
    out_specs = [o[1] for o in outs]
    return pl.pallas_call(
        functools.partial(_pre_kernel, do_cum=do_cum),
        grid=(gi, gj),
        in_specs=in_specs,
        out_specs=out_specs,
        out_shape=out_shape,
        scratch_shapes=scratch,
        compiler_params=_cparams(("arbitrary", "arbitrary")),
        name="pre_mixer",
    )(x3, mod3, mod3, g1, w_p, bd, tri, gains, bff)


def _half_mask(rows, g):
    lane = lax.broadcasted_iota(jnp.int32, (rows, LANES), 1)
    return (lane < HEAD_DIM) if g == 0 else (lane >= HEAD_DIM)


def _slope(h):
    return 2.0 ** (-8.0 * (h + 1) / N_HEADS)


def _select_blocks(imp, cur):
    blk = lax.broadcasted_iota(jnp.int32, imp.shape, 1)
    forced = (blk == 0) | (blk == cur) | (blk == cur - 1)
    score = jnp.where(blk <= cur, imp + jnp.where(forced, FORCE_BONUS, 0.0), -jnp.inf)
    blkf = blk.astype(F32)

    def body(_, carry):
        sc, sel = carry
        mx = jnp.max(sc, axis=-1, keepdims=True)
        idx = jnp.min(jnp.where(sc == mx, blkf, 1e9), axis=-1, keepdims=True)
        pick = blkf == idx
        return jnp.where(pick, -jnp.inf, sc), jnp.where(pick, 1.0, sel)

    _, sel = lax.fori_loop(0, N_SELECT, body, (score, jnp.zeros(imp.shape, F32)))
    return jnp.where(blk <= cur, sel, 0.0)


def _select_blocks_by_rank(imp, cur, n_blocks):
    blk = lax.broadcasted_iota(jnp.int32, imp.shape, 1)
    forced = (blk == 0) | (blk == cur) | (blk == cur - 1)
    score = jnp.where(blk <= cur, imp + jnp.where(forced, FORCE_BONUS, 0.0), -jnp.inf)
    beaten_by = jnp.zeros(imp.shape, F32)
    for i in range(n_blocks):
        si = score[:, i:i + 1]
        beaten_by = beaten_by + jnp.where((si > score) | ((si == score) & (blk > i)), 1.0, 0.0)
    return jnp.where((beaten_by < N_SELECT) & (blk <= cur), 1.0, 0.0)


def _importance_matrix(n_cmp_rows, n_sel_cols, n_cmp):
    r = SEL_BLOCK // CMP_STRIDE
    a = np.zeros((n_cmp_rows, n_sel_cols), np.float32)
    for n in range(n_cmp):
        for j in range(n_sel_cols):
            off = n - r * j
            if off in (-1, r - 1):
                a[n, j] = 0.5
            elif 0 <= off <= r - 2:
                a[n, j] = 1.0
    return a


def _stack_q_features(q_ref, qa_ref, tq, feat_fn):
    for g in range(N_GROUPS):
        for r in range(N_REP):
            h = g * N_REP + r
            qc = q_ref[:, r * LANES:(r + 1) * LANES]
            qa_ref[h * tq:(h + 1) * tq, 0:LANES] = jnp.where(_half_mask(tq, g), qc, jnp.zeros_like(qc))
            qa_ref[h * tq:(h + 1) * tq, LANES:2 * LANES] = feat_fn(h).astype(BF16)


def _stacked_attend(qa, k_ref, v_ref, tk, first, lo, mask_first, mask_rest):
    R = qa.shape[0]
    half = R // 2

    def step(j, mask, m, acc):
        st = pl.multiple_of(j * tk, tk)
        s = mask(_dot_nt(qa, k_ref[pl.ds(st, tk), :]))
        va = v_ref[pl.ds(st, tk), :]
        m_new = jnp.maximum(m, jnp.max(s, axis=-1, keepdims=True))
        alpha = jnp.exp(m - m_new)
        p = jnp.exp(s - m_new).astype(BF16)
        pv = jnp.concatenate([_dot(p[:half], va[:, 0:LANES]), _dot(p[half:], va[:, LANES:2 * LANES])], axis=0)
        return m_new, alpha * acc + pv

    carry = step(first, mask_first, jnp.full((R, 1), NEG, F32), jnp.zeros((R, LANES), F32))
    m, acc = lax.fori_loop(lo, first, lambda j, c: step(j, lambda s: mask_rest(j, s), *c), carry)
    return acc / pltpu.roll(acc, HEAD_DIM, 1)


def _row_pos(t0, tq):
    return t0 + lax.rem(lax.broadcasted_iota(jnp.int32, (N_HEADS * tq, 1), 0), tq)


def _fox_p_kernel(q_ref, ka_ref, va_ref, c_ref, o_ref, qa_ref, *, tq, tk):
    qi = pl.program_id(1)
    t0 = qi * tq
    R = N_HEADS * tq
    lane = lax.broadcasted_iota(jnp.int32, (tq, LANES), 1)
    c1, c2, c3 = (p.astype(F32) for p in _split3(c_ref[...]))

    def q_features(h):
        own = (lane == h) | (lane == N_HEADS + h) | (lane == 2 * N_HEADS + h)
        return jnp.where(lane == 3 * N_HEADS, c1[:, h:h + 1],
                         jnp.where(lane == 3 * N_HEADS + 1, c2[:, h:h + 1],
                                   jnp.where(lane == 3 * N_HEADS + 2, c3[:, h:h + 1],
                                             jnp.where(own, 1.0, 0.0))))

    _stack_q_features(q_ref, qa_ref, tq, q_features)
    qa = qa_ref[...]
    tpos = _row_pos(t0, tq)

    jd = t0 // tk
    kpos = jd * tk + lax.broadcasted_iota(jnp.int32, (1, tk), 1)
    o = _stacked_attend(qa, ka_ref, va_ref, tk, jd, 0,
                        lambda s: jnp.where(kpos <= tpos, s, NEG), lambda j, s: s)
    o_ref[...] = _unstack_o(o, tq).astype(BF16)


def _fox_p_call(qf, ka, va, c_tm, B, T, tq, tk):
    n = B * T
    nq = T // tq
    return pl.pallas_call(
        functools.partial(_fox_p_kernel, tq=tq, tk=tk),
        grid=(B, nq),
        in_specs=[pl.BlockSpec((tq, 512), lambda b, i: (b * nq + i, 0)),
                  pl.BlockSpec((T, 256), lambda b, i: (b, 0)),
                  pl.BlockSpec((T, 256), lambda b, i: (b, 0)),
                  pl.BlockSpec((tq, LANES), lambda b, i: (b * nq + i, 0))],
        out_specs=pl.BlockSpec((tq, 512), lambda b, i: (b * nq + i, 0)),
        out_shape=jax.ShapeDtypeStruct((n, 512), BF16),
        scratch_shapes=[pltpu.VMEM((N_HEADS * tq, 2 * LANES), BF16)],
        compiler_params=_cparams(("arbitrary", "arbitrary")),
        name="fox_prompt",
    )(qf, ka, va, c_tm)


def _chunk_dot(x_ref, chunk0, n, w1_ref):
    rows = jnp.concatenate(
        [x_ref[pl.ds(chunk0 * CMP_STRIDE + c, n, stride=CMP_STRIDE), :].astype(BF16)
         for c in range(CMP_STRIDE)], axis=1)
    return _dot(rows, w1_ref[...])


def _compress(xk_ref, xv_ref, n, w1k_ref, w1v_ref, posk_ref, posv_ref, w2k_ref, w2v_ref, bd, gk):
    return _compress_finish(_chunk_dot(xk_ref, 0, n, w1k_ref), _chunk_dot(xv_ref, 0, n, w1v_ref), n,
                            w1k_ref, w1v_ref, posk_ref, posv_ref, w2k_ref, w2v_ref, bd, gk)


def _compress_finish(acc_k, acc_v, n, w1k_ref, w1v_ref, posk_ref, posv_ref, w2k_ref, w2v_ref, bd, gk):
    ck = _dot(posk_ref[...], w1k_ref[...])
    cv = _dot(posv_ref[...], w1v_ref[...])

    def finish(acc, cst, w2_ref):
        lead = acc[:, 0:2 * LANES] + cst[0:1, 0:2 * LANES]
        trail = acc[:, 2 * LANES:4 * LANES] + cst[1:2, 2 * LANES:4 * LANES]
        hid = lead + pltpu.roll(trail, n - 1, 0)
        act = hid * _sigmoid(hid)
        return _dot(act.astype(BF16), w2_ref[...])

    kc = _head_rms(finish(acc_k, ck, w2k_ref), bd) * gk
    vc = finish(acc_v, cv, w2v_ref)
    return kc, vc


def _cmp_p_kernel(xk_ref, xv_ref, w1k_ref, w1v_ref, posk_ref, posv_ref, w2k_ref, w2v_ref, bd_ref, gk_ref,
                  kc_ref, vc_ref):
    n = kc_ref.shape[0]
    kc, vc = _compress(xk_ref, xv_ref, n, w1k_ref, w1v_ref, posk_ref, posv_ref, w2k_ref, w2v_ref,
                       bd_ref[...], gk_ref[...])
    kc_ref[:, 0:LANES] = kc.astype(BF16)
    c_end = lax.broadcasted_iota(jnp.int32, (n, 1), 0) * CMP_STRIDE + (2 * CMP_STRIDE - 1)
    lane = lax.broadcasted_iota(jnp.int32, (n, LANES), 1)
    kc_ref[:, LANES:2 * LANES] = _key_pos_features(c_end, lane).astype(BF16)
    vc_ref[...] = vc.astype(BF16)


def _cmp_p_call(nrow, cw, B, T):
    nc = T // CMP_STRIDE
    consts = [cw["w1k"], cw["w1v"], cw["posk"], cw["posv"], cw["w2k"], cw["w2v"], cw["bd"], cw["gk"]]
    cspecs = [pl.BlockSpec(a.shape, (lambda b, nd=a.ndim: (0,) * nd)) for a in consts]
    return pl.pallas_call(
        _cmp_p_kernel,
        grid=(B,),
        in_specs=[pl.BlockSpec((T, LANES), lambda b: (b, 0)),
                  pl.BlockSpec((T, LANES), lambda b: (b, 1))] + cspecs,
        out_specs=[pl.BlockSpec((nc, 2 * LANES), lambda b: (b, 0)),
                   pl.BlockSpec((nc, LANES), lambda b: (b, 0))],
        out_shape=[jax.ShapeDtypeStruct((B * nc, 2 * LANES), BF16),
                   jax.ShapeDtypeStruct((B * nc, LANES), BF16)],
        compiler_params=_cparams(("arbitrary",)),
        name="compress_prompt",
    )(nrow, nrow, *consts)


def _gate_cols(gate_ref, tn):
    R = N_HEADS * tn
    gl = lax.broadcasted_iota(jnp.int32, (R, LANES), 1)
    hrow = lax.broadcasted_iota(jnp.int32, (R, LANES), 0) // tn
    gt = jnp.broadcast_to(gate_ref[...][None], (N_HEADS, tn, LANES)).reshape(R, LANES)
    return [jnp.sum(jnp.where(gl == GATE_LANE0 + 3 * hrow + j, gt, 0.0), axis=-1, keepdims=True)
            for j in range(3)]


def _nsa_p_kernel(q_ref, ks_ref, vs_ref, kw_ref, vw_ref, kc_ref, vc_ref, gate_ref, a_ref, o_ref, qa_ref,
                  *, tq, tk, tw, n_sel):
    qi = pl.program_id(1)
    t0 = qi * tq
    R = N_HEADS * tq
    nc = kc_ref.shape[0]
    lane = lax.broadcasted_iota(jnp.int32, (tq, LANES), 1)
    pos_q = t0 + lax.broadcasted_iota(jnp.int32, (tq, 1), 0)
    _stack_q_features(q_ref, qa_ref, tq, lambda h: _query_pos_features(pos_q, _slope(h), lane))
    qa = qa_ref[...]
    tpos = _row_pos(t0, tq)

    nid = lax.broadcasted_iota(jnp.int32, (1, nc), 1)
    cvalid = (nid * CMP_STRIDE + (2 * CMP_STRIDE - 1) <= tpos) & (nid < nc - 1)
    s = jnp.where(cvalid, _dot_nt(qa, kc_ref[...]), NEG)
    mx = jnp.max(s, axis=-1, keepdims=True)
    p = jnp.where(cvalid, jnp.exp(s - mx), 0.0)
    den = jnp.sum(p, axis=-1, keepdims=True)
    p = p / jnp.where(den > 0, den, 1.0)
    o_c = _dot(p.astype(BF16), vc_ref[...])
    p4 = p.reshape(N_GROUPS, N_REP, tq, nc)
    psum = (p4[:, 0] + p4[:, 1] + p4[:, 2] + p4[:, 3]).reshape(N_GROUPS * tq, nc)

    def in_window(j, s):
        kpos = j * tw + lax.broadcasted_iota(jnp.int32, (1, tw), 1)
        return jnp.where((kpos <= tpos) & (tpos - kpos < WINDOW), s, NEG)

    jw = t0 // tw
    o_w = _stacked_attend(qa, kw_ref, vw_ref, tw, jw, jnp.maximum(t0 - (WINDOW - 1), 0) // tw,
                          lambda s: in_window(jw, s), in_window)

    imp = _dot_exact_r(psum, a_ref[...])
    cur = (t0 + lax.rem(lax.broadcasted_iota(jnp.int32, (N_GROUPS * tq, 1), 0), tq)) // SEL_BLOCK
    seln = pltpu.roll((_select_blocks_by_rank(imp, cur, n_sel) - 1.0) * MASK_BIG, _AUG_SEL0, 1)
    for g in range(N_GROUPS):
        for r in range(N_REP):
            h = g * N_REP + r
            feat = jnp.where(lane >= _AUG_SEL0, seln[g * tq:(g + 1) * tq],
                             _query_pos_features(pos_q, _slope(h), lane))
            qa_ref[h * tq:(h + 1) * tq, LANES:2 * LANES] = feat.astype(BF16)
    qa = qa_ref[...]

    js = t0 // tk
    kpos = js * tk + lax.broadcasted_iota(jnp.int32, (1, tk), 1)
    o_s = _stacked_attend(qa, ks_ref, vs_ref, tk, js, 0,
                          lambda s: jnp.where(kpos <= tpos, s, NEG), lambda j, s: s)

    gc, gs, gw = _gate_cols(gate_ref, tq)
    o_ref[...] = _unstack_o(gc * o_c + gs * o_s + gw * o_w, tq).astype(BF16)


def _nsa_p_call(qn, ks, vs, kw, vw, kc, vc, gates, a_mat, B, T, tq, tk, tw):
    n = B * T
    nq = T // tq
    nc = T // CMP_STRIDE
    seq = lambda b, i: (b, 0)
    blk = lambda b, i: (b * nq + i, 0)
    return pl.pallas_call(
        functools.partial(_nsa_p_kernel, tq=tq, tk=tk, tw=tw, n_sel=-(-T // SEL_BLOCK)),
        grid=(B, nq),
        in_specs=[pl.BlockSpec((tq, 512), blk),
                  pl.BlockSpec((T, 256), seq), pl.BlockSpec((T, 256), seq),
                  pl.BlockSpec((T, 256), seq), pl.BlockSpec((T, 256), seq),
                  pl.BlockSpec((nc, 2 * LANES), seq),
                  pl.BlockSpec((nc, LANES), seq),
                  pl.BlockSpec((tq, LANES), blk),
                  pl.BlockSpec(a_mat.shape, lambda b, i: (0, 0))],
        out_specs=pl.BlockSpec((tq, 512), blk),
        out_shape=jax.ShapeDtypeStruct((n, 512), BF16),
        scratch_shapes=[pltpu.VMEM((N_HEADS * tq, 2 * LANES), BF16)],
        compiler_params=_cparams(("arbitrary", "arbitrary")),
        name="nsa_prompt",
    )(qn, ks, vs, kw, vw, kc, vc, gates, a_mat)


def _mix_kernel(x_ref, of_ref, on_ref, sc_ref, sh_ref, gt_ref, g1_ref, wmg_ref, wbf_ref, wbn_ref,
                wout_ref, o_ref):
    nb, tt, d = x_ref.shape
    tm = nb * tt
    x = x_ref[...]
    ms = jnp.mean(x * x, axis=-1, keepdims=True)
    h = x * lax.rsqrt(ms + RMS_EPS) * g1_ref[...] * (1.0 + sc_ref[...]) + sh_ref[...]
    h = h.reshape(tm, d).astype(BF16)
    g_fox = _sigmoid(_dot(h, wmg_ref[:, 0:d]))
    g_nsa = _sigmoid(_dot(h, wmg_ref[:, d:2 * d]))
    mix = g_fox * _dot(of_ref[...], wbf_ref[...]) + g_nsa * _dot(on_ref[...], wbn_ref[...])
    y = _dot(mix.astype(BF16), wout_ref[...]).reshape(nb, tt, d)
    o_ref[...] = x + gt_ref[...] * y


def _mix_call(x3, o_fox, o_nsa, mod3, g1, wmg, wbf, wbn, wout, nb, tt):
    NB, TT, d = x3.shape
    tm = nb * tt
    gi, gj = NB // nb, TT // tt
    full = lambda i, j: (0, 0)
    tok = lambda i, j: (i * gj + j, 0)
    return pl.pallas_call(
        _mix_kernel,
        grid=(gi, gj),
        in_specs=[pl.BlockSpec((nb, tt, d), lambda i, j: (i, j, 0)),
                  pl.BlockSpec((tm, 512), tok),
                  pl.BlockSpec((tm, 512), tok),
                  pl.BlockSpec((nb, 1, d), lambda i, j: (i, 0, 1)),
                  pl.BlockSpec((nb, 1, d), lambda i, j: (i, 0, 0)),
                  pl.BlockSpec((nb, 1, d), lambda i, j: (i, 0, 2)),
                  pl.BlockSpec((1, d), full),
                  pl.BlockSpec(wmg.shape, full),
                  pl.BlockSpec(wbf.shape, full),
                  pl.BlockSpec(wbn.shape, full),
                  pl.BlockSpec(wout.shape, full)],
        out_specs=pl.BlockSpec((nb, tt, d), lambda i, j: (i, j, 0)),
        out_shape=jax.ShapeDtypeStruct((NB, TT, d), F32),
        compiler_params=_cparams(("arbitrary", "arbitrary")),
        name="post_mix",
    )(x3, o_fox, o_nsa, mod3, mod3, mod3, g1, wmg, wbf, wbn, wout)


def _ffn_kernel(x_ref, sc_ref, sh_ref, gt_ref, g2_ref, wup_ref, wdn_ref, o_ref, *, fc):
    nb, tt, d = x_ref.shape
    tm = nb * tt
    x = x_ref[...]
    ms = jnp.mean(x * x, axis=-1, keepdims=True)
    h = x * lax.rsqrt(ms + RMS_EPS) * g2_ref[...] * (1.0 + sc_ref[...]) + sh_ref[...]
    h = h.reshape(tm, d).astype(BF16)
    acc = jnp.zeros((tm, d), F32)
    for c in range(D_FF // fc):
        u = jnp.maximum(_dot(h, wup_ref[:, c * fc:(c + 1) * fc]), 0.0)
        acc = acc + _dot((u * u).astype(BF16), wdn_ref[c * fc:(c + 1) * fc, :])
    o_ref[...] = x + gt_ref[...] * acc.reshape(nb, tt, d)


def _ffn_call(x3, mod3, g2, wup, wdn, nb, tt):
    NB, TT, d = x3.shape
    gi, gj = NB // nb, TT // tt
    full = lambda i, j: (0, 0)
    return pl.pallas_call(
        functools.partial(_ffn_kernel, fc=1024),
        grid=(gi, gj),
        in_specs=[pl.BlockSpec((nb, tt, d), lambda i, j: (i, j, 0)),
                  pl.BlockSpec((nb, 1, d), lambda i, j: (i, 0, 4)),
                  pl.BlockSpec((nb, 1, d), lambda i, j: (i, 0, 3)),
                  pl.BlockSpec((nb, 1, d), lambda i, j: (i, 0, 5)),
                  pl.BlockSpec((1, d), full),
                  pl.BlockSpec(wup.shape, full, pipeline_mode=pl.Buffered(1)),
                  pl.BlockSpec(wdn.shape, full, pipeline_mode=pl.Buffered(1))],
        out_specs=pl.BlockSpec((nb, tt, d), lambda i, j: (i, j, 0)),
        out_shape=jax.ShapeDtypeStruct((NB, TT, d), F32),
        compiler_params=_cparams(("arbitrary", "arbitrary")),
        name="ffn",
    )(x3, mod3, mod3, mod3, g2, wup, wdn)


def _gather_pages(pt_ref, seq, parts, n_pages):
    for p in range(n_pages):
        page = pt_ref[seq, p]
        for pool_ref, row0, buf_ref, sem_ref in parts:
            pltpu.make_async_copy(pool_ref.at[page, pl.ds(row0, buf_ref.shape[0]), :],
                                  buf_ref.at[:, p * PAGE_SIZE:(p + 1) * PAGE_SIZE], sem_ref).start()


def _gather_done(parts):
    for _, _, buf_ref, sem_ref in parts:
        pltpu.make_async_copy(buf_ref, buf_ref, sem_ref).wait()


def _two_sequence_pipeline(pt_ref, parts_a, parts_b, n_pages, compute):
    i = pl.program_id(0)
    n = pl.num_programs(0)

    @pl.when(i == 0)
    def _():
        _gather_pages(pt_ref, 0, parts_a, n_pages)

    _gather_pages(pt_ref, 2 * i + 1, parts_b, n_pages)
    _gather_done(parts_a)
    out_a = compute(0, [part[2] for part in parts_a])
    _gather_pages(pt_ref, lax.rem(2 * i + 2, 2 * n), parts_a, n_pages)
    _gather_done(parts_b)
    out_b = compute(1, [part[2] for part in parts_b])

    @pl.when(i == n - 1)
    def _():
        _gather_done(parts_a)

    return out_a, out_b


def _softmax_pv(tiles):
    m = functools.reduce(jnp.maximum, [jnp.max(s, axis=-1, keepdims=True) for s, _ in tiles])
    l = jnp.zeros_like(m)
    acc = None
    for s, pv in tiles:
        p = jnp.exp(s - m)
        l = l + jnp.sum(p, axis=-1, keepdims=True)
        contrib = pv(p.astype(BF16))
        acc = contrib if acc is None else acc + contrib
    return acc / l


def _lane_cumsum(x, n):
    lane = lax.broadcasted_iota(jnp.int32, x.shape, 1)
    s = 1
    while s < n:
        x = x + jnp.where(lane >= s, pltpu.roll(x, s, 1), 0.0)
        s *= 2
    return x


def _stack_q(q_ref, tn):
    parts = []
    for g in range(N_GROUPS):
        for r in range(N_REP):
            qc = q_ref[:, r * LANES:(r + 1) * LANES]
            parts.append(jnp.where(_half_mask(tn, g), qc, jnp.zeros_like(qc)))
    return jnp.concatenate(parts, axis=0).astype(BF16)


def _unstack_o(o, tn):
    chunks = []
    for r in range(N_REP):
        a = o[r * tn:(r + 1) * tn]
        b = o[(N_REP + r) * tn:(N_REP + r + 1) * tn]
        chunks.append(jnp.where(_half_mask(tn, 0), a, b))
    return jnp.concatenate(chunks, axis=1)


def _per_head_rows(x8, tn):
    n = x8.shape[1]
    return jnp.broadcast_to(x8[:, None, :], (N_HEADS, tn, n)).reshape(N_HEADS * tn, n)


def _slope_col(tn):
    hrow = lax.broadcasted_iota(jnp.int32, (N_HEADS * tn, 1), 0) // tn
    col = jnp.zeros((N_HEADS * tn, 1), F32)
    for h in range(N_HEADS):
        col = jnp.where(hrow == h, _slope(h), col)
    return col


def _tok_col(tn):
    return lax.rem(lax.broadcasted_iota(jnp.int32, (N_HEADS * tn, 1), 0), tn)


def _pad_keys(x, dtype):
    tn = x.shape[0]
    return jnp.concatenate([x, jnp.zeros((LANES - tn, LANES), x.dtype)], axis=0).astype(dtype)


def _past_key_tiles(qs, kv_ref, kc, bias_fn):
    tiles = []
    for c in range(kv_ref.shape[1] // kc):
        kt = kv_ref[0:LANES, c * kc:(c + 1) * kc].astype(BF16)
        vt = kv_ref[LANES:2 * LANES, c * kc:(c + 1) * kc].astype(BF16)
        tiles.append((_dot(qs, kt) + bias_fn(c), lambda p, vt=vt: _dot_nt(p, vt)))
    return tiles


def _new_key_tile(qs, new_ref, col0, bias):
    knew = _pad_keys(new_ref[:, col0:col0 + LANES], BF16)
    vnew = _pad_keys(new_ref[:, col0 + LANES:col0 + 2 * LANES], BF16)
    return (_dot_nt(qs, knew) + bias, lambda p: _dot(p, vnew))


def _fox_s_kernel(pt_ref, pool_ref, lpool_ref, q_ref, new_ref, lfn_ref, o_ref,
                  kva_ref, kvb_ref, lfa_ref, lfb_ref, sem_ref, *, n_pages, tn, kc):
    past = n_pages * PAGE_SIZE
    R = N_HEADS * tn
    tcol = _tok_col(tn)
    tl = lax.broadcasted_iota(jnp.int32, (R, LANES), 1)

    def compute(u, bufs):
        kv_ref, lf_ref = bufs
        rows = pl.ds(u * tn, tn)
        qs = _stack_q(q_ref.at[rows], tn)
        cpast = _lane_cumsum(lf_ref[...], past)
        cn = _lane_cumsum(lfn_ref[u], tn) + cpast[:, past - 1:past]
        cn_rows = _per_head_rows(cn, tn)
        cq = jnp.sum(jnp.where(tl == tcol, cn_rows, 0.0), axis=-1, keepdims=True)
        tiles = [_new_key_tile(qs, new_ref.at[rows], 0, jnp.where(tl <= tcol, cq - cn_rows, NEG))]
        tiles += _past_key_tiles(qs, kv_ref, kc,
                                 lambda c: cq - _per_head_rows(cpast[:, c * kc:(c + 1) * kc], tn))
        return _unstack_o(_softmax_pv(tiles), tn)

    parts_a = [(pool_ref, 0, kva_ref, sem_ref.at[0]), (lpool_ref, 0, lfa_ref, sem_ref.at[1])]
    parts_b = [(pool_ref, 0, kvb_ref, sem_ref.at[2]), (lpool_ref, 0, lfb_ref, sem_ref.at[3])]
    outs = _two_sequence_pipeline(pt_ref, parts_a, parts_b, n_pages, compute)
    o_ref[...] = jnp.concatenate(outs, axis=0).astype(BF16)


def _fox_s_call(page_table, fox_t, logf_t, qf, frow, lf_new, tn):
    B, n_pages = page_table.shape
    past = n_pages * PAGE_SIZE
    kc = min(past, 2048)
    assert B % 2 == 0
    grid_spec = pltpu.PrefetchScalarGridSpec(
        num_scalar_prefetch=1,
        grid=(B // 2,),
        in_specs=[pl.BlockSpec(memory_space=pl.ANY),
                  pl.BlockSpec(memory_space=pl.ANY),
                  pl.BlockSpec((2 * tn, 512), lambda b, pt: (b, 0)),
                  pl.BlockSpec((2 * tn, 256), lambda b, pt: (b, 0)),
                  pl.BlockSpec((2, N_HEADS, LANES), lambda b, pt: (b, 0, 0))],
        out_specs=pl.BlockSpec((2 * tn, 512), lambda b, pt: (b, 0)),
        scratch_shapes=[pltpu.VMEM((2 * LANES, past), F32), pltpu.VMEM((2 * LANES, past), F32),
                        pltpu.VMEM((N_HEADS, past), F32), pltpu.VMEM((N_HEADS, past), F32),
                        pltpu.SemaphoreType.DMA((4,))],
    )
    return pl.pallas_call(
        functools.partial(_fox_s_kernel, n_pages=n_pages, tn=tn, kc=kc),
        grid_spec=grid_spec,
        out_shape=jax.ShapeDtypeStruct((B * tn, 512), BF16),
        compiler_params=_cparams(("arbitrary",)),
        name="fox_sample",
    )(page_table, fox_t, logf_t, qf, frow, lf_new)


def _cmp_s_kernel(pt_ref, pool_ref, q_ref, w1k_ref, w1v_ref, posk_ref, posv_ref, w2k_ref, w2v_ref,
                  bd_ref, gk_ref, a_ref, oc_ref, imp_ref, kva_ref, kvb_ref, xk_ref, xv_ref, sem_ref,
                  *, n_pages, tn):
    past = n_pages * PAGE_SIZE
    nc = past // CMP_STRIDE
    R = N_HEADS * tn
    n_split = 4 if n_pages % 4 == 0 else 1
    pps = n_pages // n_split
    cps = pps * PAGE_SIZE // CMP_STRIDE

    def compute(u, bufs):
        (kv_ref,) = bufs
        acc_k, acc_v = [], []
        for sp in range(n_split):
            for pg in range(sp * pps, (sp + 1) * pps):
                tok = slice(pg * PAGE_SIZE, (pg + 1) * PAGE_SIZE)
                xk_ref[tok, :] = kv_ref[0:LANES, tok].T
                xv_ref[tok, :] = kv_ref[LANES:2 * LANES, tok].T
            acc_k.append(_chunk_dot(xk_ref, sp * cps, cps, w1k_ref))
            acc_v.append(_chunk_dot(xv_ref, sp * cps, cps, w1v_ref))
        kc, vc = _compress_finish(jnp.concatenate(acc_k, axis=0), jnp.concatenate(acc_v, axis=0), nc,
                                  w1k_ref, w1v_ref, posk_ref, posv_ref, w2k_ref, w2v_ref,
                                  bd_ref[...], gk_ref[...])
        qs = _stack_q(q_ref.at[pl.ds(u * tn, tn)], tn)
        nid = lax.broadcasted_iota(jnp.int32, (R, nc), 1)
        tpos = past + _tok_col(tn)
        dc = tpos - (nid * CMP_STRIDE + (2 * CMP_STRIDE - 1))
        valid = (dc >= 0) & (nid < nc - 1)
        s = jnp.where(valid, _dot_nt(qs, kc.astype(BF16)) - _slope_col(tn) * dc.astype(F32), NEG)
        mx = jnp.max(s, axis=-1, keepdims=True)
        p = jnp.where(valid, jnp.exp(s - mx), 0.0)
        den = jnp.sum(p, axis=-1, keepdims=True)
        p = p / jnp.where(den > 0, den, 1.0)
        oc_ref[u] = _dot(p.astype(BF16), vc.astype(BF16))
        p4 = p.reshape(N_GROUPS, N_REP, tn, nc)
        psum = (p4[:, 0] + p4[:, 1] + p4[:, 2] + p4[:, 3]).reshape(N_GROUPS * tn, nc)
        imp_ref[u] = _dot_exact_r(psum, a_ref[...])

    _two_sequence_pipeline(pt_ref, [(pool_ref, 0, kva_ref, sem_ref.at[0])],
                           [(pool_ref, 0, kvb_ref, sem_ref.at[1])], n_pages, compute)


def _cmp_s_call(page_table, nsa_pool, qn, cw, a_mat, tn):
    B, n_pages = page_table.shape
    past = n_pages * PAGE_SIZE
    consts = [cw["w1k"], cw["w1v"], cw["posk"], cw["posv"], cw["w2k"], cw["w2v"], cw["bd"], cw["gk"], a_mat]
    cspecs = [pl.BlockSpec(a.shape, (lambda b, pt, nd=a.ndim: (0,) * nd)) for a in consts]
    nsl = a_mat.shape[1]
    grid_spec = pltpu.PrefetchScalarGridSpec(
        num_scalar_prefetch=1,
        grid=(B // 2,),
        in_specs=[pl.BlockSpec(memory_space=pl.ANY),
                  pl.BlockSpec((2 * tn, 512), lambda b, pt: (b, 0))] + cspecs,
        out_specs=[pl.BlockSpec((2, N_HEADS * tn, LANES), lambda b, pt: (b, 0, 0)),
                   pl.BlockSpec((2, N_GROUPS * tn, nsl), lambda b, pt: (b, 0, 0))],
        scratch_shapes=[pltpu.VMEM((2 * LANES, past), F32), pltpu.VMEM((2 * LANES, past), F32),
                        pltpu.VMEM((past, LANES), F32), pltpu.VMEM((past, LANES), F32),
                        pltpu.SemaphoreType.DMA((2,))],
    )
    return pl.pallas_call(
        functools.partial(_cmp_s_kernel, n_pages=n_pages, tn=tn),
        grid_spec=grid_spec,
        out_shape=[jax.ShapeDtypeStruct((B, N_HEADS * tn, LANES), F32),
                   jax.ShapeDtypeStruct((B, N_GROUPS * tn, nsl), F32)],
        compiler_params=_cparams(("arbitrary",)),
        name="compress_sample",
    )(page_table, nsa_pool, qn, *consts)


def _topk_s_kernel(imp_ref, o_ref, *, tn, past):
    rows = imp_ref.shape[0]
    t = past + lax.rem(lax.broadcasted_iota(jnp.int32, (rows, 1), 0), tn)
    sel = _select_blocks(imp_ref[...], t // SEL_BLOCK)
    o_ref[...] = ((sel - 1.0) * MASK_BIG).astype(BF16)


def _topk_s_call(imp2, tn, past):
    rows, nsl = imp2.shape
    tr = min(rows, 256)
    return pl.pallas_call(
        functools.partial(_topk_s_kernel, tn=tn, past=past),
        grid=(rows // tr,),
        in_specs=[pl.BlockSpec((tr, nsl), lambda i: (i, 0))],
        out_specs=pl.BlockSpec((tr, nsl), lambda i: (i, 0)),
        out_shape=jax.ShapeDtypeStruct((rows, nsl), BF16),
        compiler_params=_cparams(("arbitrary",)),
        name="select_sample",
    )(imp2)


def _sel_s_kernel(pt_ref, pool_ref, q_ref, nnew_ref, wnew_ref, win_ref, seln_ref, oc_ref, gate_ref, e_ref,
                  o_ref, kva_ref, kvb_ref, sem_ref, *, n_pages, tn, kc):
    past = n_pages * PAGE_SIZE
    R = N_HEADS * tn
    n_chunks = past // kc
    wlen = win_ref.shape[2]
    tcol = _tok_col(tn)
    slope = _slope_col(tn)
    tl = lax.broadcasted_iota(jnp.int32, (R, LANES), 1)
    newbias = jnp.where(tl <= tcol, -slope * (tcol - tl).astype(F32), NEG)

    def group_rows(x):
        n = x.shape[1]
        x4 = jnp.broadcast_to(x.reshape(N_GROUPS, 1, tn, n), (N_GROUPS, N_REP, tn, n))
        return x4.reshape(R, n)

    def compute(u, bufs):
        (kv_ref,) = bufs
        rows = pl.ds(u * tn, tn)
        qs = _stack_q(q_ref.at[rows], tn)

        seln_new = seln_ref[u, n_chunks]
        mb_new = group_rows(_dot(seln_new, e_ref[...])[:, 0:LANES])

        def past_bias(c):
            mb = group_rows(_dot(seln_ref[u, c], e_ref[...]))
            kpos = c * kc + lax.broadcasted_iota(jnp.int32, (1, kc), 1)
            return mb - slope * ((past + tcol) - kpos).astype(F32)

        tiles = [_new_key_tile(qs, nnew_ref.at[rows], 2 * LANES, jnp.where(tl <= tcol, newbias + mb_new, NEG))]
        o_s = _softmax_pv(tiles + _past_key_tiles(qs, kv_ref, kc, past_bias))

        kt = win_ref[u, 0:LANES, :].astype(BF16)
        vt = win_ref[u, LANES:2 * LANES, :].astype(BF16)
        wpos = (past - wlen) + lax.broadcasted_iota(jnp.int32, (1, wlen), 1)
        dw = (past + tcol) - wpos
        bias = jnp.where((dw < WINDOW) & (wpos >= 0), -slope * dw.astype(F32), NEG)
        o_w = _softmax_pv([_new_key_tile(qs, wnew_ref.at[rows], 0, newbias),
                           (_dot(qs, kt) + bias, lambda p: _dot_nt(p, vt))])

        gc, gs, gw = _gate_cols(gate_ref.at[rows], tn)
        return _unstack_o(gc * oc_ref[u] + gs * o_s + gw * o_w, tn)

    outs = _two_sequence_pipeline(pt_ref, [(pool_ref, 2 * LANES, kva_ref, sem_ref.at[0])],
                                  [(pool_ref, 2 * LANES, kvb_ref, sem_ref.at[1])], n_pages, compute)
    o_ref[...] = jnp.concatenate(outs, axis=0).astype(BF16)


def _sel_s_call(page_table, nsa_pool, qn, nrow, wrow, win_t, seln, oc, gates, e_mat, tn, kc):
    B, n_pages = page_table.shape
    past = n_pages * PAGE_SIZE
    wlen = win_t.shape[2]
    grid_spec = pltpu.PrefetchScalarGridSpec(
        num_scalar_prefetch=1,
        grid=(B // 2,),
        in_specs=[pl.BlockSpec(memory_space=pl.ANY),
                  pl.BlockSpec((2 * tn, 512), lambda b, pt: (b, 0)),
                  pl.BlockSpec((2 * tn, 512), lambda b, pt: (b, 0)),
                  pl.BlockSpec((2 * tn, 256), lambda b, pt: (b, 0)),
                  pl.BlockSpec((2, 2 * LANES, wlen), lambda b, pt: (b, 0, 0)),
                  pl.BlockSpec((2,) + seln.shape[1:], lambda b, pt: (b, 0, 0, 0)),
                  pl.BlockSpec((2, N_HEADS * tn, LANES), lambda b, pt: (b, 0, 0)),
                  pl.BlockSpec((2 * tn, LANES), lambda b, pt: (b, 0)),
                  pl.BlockSpec(e_mat.shape, lambda b, pt: (0, 0))],
        out_specs=pl.BlockSpec((2 * tn, 512), lambda b, pt: (b, 0)),
        scratch_shapes=[pltpu.VMEM((2 * LANES, past), F32), pltpu.VMEM((2 * LANES, past), F32),
                        pltpu.SemaphoreType.DMA((2,))],
    )
    return pl.pallas_call(
        functools.partial(_sel_s_kernel, n_pages=n_pages, tn=tn, kc=kc),
        grid_spec=grid_spec,
        out_shape=jax.ShapeDtypeStruct((B * tn, 512), BF16),
        compiler_params=_cparams(("arbitrary",)),
        name="select_attend_sample",
    )(page_table, nsa_pool, qn, nrow, wrow, win_t, seln, oc, gates, e_mat)


def _pair_cols(base):
    idx = []
    for r in range(N_REP):
        for g in range(N_GROUPS):
            h = g * N_REP + r
            idx.extend(range(base + h * HEAD_DIM, base + (h + 1) * HEAD_DIM))
    return np.asarray(idx, np.int32)


def _prep_weights(w_in, b_fox_f, fox_qn_g, fox_kn_g, nsa_qn_g, nsa_kn_slc_g, nsa_kn_win_g):
    o_fq, o_fk, o_fv, o_ff, o_nq, o_nkv, o_ng, o_mg = 0, 512, 640, 768, 776, 1288, 2056, 2080
    cols = np.concatenate([
        _pair_cols(o_fq), np.arange(o_fk, o_fk + 128), np.arange(o_fv, o_fv + 128),
        _pair_cols(o_nq), np.arange(o_nkv, o_nkv + 768),
        np.arange(o_ff, o_ff + 8), np.arange(o_ng, o_ng + 24)]).astype(np.int32)
    w_p = jnp.take(w_in, cols, axis=1)
    w_p = jnp.pad(w_p, ((0, 0), (0, _C_END - w_p.shape[1]))).astype(BF16)
    w_mg = w_in[:, o_mg:o_mg + 2 * D_MODEL].astype(BF16)
    tile2 = lambda g: jnp.tile(g, 2)
    gains = jnp.stack([tile2(fox_qn_g), tile2(fox_kn_g), tile2(nsa_qn_g), tile2(nsa_kn_slc_g),
                       tile2(nsa_kn_win_g)] + [jnp.zeros((LANES,), F32)] * 3)
    bff = jnp.zeros((1, LANES), F32).at[0, LOGF_LANE0:LOGF_LANE0 + N_HEADS].set(b_fox_f)
    return w_p, w_mg, gains, bff


def _prep_compress(pos, w1, w2):
    w1r = w1.reshape(2, CMP_STRIDE, HEAD_DIM, CMP_HIDDEN)
    z = jnp.zeros((CMP_STRIDE, HEAD_DIM, CMP_HIDDEN), F32)
    top = jnp.concatenate([w1r[0], z, w1r[1], z], axis=-1)
    bot = jnp.concatenate([z, w1r[0], z, w1r[1]], axis=-1)
    w1b = jnp.concatenate([top, bot], axis=1).astype(BF16)
    w1b = w1b.reshape(CMP_STRIDE * LANES, 4 * LANES)
    pr = pos.reshape(2, CMP_STRIDE, HEAD_DIM)
    posb = jnp.concatenate([jnp.tile(pr, (1, 1, 2)).transpose(1, 0, 2),
                            jnp.zeros((CMP_STRIDE, 6, LANES), F32)], axis=1).astype(BF16)
    posb = posb.transpose(1, 0, 2).reshape(8, CMP_STRIDE * LANES)
    zz = jnp.zeros((CMP_HIDDEN, HEAD_DIM), F32)
    w2b = jnp.concatenate([jnp.concatenate([w2, zz], axis=1),
                           jnp.concatenate([zz, w2], axis=1)], axis=0).astype(BF16)
    return w1b, posb, w2b


def kernel(x_prompt, x_sample, cache_fox_kv, cache_fox_logf, cache_nsa_kv, state_win_kv, page_table,
           c_prompt, c_sample, norm1_g, norm2_g, w_ada, b_ada, w_in, b_fox_f, fox_qn_g, fox_kn_g,
           nsa_qn_g, nsa_kn_cmp_g, nsa_kn_slc_g, nsa_kn_win_g, cmp_pos_k, cmp_w1_k, cmp_w2_k,
           cmp_pos_v, cmp_w1_v, cmp_w2_v, w_br_fox, w_br_nsa, w_out, w_up, w_down):
    assert norm1_g.shape[0] == 1
    B, T, d = x_prompt.shape
    DB, TN, _ = x_sample.shape
    n_phys = cache_fox_kv.shape[1]
    n_pages = page_table.shape[1]
    past = n_pages * PAGE_SIZE
    wlen = state_win_kv.shape[2]
    assert T % 256 == 0 and T >= WINDOW and wlen == WINDOW and TN == 8 and past % CMP_STRIDE == 0

    w_p, w_mg, gains, bff = _prep_weights(w_in[0], b_fox_f[0], fox_qn_g[0], fox_kn_g[0], nsa_qn_g[0],
                                          nsa_kn_slc_g[0], nsa_kn_win_g[0])
    w1k, posk, w2k = _prep_compress(cmp_pos_k[0], cmp_w1_k[0], cmp_w2_k[0])
    w1v, posv, w2v = _prep_compress(cmp_pos_v[0], cmp_w1_v[0], cmp_w2_v[0])
    bd = jnp.asarray(np.kron(np.eye(2), np.ones((HEAD_DIM, HEAD_DIM))), BF16)
    cw = dict(w1k=w1k, w1v=w1v, posk=posk, posv=posv, w2k=w2k, w2v=w2v, bd=bd,
              gk=jnp.tile(nsa_kn_cmp_g[0], 2).reshape(1, LANES))
    pair_rows = _pair_cols(0)
    wbf = jnp.take(w_br_fox[0], pair_rows, axis=0).astype(BF16)
    wbn = jnp.take(w_br_nsa[0], pair_rows, axis=0).astype(BF16)
    wout = w_out[0].astype(BF16)
    wup = w_up[0].astype(BF16)
    wdn = w_down[0].astype(BF16)
    g1 = norm1_g[0].reshape(1, d)
    g2 = norm2_g[0].reshape(1, d)
    tm_p = 512
    tri = jnp.asarray(np.tril(np.ones((tm_p, tm_p), np.float32)), BF16)

    mod = _ada_call(jnp.concatenate([c_prompt, c_sample], axis=0), w_ada[0], b_ada[0])
    mod_p = mod[:B].reshape(B, 1, 6 * d)
    mod_s = mod[B:].reshape(DB, 1, 6 * d)

    (qf, frow_t, qn, nrow_t, wrow_t, gates, lf, c_tm, kf, vf, ks, vs, kw, vw, nraw) = _pre_call(
        x_prompt, mod_p, g1, w_p, bd, tri, gains, bff, nb=1, tt=tm_p, do_cum=True)
    tq, tk, tw = 128, 512, 256
    o_fox = _fox_p_call(qf, kf, vf, c_tm, B, T, tq, tk)
    kc_p, vc_p = _cmp_p_call(nraw, cw, B, T)
    nc_p = T // CMP_STRIDE
    nsel_p = -(-T // SEL_BLOCK)
    assert nsel_p <= LANES - _AUG_SEL0 and T % tk == 0
    a_p = jnp.asarray(_importance_matrix(nc_p, LANES, nc_p - 1)[:, :LANES] *
                      (np.arange(LANES) < nsel_p)[None, :], BF16)
    o_nsa = _nsa_p_call(qn, ks, vs, kw, vw, kc_p, vc_p, gates, a_p, B, T, tq, tk, tw)
    x1 = _mix_call(x_prompt, o_fox, o_nsa, mod_p, g1, w_mg, wbf, wbn, wout, nb=1, tt=tm_p)
    y_prompt = _ffn_call(x1, mod_p, g2, wup, wdn, nb=1, tt=tm_p)

    nb_s = min(DB, 32)
    (qf_s, frow_s, qn_s, nrow_s, wrow_s, gates_s, lf_s) = _pre_call(
        x_sample, mod_s, g1, w_p, bd, tri, gains, bff, nb=nb_s, tt=TN, do_cum=False)
    qf_s = qf_s.astype(F32)
    qn_s = qn_s.astype(F32)

    fox_t = jnp.transpose(cache_fox_kv[0], (0, 2, 3, 4, 1)).reshape(n_phys, 2 * LANES, PAGE_SIZE)
    nsa_t = jnp.transpose(cache_nsa_kv[0], (0, 2, 3, 4, 1)).reshape(n_phys, 4 * LANES, PAGE_SIZE)
    logf_t = jnp.transpose(cache_fox_logf[0], (0, 2, 1))
    win_t = jnp.transpose(state_win_kv[0], (0, 2, 3, 4, 1)).reshape(DB, 2 * LANES, wlen)

    lf_new = lf_s[:, LOGF_LANE0:LOGF_LANE0 + N_HEADS].reshape(DB, TN, N_HEADS).transpose(0, 2, 1)
    lf_new = jnp.pad(lf_new, ((0, 0), (0, 0), (0, LANES - TN)))
    o_fox_s = _fox_s_call(page_table, fox_t, logf_t, qf_s, frow_s, lf_new, TN)

    nsa_pool = nsa_t
    nc_s = past // CMP_STRIDE
    nsel_s = -(-(past + TN) // SEL_BLOCK)
    nsl = -(-nsel_s // LANES) * LANES
    a_s = jnp.asarray(_importance_matrix(nc_s, nsl, nc_s - 1) * (np.arange(nsl) < nsel_s)[None, :], BF16)
    oc_s, imp_s = _cmp_s_call(page_table, nsa_pool, qn_s, cw, a_s, TN)
    seln = _topk_s_call(imp_s.reshape(DB * N_GROUPS * TN, nsl), TN, past)
    kc_keys = min(past, 2048)
    bpc = kc_keys // SEL_BLOCK
    seln = seln.reshape(DB, N_GROUPS * TN, nsl // bpc, bpc).transpose(0, 2, 1, 3)
    e_s = jnp.asarray((np.arange(kc_keys)[None, :] // SEL_BLOCK) == np.arange(bpc)[:, None], BF16)
    o_nsa_s = _sel_s_call(page_table, nsa_pool, qn_s, nrow_s, wrow_s, win_t, seln, oc_s, gates_s, e_s,
                          TN, kc_keys)
    x1_s = _mix_call(x_sample, o_fox_s, o_nsa_s, mod_s, g1, w_mg, wbf, wbn, wout, nb=nb_s, tt=TN)
    y_sample = _ffn_call(x1_s, mod_s, g2, wup, wdn, nb=nb_s, tt=TN)

    def token_major(rows_t, n_slots):
        toks = rows_t.shape[2]
        return rows_t.reshape(B, n_slots, N_GROUPS, HEAD_DIM, toks).transpose(0, 4, 1, 2, 3)

    lf_p = lf[:, LOGF_LANE0:LOGF_LANE0 + N_HEADS]
    win_s = jnp.concatenate([state_win_kv[0, :, TN:],
                             wrow_s.reshape(DB, TN, 2, N_GROUPS, HEAD_DIM)], axis=1)
    return (y_prompt, y_sample,
            token_major(frow_t, 2)[None],
            lf_p.reshape(1, B, T, N_HEADS),
            token_major(nrow_t, 4)[None],
            token_major(wrow_t[:, :, T - wlen:], 2)[None],
            frow_s.reshape(1, DB, TN, 2, N_GROUPS, HEAD_DIM),
            lf_s[:, LOGF_LANE0:LOGF_LANE0 + N_HEADS].reshape(1, DB, TN, N_HEADS),
            nrow_s.reshape(1, DB, TN, 4, N_GROUPS, HEAD_DIM),
            win_s[None])
```

```python
import functools

import numpy as np
import jax
import jax.numpy as jnp
from jax import lax
from jax.experimental import pallas as pl
from jax.experimental.pallas import tpu as pltpu

F32 = jnp.float32
BF16 = jnp.bfloat16

D_MODEL = 1024
HEAD_DIM = 64
N_HEADS = 8
N_GROUPS = 2
N_REP = N_HEADS // N_GROUPS
PAGE_SIZE = 128
CMP_STRIDE = 16
CMP_HIDDEN = 2 * HEAD_DIM
SEL_BLOCK = 64
N_SELECT = 16
WINDOW = 512
D_FF = 4 * D_MODEL
RMS_EPS = 1e-6
FORCE_BONUS = 1.0e4
LANES = 128
NEG = -1e30
MASK_BIG = 2.0 ** 100
LOGF_LANE0 = 0
GATE_LANE0 = 8
VMEM_LIMIT = 56 * 1024 * 1024


def _dot(a, b):
    return jnp.dot(a, b, preferred_element_type=F32)


def _dot_nt(a, b):
    return lax.dot_general(a, b, (((1,), (1,)), ((), ())), preferred_element_type=F32)


def _split3(x):
    x1 = x.astype(BF16)
    r = x - x1.astype(F32)
    x2 = r.astype(BF16)
    x3 = (r - x2.astype(F32)).astype(BF16)
    return x1, x2, x3


def _dot_exact_r(x, m):
    a, b, c = _split3(x)
    return _dot(a, m) + _dot(b, m) + _dot(c, m)


def _dot_exact_l(m, x):
    a, b, c = _split3(x)
    return _dot(m, a) + _dot(m, b) + _dot(m, c)


def _sigmoid(x):
    return 1.0 / (1.0 + jnp.exp(-x))


def _head_rms(zc, bd):
    a = zc * zc
    a1 = a.astype(BF16)
    a2 = (a - a1.astype(F32)).astype(BF16)
    ss = _dot(a1, bd) + _dot(a2, bd)
    return zc * lax.rsqrt(ss * (1.0 / HEAD_DIM) + RMS_EPS)


def _cparams(sem, vmem=VMEM_LIMIT):
    return pltpu.CompilerParams(dimension_semantics=sem, vmem_limit_bytes=vmem)


def _ada_kernel(c_ref, w_ref, b_ref, o_ref):
    c = c_ref[...]
    a = c * _sigmoid(c)
    o_ref[...] = _dot(a.astype(BF16), w_ref[...].astype(BF16)) + b_ref[...]


def _ada_call(c, w_ada, b_ada):
    r, d = c.shape
    n = w_ada.shape[1]
    tn = 1024
    return pl.pallas_call(
        _ada_kernel,
        grid=(n // tn,),
        in_specs=[pl.BlockSpec((r, d), lambda j: (0, 0)),
                  pl.BlockSpec((d, tn), lambda j: (0, j)),
                  pl.BlockSpec((1, tn), lambda j: (0, j))],
        out_specs=pl.BlockSpec((r, tn), lambda j: (0, j)),
        out_shape=jax.ShapeDtypeStruct((r, n), F32),
        compiler_params=_cparams(("arbitrary",)),
        name="ada_mod",
    )(c, w_ada, b_ada.reshape(1, n))


_C_FQ, _C_FK, _C_FV, _C_NQ, _C_NKV, _C_SMALL, _C_END = 0, 512, 640, 768, 1280, 2048, 2176


def _v_with_ones(v):
    lane = lax.broadcasted_iota(jnp.int32, v.shape, 1)
    return jnp.concatenate([jnp.where(lane < HEAD_DIM, v, 1.0), jnp.where(lane < HEAD_DIM, 1.0, v)],
                           axis=1).astype(BF16)


def _pos_hi_lo(pos):
    return (pos >> 8).astype(F32), (pos & 255).astype(F32)


_AUG_SEL0 = 4


def _key_pos_features(pos, lane):
    hi, lo = _pos_hi_lo(pos)
    return jnp.where(lane < 2, 1.0,
                     jnp.where(lane == 2, hi,
                               jnp.where(lane == 3, lo,
                                         jnp.where(lane - _AUG_SEL0 == pos // SEL_BLOCK, 1.0, 0.0))))


def _query_pos_features(pos, slope, lane):
    hi, lo = _pos_hi_lo(pos)
    return jnp.where(lane == 0, -(slope * 256.0) * hi,
                     jnp.where(lane == 1, -slope * lo,
                               jnp.where(lane == 2, slope * 256.0,
                                         jnp.where(lane == 3, slope, 0.0))))


def _pre_kernel(x_ref, sc_ref, sh_ref, g1_ref, w_ref, bd_ref, tri_ref, gains_ref, bff_ref,
                qf_ref, frow_ref, qn_ref, nrow_ref, wrow_ref, gate_ref, lf_ref,
                *rest, do_cum):
    nb, tt, d = x_ref.shape
    tm = nb * tt
    x = x_ref[...]
    ms = jnp.mean(x * x, axis=-1, keepdims=True)
    h = x * lax.rsqrt(ms + RMS_EPS) * g1_ref[...] * (1.0 + sc_ref[...]) + sh_ref[...]
    h = h.reshape(tm, d).astype(BF16)
    bd2 = bd_ref[...]

    def proj(c0, n_chunks=1):
        z = _dot(h, w_ref[:, c0:c0 + n_chunks * LANES])
        return [z[:, i * LANES:(i + 1) * LANES] for i in range(n_chunks)]

    def rms_pair(za, zb):
        n = _head_rms(jnp.concatenate([za, zb], axis=1), bd2)
        return n[:, 0:LANES], n[:, LANES:2 * LANES]

    scale = HEAD_DIM ** -0.5
    for q_ref, c0, grow in ((qf_ref, _C_FQ, 0), (qn_ref, _C_NQ, 2)):
        z = proj(c0, N_REP)
        for r, zc in enumerate(rms_pair(z[0], z[1]) + rms_pair(z[2], z[3])):
            q_ref[:, r * LANES:(r + 1) * LANES] = (zc * (gains_ref[grow:grow + 1, :] * scale)).astype(BF16)

    def put(ref, i, val):
        if do_cum:
            ref[0, i * LANES:(i + 1) * LANES, :] = val.T
        else:
            ref[:, i * LANES:(i + 1) * LANES] = val

    fk, fv = proj(_C_FK, 2)
    ck, cv = proj(_C_NKV, 2)
    sk, sv = proj(_C_NKV + 2 * LANES, 2)
    wk, wv = proj(_C_NKV + 4 * LANES, 2)
    fk, sk = rms_pair(fk, sk)
    fk = fk * gains_ref[1:2, :]
    sk = sk * gains_ref[3:4, :]
    wk = _head_rms(wk, bd2[0:LANES, 0:LANES]) * gains_ref[4:5, :]
    put(frow_ref, 0, fk)
    put(frow_ref, 1, fv)
    for i, val in enumerate((ck, cv, sk, sv)):
        put(nrow_ref, i, val)
    put(wrow_ref, 0, wk)
    put(wrow_ref, 1, wv)

    (zl,) = proj(_C_SMALL)
    gate_ref[...] = _sigmoid(zl)
    xl = zl + bff_ref[...]
    lf = jnp.minimum(xl, 0.0) - jnp.log1p(jnp.exp(-jnp.abs(xl)))
    lf_ref[...] = lf

    if do_cum:
        c_ref, kf_ref, vf_ref, ks_ref, vs_ref, kw_ref, vw_ref, nraw_ref, carry_ref = rest
        nraw_ref[:, 0:LANES] = ck
        nraw_ref[:, LANES:2 * LANES] = cv

        @pl.when(pl.program_id(1) == 0)
        def _():
            carry_ref[...] = jnp.zeros_like(carry_ref)

        c = _dot_exact_l(tri_ref[...], lf) + carry_ref[0:1, :]
        c_ref[...] = c
        carry_ref[...] = jnp.broadcast_to(c[tm - 1:tm, :], carry_ref.shape)

        lane = lax.broadcasted_iota(jnp.int32, (tm, LANES), 1)
        c1, c2, c3 = (p.astype(F32) for p in _split3(c))
        aug_f = jnp.where(lane < N_HEADS, -c1,
                          jnp.where(lane < 2 * N_HEADS, -pltpu.roll(c2, N_HEADS, 1),
                                    jnp.where(lane < 3 * N_HEADS, -pltpu.roll(c3, 2 * N_HEADS, 1),
                                              jnp.where(lane < 3 * N_HEADS + 3, 1.0, 0.0))))
        kf_ref[:, 0:LANES] = fk.astype(BF16)
        kf_ref[:, LANES:2 * LANES] = aug_f.astype(BF16)
        vf_ref[...] = _v_with_ones(fv)

        pos = pl.program_id(1) * tm + lax.broadcasted_iota(jnp.int32, (tm, 1), 0)
        aug_p = _key_pos_features(pos, lane).astype(BF16)
        ks_ref[:, 0:LANES] = sk.astype(BF16)
        ks_ref[:, LANES:2 * LANES] = aug_p
        vs_ref[...] = _v_with_ones(sv)
        kw_ref[:, 0:LANES] = wk.astype(BF16)
        kw_ref[:, LANES:2 * LANES] = aug_p
        vw_ref[...] = _v_with_ones(wv)


def _pre_call(x3, mod3, g1, w_p, bd, tri, gains, bff, nb, tt, do_cum):
    NB, TT, d = x3.shape
    tm = nb * tt
    n = NB * TT
    gi, gj = NB // nb, TT // tt
    tok = lambda i, j: (i * gj + j, 0)
    full = lambda i, j: (0, 0)
    in_specs = [
        pl.BlockSpec((nb, tt, d), lambda i, j: (i, j, 0)),
        pl.BlockSpec((nb, 1, d), lambda i, j: (i, 0, 1)),
        pl.BlockSpec((nb, 1, d), lambda i, j: (i, 0, 0)),
        pl.BlockSpec((1, d), full),
        pl.BlockSpec(w_p.shape, full),
        pl.BlockSpec(bd.shape, full),
        pl.BlockSpec(tri.shape, full),
        pl.BlockSpec(gains.shape, full),
        pl.BlockSpec(bff.shape, full),
    ]
    def rows_out(width):
        if do_cum:
            return (jax.ShapeDtypeStruct((NB, width, TT), F32),
                    pl.BlockSpec((1, width, tm), lambda i, j: (i, 0, j)))
        return jax.ShapeDtypeStruct((n, width), F32), pl.BlockSpec((tm, width), tok)

    def tok_out(width, dtype):
        return jax.ShapeDtypeStruct((n, width), dtype), pl.BlockSpec((tm, width), tok)

    outs = [
        tok_out(512, BF16),
        rows_out(256),
        tok_out(512, BF16),
        rows_out(512),
        rows_out(256),
        tok_out(LANES, F32),
        tok_out(LANES, F32),
    ]
    scratch = []
    if do_cum:
        assert nb == 1
        outs += [tok_out(LANES, F32)]
        outs += [tok_out(256, BF16)] * 6
        outs += [tok_out(256, F32)]
        scratch = [pltpu.VMEM((8, LANES), F32)]
    out_shape = [o[0] for o in outs]
    out_specs = [o[1] for o in outs]
    return pl.pallas_call(
        functools.partial(_pre_kernel, do_cum=do_cum),
        grid=(gi, gj),
        in_specs=in_specs,
        out_specs=out_specs,
        out_shape=out_shape,
        scratch_shapes=scratch,
        compiler_params=_cparams(("arbitrary", "arbitrary")),
        name="pre_mixer",
    )(x3, mod3, mod3, g1, w_p, bd, tri, gains, bff)


def _half_mask(rows, g):
    lane = lax.broadcasted_iota(jnp.int32, (rows, LANES), 1)
    return (lane < HEAD_DIM) if g == 0 else (lane >= HEAD_DIM)


def _slope(h):
    return 2.0 ** (-8.0 * (h + 1) / N_HEADS)


def _select_blocks(imp, cur):
    blk = lax.broadcasted_iota(jnp.int32, imp.shape, 1)
    forced = (blk == 0) | (blk == cur) | (blk == cur - 1)
    score = jnp.where(blk <= cur, imp + jnp.where(forced, FORCE_BONUS, 0.0), -jnp.inf)
    blkf = blk.astype(F32)

    def body(_, carry):
        sc, sel = carry
        mx = jnp.max(sc, axis=-1, keepdims=True)
        idx = jnp.min(jnp.where(sc == mx, blkf, 1e9), axis=-1, keepdims=True)
        pick = blkf == idx
        return jnp.where(pick, -jnp.inf, sc), jnp.where(pick, 1.0, sel)

    _, sel = lax.fori_loop(0, N_SELECT, body, (score, jnp.zeros(imp.shape, F32)))
    return jnp.where(blk <= cur, sel, 0.0)


def _select_blocks_by_rank(imp, cur, n_blocks):
    blk = lax.broadcasted_iota(jnp.int32, imp.shape, 1)
    forced = (blk == 0) | (blk == cur) | (blk == cur - 1)
    score = jnp.where(blk <= cur, imp + jnp.where(forced, FORCE_BONUS, 0.0), -jnp.inf)
    beaten_by = jnp.zeros(imp.shape, F32)
    for i in range(n_blocks):
        si = score[:, i:i + 1]
        beaten_by = beaten_by + jnp.where((si > score) | ((si == score) & (blk > i)), 1.0, 0.0)
    return jnp.where((beaten_by < N_SELECT) & (blk <= cur), 1.0, 0.0)


def _importance_matrix(n_cmp_rows, n_sel_cols, n_cmp):
    r = SEL_BLOCK // CMP_STRIDE
    a = np.zeros((n_cmp_rows, n_sel_cols), np.float32)
    for n in range(n_cmp):
        for j in range(n_sel_cols):
            off = n - r * j
            if off in (-1, r - 1):
                a[n, j] = 0.5
            elif 0 <= off <= r - 2:
                a[n, j] = 1.0
    return a


def _stack_q_features(q_ref, qa_ref, tq, feat_fn):
    for g in range(N_GROUPS):
        for r in range(N_REP):
            h = g * N_REP + r
            qc = q_ref[:, r * LANES:(r + 1) * LANES]
            qa_ref[h * tq:(h + 1) * tq, 0:LANES] = jnp.where(_half_mask(tq, g), qc, jnp.zeros_like(qc))
            qa_ref[h * tq:(h + 1) * tq, LANES:2 * LANES] = feat_fn(h).astype(BF16)


def _stacked_attend(qa, k_ref, v_ref, tk, first, lo, mask_first, mask_rest):
    half = qa.shape[0] // 2

    def probabilities(j, mask, m_old):
        st = pl.multiple_of(j * tk, tk)
        s = mask(_dot_nt(qa, k_ref[pl.ds(st, tk), :]))
        m_new = jnp.max(s, axis=-1, keepdims=True)
        if m_old is not None:
            m_new = jnp.maximum(m_old, m_new)
        p = jnp.exp(s - m_new).astype(BF16)
        va = v_ref[pl.ds(st, tk), :]
        pv = jnp.concatenate([_dot(p[:half], va[:, 0:LANES]), _dot(p[half:], va[:, LANES:2 * LANES])], axis=0)
        return m_new, pv

    def body(j, carry):
        m_old, acc = carry
        m_new, pv = probabilities(j, lambda s: mask_rest(j, s), m_old)
        return m_new, jnp.exp(m_old - m_new) * acc + pv

    m, acc = lax.fori_loop(lo, first, body, probabilities(first, mask_first, None))
    return acc / pltpu.roll(acc, HEAD_DIM, 1)


def _row_pos(t0, tq):
    return t0 + lax.rem(lax.broadcasted_iota(jnp.int32, (N_HEADS * tq, 1), 0), tq)


def _fox_p_kernel(q_ref, ka_ref, va_ref, c_ref, o_ref, qa_ref, *, tq, tk):
    qi = pl.program_id(1)
    t0 = qi * tq
    R = N_HEADS * tq
    lane = lax.broadcasted_iota(jnp.int32, (tq, LANES), 1)
    c1, c2, c3 = (p.astype(F32) for p in _split3(c_ref[...]))

    def q_features(h):
        own = (lane == h) | (lane == N_HEADS + h) | (lane == 2 * N_HEADS + h)
        return jnp.where(lane == 3 * N_HEADS, c1[:, h:h + 1],
                         jnp.where(lane == 3 * N_HEADS + 1, c2[:, h:h + 1],
                                   jnp.where(lane == 3 * N_HEADS + 2, c3[:, h:h + 1],
                                             jnp.where(own, 1.0, 0.0))))

    _stack_q_features(q_ref, qa_ref, tq, q_features)
    qa = qa_ref[...]
    tpos = _row_pos(t0, tq)

    jd = t0 // tk
    kpos = jd * tk + lax.broadcasted_iota(jnp.int32, (1, tk), 1)
    o = _stacked_attend(qa, ka_ref, va_ref, tk, jd, 0,
                        lambda s: jnp.where(kpos <= tpos, s, NEG), lambda j, s: s)
    o_ref[...] = _unstack_o(o, tq).astype(BF16)


def _fox_p_call(qf, ka, va, c_tm, B, T, tq, tk):
    n = B * T
    nq = T // tq
    return pl.pallas_call(
        functools.partial(_fox_p_kernel, tq=tq, tk=tk),
        grid=(B, nq),
        in_specs=[pl.BlockSpec((tq, 512), lambda b, i: (b * nq + i, 0)),
                  pl.BlockSpec((T, 256), lambda b, i: (b, 0)),
                  pl.BlockSpec((T, 256), lambda b, i: (b, 0)),
                  pl.BlockSpec((tq, LANES), lambda b, i: (b * nq + i, 0))],
        out_specs=pl.BlockSpec((tq, 512), lambda b, i: (b * nq + i, 0)),
        out_shape=jax.ShapeDtypeStruct((n, 512), BF16),
        scratch_shapes=[pltpu.VMEM((N_HEADS * tq, 2 * LANES), BF16)],
        compiler_params=_cparams(("arbitrary", "arbitrary")),
        name="fox_prompt",
    )(qf, ka, va, c_tm)


def _chunk_dot(x_ref, chunk0, n, w1_ref):
    rows = jnp.concatenate(
        [x_ref[pl.ds(chunk0 * CMP_STRIDE + c, n, stride=CMP_STRIDE), :].astype(BF16)
         for c in range(CMP_STRIDE)], axis=1)
    return _dot(rows, w1_ref[...])


def _compress(xk_ref, xv_ref, n, w1k_ref, w1v_ref, posk_ref, posv_ref, w2k_ref, w2v_ref, bd, gk):
    return _compress_finish(_chunk_dot(xk_ref, 0, n, w1k_ref), _chunk_dot(xv_ref, 0, n, w1v_ref), n,
                            w1k_ref, w1v_ref, posk_ref, posv_ref, w2k_ref, w2v_ref, bd, gk)


def _compress_finish(acc_k, acc_v, n, w1k_ref, w1v_ref, posk_ref, posv_ref, w2k_ref, w2v_ref, bd, gk):
    ck = _dot(posk_ref[...], w1k_ref[...])
    cv = _dot(posv_ref[...], w1v_ref[...])

    def finish(acc, cst, w2_ref):
        lead = acc[:, 0:2 * LANES] + cst[0:1, 0:2 * LANES]
        trail = acc[:, 2 * LANES:4 * LANES] + cst[1:2, 2 * LANES:4 * LANES]
        hid = lead + pltpu.roll(trail, n - 1, 0)
        act = hid * _sigmoid(hid)
        return _dot(act.astype(BF16), w2_ref[...])

    kc = _head_rms(finish(acc_k, ck, w2k_ref), bd) * gk
    vc = finish(acc_v, cv, w2v_ref)
    return kc, vc


def _cmp_p_kernel(xk_ref, xv_ref, w1k_ref, w1v_ref, posk_ref, posv_ref, w2k_ref, w2v_ref, bd_ref, gk_ref,
                  kc_ref, vc_ref):
    n = kc_ref.shape[0]
    kc, vc = _compress(xk_ref, xv_ref, n, w1k_ref, w1v_ref, posk_ref, posv_ref, w2k_ref, w2v_ref,
                       bd_ref[...], gk_ref[...])
    kc_ref[:, 0:LANES] = kc.astype(BF16)
    c_end = lax.broadcasted_iota(jnp.int32, (n, 1), 0) * CMP_STRIDE + (2 * CMP_STRIDE - 1)
    lane = lax.broadcasted_iota(jnp.int32, (n, LANES), 1)
    kc_ref[:, LANES:2 * LANES] = _key_pos_features(c_end, lane).astype(BF16)
    vc_ref[...] = vc.astype(BF16)


def _cmp_p_call(nrow, cw, B, T):
    nc = T // CMP_STRIDE
    consts = [cw["w1k"], cw["w1v"], cw["posk"], cw["posv"], cw["w2k"], cw["w2v"], cw["bd"], cw["gk"]]
    cspecs = [pl.BlockSpec(a.shape, (lambda b, nd=a.ndim: (0,) * nd)) for a in consts]
    return pl.pallas_call(
        _cmp_p_kernel,
        grid=(B,),
        in_specs=[pl.BlockSpec((T, LANES), lambda b: (b, 0)),
                  pl.BlockSpec((T, LANES), lambda b: (b, 1))] + cspecs,
        out_specs=[pl.BlockSpec((nc, 2 * LANES), lambda b: (b, 0)),
                   pl.BlockSpec((nc, LANES), lambda b: (b, 0))],
        out_shape=[jax.ShapeDtypeStruct((B * nc, 2 * LANES), BF16),
                   jax.ShapeDtypeStruct((B * nc, LANES), BF16)],
        compiler_params=_cparams(("arbitrary",)),
        name="compress_prompt",
    )(nrow, nrow, *consts)


def _gate_cols(gate_ref, tn):
    R = N_HEADS * tn
    gl = lax.broadcasted_iota(jnp.int32, (R, LANES), 1)
    hrow = lax.broadcasted_iota(jnp.int32, (R, LANES), 0) // tn
    gt = jnp.broadcast_to(gate_ref[...][None], (N_HEADS, tn, LANES)).reshape(R, LANES)
    return [jnp.sum(jnp.where(gl == GATE_LANE0 + 3 * hrow + j, gt, 0.0), axis=-1, keepdims=True)
            for j in range(3)]


def _nsa_p_kernel(q_ref, ks_ref, vs_ref, kw_ref, vw_ref, kc_ref, vc_ref, gate_ref, a_ref, o_ref, qa_ref,
                  *, tq, tk, tw, n_sel):
    qi = pl.program_id(1)
    t0 = qi * tq
    R = N_HEADS * tq
    nc = kc_ref.shape[0]
    lane = lax.broadcasted_iota(jnp.int32, (tq, LANES), 1)
    pos_q = t0 + lax.broadcasted_iota(jnp.int32, (tq, 1), 0)
    _stack_q_features(q_ref, qa_ref, tq, lambda h: _query_pos_features(pos_q, _slope(h), lane))
    qa = qa_ref[...]
    tpos = _row_pos(t0, tq)

    nid = lax.broadcasted_iota(jnp.int32, (1, nc), 1)
    cvalid = (nid * CMP_STRIDE + (2 * CMP_STRIDE - 1) <= tpos) & (nid < nc - 1)
    s = jnp.where(cvalid, _dot_nt(qa, kc_ref[...]), NEG)
    mx = jnp.max(s, axis=-1, keepdims=True)
    p = jnp.where(cvalid, jnp.exp(s - mx), 0.0)
    den = jnp.sum(p, axis=-1, keepdims=True)
    p = p / jnp.where(den > 0, den, 1.0)
    o_c = _dot(p.astype(BF16), vc_ref[...])
    p4 = p.reshape(N_GROUPS, N_REP, tq, nc)
    psum = (p4[:, 0] + p4[:, 1] + p4[:, 2] + p4[:, 3]).reshape(N_GROUPS * tq, nc)

    def in_window(j, s):
        kpos = j * tw + lax.broadcasted_iota(jnp.int32, (1, tw), 1)
        return jnp.where((kpos <= tpos) & (tpos - kpos < WINDOW), s, NEG)

    jw = t0 // tw
    o_w = _stacked_attend(qa, kw_ref, vw_ref, tw, jw, jnp.maximum(t0 - (WINDOW - 1), 0) // tw,
                          lambda s: in_window(jw, s), in_window)

    imp = _dot_exact_r(psum, a_ref[...])
    cur = (t0 + lax.rem(lax.broadcasted_iota(jnp.int32, (N_GROUPS * tq, 1), 0), tq)) // SEL_BLOCK
    seln = pltpu.roll((_select_blocks_by_rank(imp, cur, n_sel) - 1.0) * MASK_BIG, _AUG_SEL0, 1)
    for g in range(N_GROUPS):
        for r in range(N_REP):
            h = g * N_REP + r
            feat = jnp.where(lane >= _AUG_SEL0, seln[g * tq:(g + 1) * tq],
                             _query_pos_features(pos_q, _slope(h), lane))
            qa_ref[h * tq:(h + 1) * tq, LANES:2 * LANES] = feat.astype(BF16)
    qa = qa_ref[...]

    js = t0 // tk
    kpos = js * tk + lax.broadcasted_iota(jnp.int32, (1, tk), 1)
    o_s = _stacked_attend(qa, ks_ref, vs_ref, tk, js, 0,
                          lambda s: jnp.where(kpos <= tpos, s, NEG), lambda j, s: s)

    gc, gs, gw = _gate_cols(gate_ref, tq)
    o_ref[...] = _unstack_o(gc * o_c + gs * o_s + gw * o_w, tq).astype(BF16)


def _nsa_p_call(qn, ks, vs, kw, vw, kc, vc, gates, a_mat, B, T, tq, tk, tw):
    n = B * T
    nq = T // tq
    nc = T // CMP_STRIDE
    seq = lambda b, i: (b, 0)
    blk = lambda b, i: (b * nq + i, 0)
    return pl.pallas_call(
        functools.partial(_nsa_p_kernel, tq=tq, tk=tk, tw=tw, n_sel=-(-T // SEL_BLOCK)),
        grid=(B, nq),
        in_specs=[pl.BlockSpec((tq, 512), blk),
                  pl.BlockSpec((T, 256), seq), pl.BlockSpec((T, 256), seq),
                  pl.BlockSpec((T, 256), seq), pl.BlockSpec((T, 256), seq),
                  pl.BlockSpec((nc, 2 * LANES), seq),
                  pl.BlockSpec((nc, LANES), seq),
                  pl.BlockSpec((tq, LANES), blk),
                  pl.BlockSpec(a_mat.shape, lambda b, i: (0, 0))],
        out_specs=pl.BlockSpec((tq, 512), blk),
        out_shape=jax.ShapeDtypeStruct((n, 512), BF16),
        scratch_shapes=[pltpu.VMEM((N_HEADS * tq, 2 * LANES), BF16)],
        compiler_params=_cparams(("arbitrary", "arbitrary")),
        name="nsa_prompt",
    )(qn, ks, vs, kw, vw, kc, vc, gates, a_mat)


def _mix_kernel(x_ref, of_ref, on_ref, sc_ref, sh_ref, gt_ref, g1_ref, wmg_ref, wbf_ref, wbn_ref,
                wout_ref, o_ref):
    nb, tt, d = x_ref.shape
    tm = nb * tt
    x = x_ref[...]
    ms = jnp.mean(x * x, axis=-1, keepdims=True)
    h = x * lax.rsqrt(ms + RMS_EPS) * g1_ref[...] * (1.0 + sc_ref[...]) + sh_ref[...]
    h = h.reshape(tm, d).astype(BF16)
    g_fox = _sigmoid(_dot(h, wmg_ref[:, 0:d]))
    g_nsa = _sigmoid(_dot(h, wmg_ref[:, d:2 * d]))
    mix = g_fox * _dot(of_ref[...], wbf_ref[...]) + g_nsa * _dot(on_ref[...], wbn_ref[...])
    y = _dot(mix.astype(BF16), wout_ref[...]).reshape(nb, tt, d)
    o_ref[...] = x + gt_ref[...] * y


def _mix_call(x3, o_fox, o_nsa, mod3, g1, wmg, wbf, wbn, wout, nb, tt):
    NB, TT, d = x3.shape
    tm = nb * tt
    gi, gj = NB // nb, TT // tt
    full = lambda i, j: (0, 0)
    tok = lambda i, j: (i * gj + j, 0)
    return pl.pallas_call(
        _mix_kernel,
        grid=(gi, gj),
        in_specs=[pl.BlockSpec((nb, tt, d), lambda i, j: (i, j, 0)),
                  pl.BlockSpec((tm, 512), tok),
                  pl.BlockSpec((tm, 512), tok),
                  pl.BlockSpec((nb, 1, d), lambda i, j: (i, 0, 1)),
                  pl.BlockSpec((nb, 1, d), lambda i, j: (i, 0, 0)),
                  pl.BlockSpec((nb, 1, d), lambda i, j: (i, 0, 2)),
                  pl.BlockSpec((1, d), full),
                  pl.BlockSpec(wmg.shape, full),
                  pl.BlockSpec(wbf.shape, full),
                  pl.BlockSpec(wbn.shape, full),
                  pl.BlockSpec(wout.shape, full)],
        out_specs=pl.BlockSpec((nb, tt, d), lambda i, j: (i, j, 0)),
        out_shape=jax.ShapeDtypeStruct((NB, TT, d), F32),
        compiler_params=_cparams(("arbitrary", "arbitrary")),
        name="post_mix",
    )(x3, o_fox, o_nsa, mod3, mod3, mod3, g1, wmg, wbf, wbn, wout)


def _ffn_kernel(x_ref, sc_ref, sh_ref, gt_ref, g2_ref, wup_ref, wdn_ref, o_ref, *, fc):
    nb, tt, d = x_ref.shape
    tm = nb * tt
    x = x_ref[...]
    ms = jnp.mean(x * x, axis=-1, keepdims=True)
    h = x * lax.rsqrt(ms + RMS_EPS) * g2_ref[...] * (1.0 + sc_ref[...]) + sh_ref[...]
    h = h.reshape(tm, d).astype(BF16)
    acc = jnp.zeros((tm, d), F32)
    for c in range(D_FF // fc):
        u = jnp.maximum(_dot(h, wup_ref[:, c * fc:(c + 1) * fc]), 0.0)
        acc = acc + _dot((u * u).astype(BF16), wdn_ref[c * fc:(c + 1) * fc, :])
    o_ref[...] = x + gt_ref[...] * acc.reshape(nb, tt, d)


def _ffn_call(x3, mod3, g2, wup, wdn, nb, tt):
    NB, TT, d = x3.shape
    gi, gj = NB // nb, TT // tt
    full = lambda i, j: (0, 0)
    return pl.pallas_call(
        functools.partial(_ffn_kernel, fc=1024),
        grid=(gi, gj),
        in_specs=[pl.BlockSpec((nb, tt, d), lambda i, j: (i, j, 0)),
                  pl.BlockSpec((nb, 1, d), lambda i, j: (i, 0, 4)),
                  pl.BlockSpec((nb, 1, d), lambda i, j: (i, 0, 3)),
                  pl.BlockSpec((nb, 1, d), lambda i, j: (i, 0, 5)),
                  pl.BlockSpec((1, d), full),
                  pl.BlockSpec(wup.shape, full, pipeline_mode=pl.Buffered(1)),
                  pl.BlockSpec(wdn.shape, full, pipeline_mode=pl.Buffered(1))],
        out_specs=pl.BlockSpec((nb, tt, d), lambda i, j: (i, j, 0)),
        out_shape=jax.ShapeDtypeStruct((NB, TT, d), F32),
        compiler_params=_cparams(("arbitrary", "arbitrary")),
        name="ffn",
    )(x3, mod3, mod3, mod3, g2, wup, wdn)


def _gather_pages(pt_ref, seq, parts, n_pages):
    for p in range(n_pages):
        page = pt_ref[seq, p]
        for pool_ref, row0, buf_ref, sem_ref in parts:
            pltpu.make_async_copy(pool_ref.at[page, pl.ds(row0, buf_ref.shape[0]), :],
                                  buf_ref.at[:, p * PAGE_SIZE:(p + 1) * PAGE_SIZE], sem_ref).start()


def _gather_done(parts):
    for _, _, buf_ref, sem_ref in parts:
        pltpu.make_async_copy(buf_ref, buf_ref, sem_ref).wait()


def _two_sequence_pipeline(pt_ref, parts_a, parts_b, n_pages, compute):
    i = pl.program_id(0)
    n = pl.num_programs(0)

    @pl.when(i == 0)
    def _():
        _gather_pages(pt_ref, 0, parts_a, n_pages)

    _gather_pages(pt_ref, 2 * i + 1, parts_b, n_pages)
    _gather_done(parts_a)
    out_a = compute(0, [part[2] for part in parts_a])
    _gather_pages(pt_ref, lax.rem(2 * i + 2, 2 * n), parts_a, n_pages)
    _gather_done(parts_b)
    out_b = compute(1, [part[2] for part in parts_b])

    @pl.when(i == n - 1)
    def _():
        _gather_done(parts_a)

    return out_a, out_b


def _softmax_pv(tiles):
    m = functools.reduce(jnp.maximum, [jnp.max(s, axis=-1, keepdims=True) for s, _ in tiles])
    l = jnp.zeros_like(m)
    acc = None
    for s, pv in tiles:
        p = jnp.exp(s - m)
        l = l + jnp.sum(p, axis=-1, keepdims=True)
        contrib = pv(p.astype(BF16))
        acc = contrib if acc is None else acc + contrib
    return acc / l


def _lane_cumsum(x, n):
    lane = lax.broadcasted_iota(jnp.int32, x.shape, 1)
    s = 1
    while s < n:
        x = x + jnp.where(lane >= s, pltpu.roll(x, s, 1), 0.0)
        s *= 2
    return x


def _stack_q(q_ref, tn):
    parts = []
    for g in range(N_GROUPS):
        for r in range(N_REP):
            qc = q_ref[:, r * LANES:(r + 1) * LANES]
            parts.append(jnp.where(_half_mask(tn, g), qc, jnp.zeros_like(qc)))
    return jnp.concatenate(parts, axis=0).astype(BF16)


def _unstack_o(o, tn):
    chunks = []
    for r in range(N_REP):
        a = o[r * tn:(r + 1) * tn]
        b = o[(N_REP + r) * tn:(N_REP + r + 1) * tn]
        chunks.append(jnp.where(_half_mask(tn, 0), a, b))
    return jnp.concatenate(chunks, axis=1)


def _per_head_rows(x8, tn):
    n = x8.shape[1]
    return jnp.broadcast_to(x8[:, None, :], (N_HEADS, tn, n)).reshape(N_HEADS * tn, n)


def _slope_col(tn):
    hrow = lax.broadcasted_iota(jnp.int32, (N_HEADS * tn, 1), 0) // tn
    col = jnp.zeros((N_HEADS * tn, 1), F32)
    for h in range(N_HEADS):
        col = jnp.where(hrow == h, _slope(h), col)
    return col


def _tok_col(tn):
    return lax.rem(lax.broadcasted_iota(jnp.int32, (N_HEADS * tn, 1), 0), tn)


def _pad_keys(x, dtype):
    tn = x.shape[0]
    return jnp.concatenate([x, jnp.zeros((LANES - tn, LANES), x.dtype)], axis=0).astype(dtype)


def _past_key_tiles(qs, kv_ref, kc, bias_fn):
    tiles = []
    for c in range(kv_ref.shape[1] // kc):
        kt = kv_ref[0:LANES, c * kc:(c + 1) * kc].astype(BF16)
        vt = kv_ref[LANES:2 * LANES, c * kc:(c + 1) * kc].astype(BF16)
        tiles.append((_dot(qs, kt) + bias_fn(c), lambda p, vt=vt: _dot_nt(p, vt)))
    return tiles


def _new_key_tile(qs, new_ref, col0, bias):
    knew = _pad_keys(new_ref[:, col0:col0 + LANES], BF16)
    vnew = _pad_keys(new_ref[:, col0 + LANES:col0 + 2 * LANES], BF16)
    return (_dot_nt(qs, knew) + bias, lambda p: _dot(p, vnew))


def _fox_s_kernel(pt_ref, pool_ref, lpool_ref, q_ref, new_ref, lfn_ref, o_ref,
                  kva_ref, kvb_ref, lfa_ref, lfb_ref, sem_ref, *, n_pages, tn, kc):
    past = n_pages * PAGE_SIZE
    R = N_HEADS * tn
    tcol = _tok_col(tn)
    tl = lax.broadcasted_iota(jnp.int32, (R, LANES), 1)

    def compute(u, bufs):
        kv_ref, lf_ref = bufs
        rows = pl.ds(u * tn, tn)
        qs = _stack_q(q_ref.at[rows], tn)
        cpast = _lane_cumsum(lf_ref[...], past)
        cn = _lane_cumsum(lfn_ref[u], tn) + cpast[:, past - 1:past]
        cn_rows = _per_head_rows(cn, tn)
        cq = jnp.sum(jnp.where(tl == tcol, cn_rows, 0.0), axis=-1, keepdims=True)
        tiles = [_new_key_tile(qs, new_ref.at[rows], 0, jnp.where(tl <= tcol, cq - cn_rows, NEG))]
        tiles += _past_key_tiles(qs, kv_ref, kc,
                                 lambda c: cq - _per_head_rows(cpast[:, c * kc:(c + 1) * kc], tn))
        return _unstack_o(_softmax_pv(tiles), tn)

    parts_a = [(pool_ref, 0, kva_ref, sem_ref.at[0]), (lpool_ref, 0, lfa_ref, sem_ref.at[1])]
    parts_b = [(pool_ref, 0, kvb_ref, sem_ref.at[2]), (lpool_ref, 0, lfb_ref, sem_ref.at[3])]
    outs = _two_sequence_pipeline(pt_ref, parts_a, parts_b, n_pages, compute)
    o_ref[...] = jnp.concatenate(outs, axis=0).astype(BF16)


def _fox_s_call(page_table, fox_t, logf_t, qf, frow, lf_new, tn):
    B, n_pages = page_table.shape
    past = n_pages * PAGE_SIZE
    kc = min(past, 2048)
    assert B % 2 == 0
    grid_spec = pltpu.PrefetchScalarGridSpec(
        num_scalar_prefetch=1,
        grid=(B // 2,),
        in_specs=[pl.BlockSpec(memory_space=pl.ANY),
                  pl.BlockSpec(memory_space=pl.ANY),
                  pl.BlockSpec((2 * tn, 512), lambda b, pt: (b, 0)),
                  pl.BlockSpec((2 * tn, 256), lambda b, pt: (b, 0)),
                  pl.BlockSpec((2, N_HEADS, LANES), lambda b, pt: (b, 0, 0))],
        out_specs=pl.BlockSpec((2 * tn, 512), lambda b, pt: (b, 0)),
        scratch_shapes=[pltpu.VMEM((2 * LANES, past), F32), pltpu.VMEM((2 * LANES, past), F32),
                        pltpu.VMEM((N_HEADS, past), F32), pltpu.VMEM((N_HEADS, past), F32),
                        pltpu.SemaphoreType.DMA((4,))],
    )
    return pl.pallas_call(
        functools.partial(_fox_s_kernel, n_pages=n_pages, tn=tn, kc=kc),
        grid_spec=grid_spec,
        out_shape=jax.ShapeDtypeStruct((B * tn, 512), BF16),
        compiler_params=_cparams(("arbitrary",)),
        name="fox_sample",
    )(page_table, fox_t, logf_t, qf, frow, lf_new)


def _cmp_s_kernel(pt_ref, pool_ref, q_ref, w1k_ref, w1v_ref, posk_ref, posv_ref, w2k_ref, w2v_ref,
                  bd_ref, gk_ref, a_ref, oc_ref, imp_ref, kva_ref, kvb_ref, xk_ref, xv_ref, sem_ref,
                  *, n_pages, tn):
    past = n_pages * PAGE_SIZE
    nc = past // CMP_STRIDE
    R = N_HEADS * tn
    n_split = 4 if n_pages % 4 == 0 else 1
    pps = n_pages // n_split
    cps = pps * PAGE_SIZE // CMP_STRIDE

    def compute(u, bufs):
        (kv_ref,) = bufs
        acc_k, acc_v = [], []
        for sp in range(n_split):
            for pg in range(sp * pps, (sp + 1) * pps):
                tok = slice(pg * PAGE_SIZE, (pg + 1) * PAGE_SIZE)
                xk_ref[tok, :] = kv_ref[0:LANES, tok].T
                xv_ref[tok, :] = kv_ref[LANES:2 * LANES, tok].T
            acc_k.append(_chunk_dot(xk_ref, sp * cps, cps, w1k_ref))
            acc_v.append(_chunk_dot(xv_ref, sp * cps, cps, w1v_ref))
        kc, vc = _compress_finish(jnp.concatenate(acc_k, axis=0), jnp.concatenate(acc_v, axis=0), nc,
                                  w1k_ref, w1v_ref, posk_ref, posv_ref, w2k_ref, w2v_ref,
                                  bd_ref[...], gk_ref[...])
        qs = _stack_q(q_ref.at[pl.ds(u * tn, tn)], tn)
        nid = lax.broadcasted_iota(jnp.int32, (R, nc), 1)
        tpos = past + _tok_col(tn)
        dc = tpos - (nid * CMP_STRIDE + (2 * CMP_STRIDE - 1))
        valid = (dc >= 0) & (nid < nc - 1)
        s = jnp.where(valid, _dot_nt(qs, kc.astype(BF16)) - _slope_col(tn) * dc.astype(F32), NEG)
        mx = jnp.max(s, axis=-1, keepdims=True)
        p = jnp.where(valid, jnp.exp(s - mx), 0.0)
        den = jnp.sum(p, axis=-1, keepdims=True)
        p = p / jnp.where(den > 0, den, 1.0)
        oc_ref[u] = _dot(p.astype(BF16), vc.astype(BF16))
        p4 = p.reshape(N_GROUPS, N_REP, tn, nc)
        psum = (p4[:, 0] + p4[:, 1] + p4[:, 2] + p4[:, 3]).reshape(N_GROUPS * tn, nc)
        imp_ref[u] = _dot_exact_r(psum, a_ref[...])

    _two_sequence_pipeline(pt_ref, [(pool_ref, 0, kva_ref, sem_ref.at[0])],
                           [(pool_ref, 0, kvb_ref, sem_ref.at[1])], n_pages, compute)


def _cmp_s_call(page_table, nsa_pool, qn, cw, a_mat, tn):
    B, n_pages = page_table.shape
    past = n_pages * PAGE_SIZE
    consts = [cw["w1k"], cw["w1v"], cw["posk"], cw["posv"], cw["w2k"], cw["w2v"], cw["bd"], cw["gk"], a_mat]
    cspecs = [pl.BlockSpec(a.shape, (lambda b, pt, nd=a.ndim: (0,) * nd)) for a in consts]
    nsl = a_mat.shape[1]
    grid_spec = pltpu.PrefetchScalarGridSpec(
        num_scalar_prefetch=1,
        grid=(B // 2,),
        in_specs=[pl.BlockSpec(memory_space=pl.ANY),
                  pl.BlockSpec((2 * tn, 512), lambda b, pt: (b, 0))] + cspecs,
        out_specs=[pl.BlockSpec((2, N_HEADS * tn, LANES), lambda b, pt: (b, 0, 0)),
                   pl.BlockSpec((2, N_GROUPS * tn, nsl), lambda b, pt: (b, 0, 0))],
        scratch_shapes=[pltpu.VMEM((2 * LANES, past), F32), pltpu.VMEM((2 * LANES, past), F32),
                        pltpu.VMEM((past, LANES), F32), pltpu.VMEM((past, LANES), F32),
                        pltpu.SemaphoreType.DMA((2,))],
    )
    return pl.pallas_call(
        functools.partial(_cmp_s_kernel, n_pages=n_pages, tn=tn),
        grid_spec=grid_spec,
        out_shape=[jax.ShapeDtypeStruct((B, N_HEADS * tn, LANES), F32),
                   jax.ShapeDtypeStruct((B, N_GROUPS * tn, nsl), F32)],
        compiler_params=_cparams(("arbitrary",)),
        name="compress_sample",
    )(page_table, nsa_pool, qn, *consts)


def _topk_s_kernel(imp_ref, o_ref, *, tn, past):
    rows = imp_ref.shape[0]
    t = past + lax.rem(lax.broadcasted_iota(jnp.int32, (rows, 1), 0), tn)
    sel = _select_blocks(imp_ref[...], t // SEL_BLOCK)
    o_ref[...] = ((sel - 1.0) * MASK_BIG).astype(BF16)


def _topk_s_call(imp2, tn, past):
    rows, nsl = imp2.shape
    tr = min(rows, 256)
    return pl.pallas_call(
        functools.partial(_topk_s_kernel, tn=tn, past=past),
        grid=(rows // tr,),
        in_specs=[pl.BlockSpec((tr, nsl), lambda i: (i, 0))],
        out_specs=pl.BlockSpec((tr, nsl), lambda i: (i, 0)),
        out_shape=jax.ShapeDtypeStruct((rows, nsl), BF16),
        compiler_params=_cparams(("arbitrary",)),
        name="select_sample",
    )(imp2)


def _sel_s_kernel(pt_ref, pool_ref, q_ref, nnew_ref, wnew_ref, win_ref, seln_ref, oc_ref, gate_ref, e_ref,
                  o_ref, kva_ref, kvb_ref, sem_ref, *, n_pages, tn, kc):
    past = n_pages * PAGE_SIZE
    R = N_HEADS * tn
    n_chunks = past // kc
    wlen = win_ref.shape[2]
    tcol = _tok_col(tn)
    slope = _slope_col(tn)
    tl = lax.broadcasted_iota(jnp.int32, (R, LANES), 1)
    newbias = jnp.where(tl <= tcol, -slope * (tcol - tl).astype(F32), NEG)

    def group_rows(x):
        n = x.shape[1]
        x4 = jnp.broadcast_to(x.reshape(N_GROUPS, 1, tn, n), (N_GROUPS, N_REP, tn, n))
        return x4.reshape(R, n)

    def compute(u, bufs):
        (kv_ref,) = bufs
        rows = pl.ds(u * tn, tn)
        qs = _stack_q(q_ref.at[rows], tn)

        seln_new = seln_ref[u, n_chunks]
        mb_new = group_rows(_dot(seln_new, e_ref[...])[:, 0:LANES])

        def past_bias(c):
            mb = group_rows(_dot(seln_ref[u, c], e_ref[...]))
            kpos = c * kc + lax.broadcasted_iota(jnp.int32, (1, kc), 1)
            return mb - slope * ((past + tcol) - kpos).astype(F32)

        tiles = [_new_key_tile(qs, nnew_ref.at[rows], 2 * LANES, jnp.where(tl <= tcol, newbias + mb_new, NEG))]
        o_s = _softmax_pv(tiles + _past_key_tiles(qs, kv_ref, kc, past_bias))

        kt = win_ref[u, 0:LANES, :].astype(BF16)
        vt = win_ref[u, LANES:2 * LANES, :].astype(BF16)
        wpos = (past - wlen) + lax.broadcasted_iota(jnp.int32, (1, wlen), 1)
        dw = (past + tcol) - wpos
        bias = jnp.where((dw < WINDOW) & (wpos >= 0), -slope * dw.astype(F32), NEG)
        o_w = _softmax_pv([_new_key_tile(qs, wnew_ref.at[rows], 0, newbias),
                           (_dot(qs, kt) + bias, lambda p: _dot_nt(p, vt))])

        gc, gs, gw = _gate_cols(gate_ref.at[rows], tn)
        return _unstack_o(gc * oc_ref[u] + gs * o_s + gw * o_w, tn)

    outs = _two_sequence_pipeline(pt_ref, [(pool_ref, 2 * LANES, kva_ref, sem_ref.at[0])],
                                  [(pool_ref, 2 * LANES, kvb_ref, sem_ref.at[1])], n_pages, compute)
    o_ref[...] = jnp.concatenate(outs, axis=0).astype(BF16)


def _sel_s_call(page_table, nsa_pool, qn, nrow, wrow, win_t, seln, oc, gates, e_mat, tn, kc):
    B, n_pages = page_table.shape
    past = n_pages * PAGE_SIZE
    wlen = win_t.shape[2]
    grid_spec = pltpu.PrefetchScalarGridSpec(
        num_scalar_prefetch=1,
        grid=(B // 2,),
        in_specs=[pl.BlockSpec(memory_space=pl.ANY),
                  pl.BlockSpec((2 * tn, 512), lambda b, pt: (b, 0)),
                  pl.BlockSpec((2 * tn, 512), lambda b, pt: (b, 0)),
                  pl.BlockSpec((2 * tn, 256), lambda b, pt: (b, 0)),
                  pl.BlockSpec((2, 2 * LANES, wlen), lambda b, pt: (b, 0, 0)),
                  pl.BlockSpec((2,) + seln.shape[1:], lambda b, pt: (b, 0, 0, 0)),
                  pl.BlockSpec((2, N_HEADS * tn, LANES), lambda b, pt: (b, 0, 0)),
                  pl.BlockSpec((2 * tn, LANES), lambda b, pt: (b, 0)),
                  pl.BlockSpec(e_mat.shape, lambda b, pt: (0, 0))],
        out_specs=pl.BlockSpec((2 * tn, 512), lambda b, pt: (b, 0)),
        scratch_shapes=[pltpu.VMEM((2 * LANES, past), F32), pltpu.VMEM((2 * LANES, past), F32),
                        pltpu.SemaphoreType.DMA((2,))],
    )
    return pl.pallas_call(
        functools.partial(_sel_s_kernel, n_pages=n_pages, tn=tn, kc=kc),
        grid_spec=grid_spec,
        out_shape=jax.ShapeDtypeStruct((B * tn, 512), BF16),
        compiler_params=_cparams(("arbitrary",)),
        name="select_attend_sample",
    )(page_table, nsa_pool, qn, nrow, wrow, win_t, seln, oc, gates, e_mat)


def _pair_cols(base):
    idx = []
    for r in range(N_REP):
        for g in range(N_GROUPS):
            h = g * N_REP + r
            idx.extend(range(base + h * HEAD_DIM, base + (h + 1) * HEAD_DIM))
    return np.asarray(idx, np.int32)


def _prep_weights(w_in, b_fox_f, fox_qn_g, fox_kn_g, nsa_qn_g, nsa_kn_slc_g, nsa_kn_win_g):
    o_fq, o_fk, o_fv, o_ff, o_nq, o_nkv, o_ng, o_mg = 0, 512, 640, 768, 776, 1288, 2056, 2080
    cols = np.concatenate([
        _pair_cols(o_fq), np.arange(o_fk, o_fk + 128), np.arange(o_fv, o_fv + 128),
        _pair_cols(o_nq), np.arange(o_nkv, o_nkv + 768),
        np.arange(o_ff, o_ff + 8), np.arange(o_ng, o_ng + 24)]).astype(np.int32)
    w_p = jnp.take(w_in, cols, axis=1)
    w_p = jnp.pad(w_p, ((0, 0), (0, _C_END - w_p.shape[1]))).astype(BF16)
    w_mg = w_in[:, o_mg:o_mg + 2 * D_MODEL].astype(BF16)
    tile2 = lambda g: jnp.tile(g, 2)
    gains = jnp.stack([tile2(fox_qn_g), tile2(fox_kn_g), tile2(nsa_qn_g), tile2(nsa_kn_slc_g),
                       tile2(nsa_kn_win_g)] + [jnp.zeros((LANES,), F32)] * 3)
    bff = jnp.zeros((1, LANES), F32).at[0, LOGF_LANE0:LOGF_LANE0 + N_HEADS].set(b_fox_f)
    return w_p, w_mg, gains, bff


def _prep_compress(pos, w1, w2):
    w1r = w1.reshape(2, CMP_STRIDE, HEAD_DIM, CMP_HIDDEN)
    z = jnp.zeros((CMP_STRIDE, HEAD_DIM, CMP_HIDDEN), F32)
    top = jnp.concatenate([w1r[0], z, w1r[1], z], axis=-1)
    bot = jnp.concatenate([z, w1r[0], z, w1r[1]], axis=-1)
    w1b = jnp.concatenate([top, bot], axis=1).astype(BF16)
    w1b = w1b.reshape(CMP_STRIDE * LANES, 4 * LANES)
    pr = pos.reshape(2, CMP_STRIDE, HEAD_DIM)
    posb = jnp.concatenate([jnp.tile(pr, (1, 1, 2)).transpose(1, 0, 2),
                            jnp.zeros((CMP_STRIDE, 6, LANES), F32)], axis=1).astype(BF16)
    posb = posb.transpose(1, 0, 2).reshape(8, CMP_STRIDE * LANES)
    zz = jnp.zeros((CMP_HIDDEN, HEAD_DIM), F32)
    w2b = jnp.concatenate([jnp.concatenate([w2, zz], axis=1),
                           jnp.concatenate([zz, w2], axis=1)], axis=0).astype(BF16)
    return w1b, posb, w2b


def kernel(x_prompt, x_sample, cache_fox_kv, cache_fox_logf, cache_nsa_kv, state_win_kv, page_table,
           c_prompt, c_sample, norm1_g, norm2_g, w_ada, b_ada, w_in, b_fox_f, fox_qn_g, fox_kn_g,
           nsa_qn_g, nsa_kn_cmp_g, nsa_kn_slc_g, nsa_kn_win_g, cmp_pos_k, cmp_w1_k, cmp_w2_k,
           cmp_pos_v, cmp_w1_v, cmp_w2_v, w_br_fox, w_br_nsa, w_out, w_up, w_down):
    assert norm1_g.shape[0] == 1
    B, T, d = x_prompt.shape
    DB, TN, _ = x_sample.shape
    n_phys = cache_fox_kv.shape[1]
    n_pages = page_table.shape[1]
    past = n_pages * PAGE_SIZE
    wlen = state_win_kv.shape[2]
    assert T % 256 == 0 and T >= WINDOW and wlen == WINDOW and TN == 8 and past % CMP_STRIDE == 0

    w_p, w_mg, gains, bff = _prep_weights(w_in[0], b_fox_f[0], fox_qn_g[0], fox_kn_g[0], nsa_qn_g[0],
                                          nsa_kn_slc_g[0], nsa_kn_win_g[0])
    w1k, posk, w2k = _prep_compress(cmp_pos_k[0], cmp_w1_k[0], cmp_w2_k[0])
    w1v, posv, w2v = _prep_compress(cmp_pos_v[0], cmp_w1_v[0], cmp_w2_v[0])
    bd = jnp.asarray(np.kron(np.eye(2), np.ones((HEAD_DIM, HEAD_DIM))), BF16)
    bd2 = jnp.asarray(np.kron(np.eye(4), np.ones((HEAD_DIM, HEAD_DIM))), BF16)
    cw = dict(w1k=w1k, w1v=w1v, posk=posk, posv=posv, w2k=w2k, w2v=w2v, bd=bd,
              gk=jnp.tile(nsa_kn_cmp_g[0], 2).reshape(1, LANES))
    pair_rows = _pair_cols(0)
    wbf = jnp.take(w_br_fox[0], pair_rows, axis=0).astype(BF16)
    wbn = jnp.take(w_br_nsa[0], pair_rows, axis=0).astype(BF16)
    wout = w_out[0].astype(BF16)
    wup = w_up[0].astype(BF16)
    wdn = w_down[0].astype(BF16)
    g1 = norm1_g[0].reshape(1, d)
    g2 = norm2_g[0].reshape(1, d)
    tm_p = 512
    tri = jnp.asarray(np.tril(np.ones((tm_p, tm_p), np.float32)), BF16)

    mod = _ada_call(jnp.concatenate([c_prompt, c_sample], axis=0), w_ada[0], b_ada[0])
    mod_p = mod[:B].reshape(B, 1, 6 * d)
    mod_s = mod[B:].reshape(DB, 1, 6 * d)

    (qf, frow_t, qn, nrow_t, wrow_t, gates, lf, c_tm, kf, vf, ks, vs, kw, vw, nraw) = _pre_call(
        x_prompt, mod_p, g1, w_p, bd2, tri, gains, bff, nb=1, tt=tm_p, do_cum=True)
    tq, tk, tw = 128, 512, 256
    o_fox = _fox_p_call(qf, kf, vf, c_tm, B, T, tq, tk)
    kc_p, vc_p = _cmp_p_call(nraw, cw, B, T)
    nc_p = T // CMP_STRIDE
    nsel_p = -(-T // SEL_BLOCK)
    assert nsel_p <= LANES - _AUG_SEL0 and T % tk == 0
    a_p = jnp.asarray(_importance_matrix(nc_p, LANES, nc_p - 1)[:, :LANES] *
                      (np.arange(LANES) < nsel_p)[None, :], BF16)
    o_nsa = _nsa_p_call(qn, ks, vs, kw, vw, kc_p, vc_p, gates, a_p, B, T, tq, tk, tw)
    x1 = _mix_call(x_prompt, o_fox, o_nsa, mod_p, g1, w_mg, wbf, wbn, wout, nb=1, tt=tm_p)
    y_prompt = _ffn_call(x1, mod_p, g2, wup, wdn, nb=1, tt=tm_p)

    nb_s = min(DB, 32)
    (qf_s, frow_s, qn_s, nrow_s, wrow_s, gates_s, lf_s) = _pre_call(
        x_sample, mod_s, g1, w_p, bd2, tri, gains, bff, nb=nb_s, tt=TN, do_cum=False)
    qf_s = qf_s.astype(F32)
    qn_s = qn_s.astype(F32)

    fox_t = jnp.transpose(cache_fox_kv[0], (0, 2, 3, 4, 1)).reshape(n_phys, 2 * LANES, PAGE_SIZE)
    nsa_t = jnp.transpose(cache_nsa_kv[0], (0, 2, 3, 4, 1)).reshape(n_phys, 4 * LANES, PAGE_SIZE)
    logf_t = jnp.transpose(cache_fox_logf[0], (0, 2, 1))
    win_t = jnp.transpose(state_win_kv[0], (0, 2, 3, 4, 1)).reshape(DB, 2 * LANES, wlen)

    lf_new = lf_s[:, LOGF_LANE0:LOGF_LANE0 + N_HEADS].reshape(DB, TN, N_HEADS).transpose(0, 2, 1)
    lf_new = jnp.pad(lf_new, ((0, 0), (0, 0), (0, LANES - TN)))
    o_fox_s = _fox_s_call(page_table, fox_t, logf_t, qf_s, frow_s, lf_new, TN)

    nsa_pool = nsa_t
    nc_s = past // CMP_STRIDE
    nsel_s = -(-(past + TN) // SEL_BLOCK)
    nsl = -(-nsel_s // LANES) * LANES
    a_s = jnp.asarray(_importance_matrix(nc_s, nsl, nc_s - 1) * (np.arange(nsl) < nsel_s)[None, :], BF16)
    oc_s, imp_s = _cmp_s_call(page_table, nsa_pool, qn_s, cw, a_s, TN)
    seln = _topk_s_call(imp_s.reshape(DB * N_GROUPS * TN, nsl), TN, past)
    kc_keys = min(past, 2048)
    bpc = kc_keys // SEL_BLOCK
    seln = seln.reshape(DB, N_GROUPS * TN, nsl // bpc, bpc).transpose(0, 2, 1, 3)
    e_s = jnp.asarray((np.arange(kc_keys)[None, :] // SEL_BLOCK) == np.arange(bpc)[:, None], BF16)
    o_nsa_s = _sel_s_call(page_table, nsa_pool, qn_s, nrow_s, wrow_s, win_t, seln, oc_s, gates_s, e_s,
                          TN, kc_keys)
    x1_s = _mix_call(x_sample, o_fox_s, o_nsa_s, mod_s, g1, w_mg, wbf, wbn, wout, nb=nb_s, tt=TN)
    y_sample = _ffn_call(x1_s, mod_s, g2, wup, wdn, nb=nb_s, tt=TN)

    def token_major(rows_t, n_slots):
        toks = rows_t.shape[2]
        return rows_t.reshape(B, n_slots, N_GROUPS, HEAD_DIM, toks).transpose(0, 4, 1, 2, 3)

    lf_p = lf[:, LOGF_LANE0:LOGF_LANE0 + N_HEADS]
    win_s = jnp.concatenate([state_win_kv[0, :, TN:],
                             wrow_s.reshape(DB, TN, 2, N_GROUPS, HEAD_DIM)], axis=1)
    return (y_prompt, y_sample,
            token_major(frow_t, 2)[None],
            lf_p.reshape(1, B, T, N_HEADS),
            token_major(nrow_t, 4)[None],
            token_major(wrow_t[:, :, T - wlen:], 2)[None],
            frow_s.reshape(1, DB, TN, 2, N_GROUPS, HEAD_DIM),
            lf_s[:, LOGF_LANE0:LOGF_LANE0 + N_HEADS].reshape(1, DB, TN, N_HEADS),
            nrow_s.reshape(1, DB, TN, 4, N_GROUPS, HEAD_DIM),
            win_s[None])
```

```python
import functools

import numpy as np
import jax
import jax.numpy as jnp
from jax import lax
from jax.experimental import pallas as pl
from jax.experimental.pallas import tpu as pltpu

F32 = jnp.float32
BF16 = jnp.bfloat16

D_MODEL = 1024
HEAD_DIM = 64
N_HEADS = 8
N_GROUPS = 2
N_REP = N_HEADS // N_GROUPS
PAGE_SIZE = 128
CMP_STRIDE = 16
CMP_HIDDEN = 2 * HEAD_DIM
SEL_BLOCK = 64
N_SELECT = 16
WINDOW = 512
D_FF = 4 * D_MODEL
RMS_EPS = 1e-6
FORCE_BONUS = 1.0e4
LANES = 128
NEG = -1e30
MASK_BIG = 2.0 ** 100
LOGF_LANE0 = 0
GATE_LANE0 = 8
VMEM_LIMIT = 56 * 1024 * 1024


def _dot(a, b):
    return jnp.dot(a, b, preferred_element_type=F32)


def _dot_nt(a, b):
    return lax.dot_general(a, b, (((1,), (1,)), ((), ())), preferred_element_type=F32)


def _split3(x):
    x1 = x.astype(BF16)
    r = x - x1.astype(F32)
    x2 = r.astype(BF16)
    x3 = (r - x2.astype(F32)).astype(BF16)
    return x1, x2, x3


def _dot_exact_r(x, m):
    a, b, c = _split3(x)
    return _dot(a, m) + _dot(b, m) + _dot(c, m)


def _dot_exact_l(m, x):
    a, b, c = _split3(x)
    return _dot(m, a) + _dot(m, b) + _dot(m, c)


def _sigmoid(x):
    return 1.0 / (1.0 + jnp.exp(-x))


def _head_rms(zc, bd):
    a = zc * zc
    a1 = a.astype(BF16)
    a2 = (a - a1.astype(F32)).astype(BF16)
    ss = _dot(a1, bd) + _dot(a2, bd)
    return zc * lax.rsqrt(ss * (1.0 / HEAD_DIM) + RMS_EPS)


def _cparams(sem, vmem=VMEM_LIMIT):
    return pltpu.CompilerParams(dimension_semantics=sem, vmem_limit_bytes=vmem)


def _ada_kernel(c_ref, w_ref, b_ref, o_ref):
    c = c_ref[...]
    a = c * _sigmoid(c)
    o_ref[...] = _dot(a.astype(BF16), w_ref[...].astype(BF16)) + b_ref[...]


def _ada_call(c, w_ada, b_ada):
    r, d = c.shape
    n = w_ada.shape[1]
    tn = 1024
    return pl.pallas_call(
        _ada_kernel,
        grid=(n // tn,),
        in_specs=[pl.BlockSpec((r, d), lambda j: (0, 0)),
                  pl.BlockSpec((d, tn), lambda j: (0, j)),
                  pl.BlockSpec((1, tn), lambda j: (0, j))],
        out_specs=pl.BlockSpec((r, tn), lambda j: (0, j)),
        out_shape=jax.ShapeDtypeStruct((r, n), F32),
        compiler_params=_cparams(("arbitrary",)),
        name="ada_mod",
    )(c, w_ada, b_ada.reshape(1, n))


_C_FQ, _C_FK, _C_FV, _C_NQ, _C_NKV, _C_SMALL, _C_END = 0, 512, 640, 768, 1280, 2048, 2176


def _v_with_ones(v):
    lane = lax.broadcasted_iota(jnp.int32, v.shape, 1)
    return jnp.concatenate([jnp.where(lane < HEAD_DIM, v, 1.0), jnp.where(lane < HEAD_DIM, 1.0, v)],
                           axis=1).astype(BF16)


def _pos_hi_lo(pos):
    return (pos >> 8).astype(F32), (pos & 255).astype(F32)


_AUG_SEL0 = 4


def _key_pos_features(pos, lane):
    hi, lo = _pos_hi_lo(pos)
    return jnp.where(lane < 2, 1.0,
                     jnp.where(lane == 2, hi,
                               jnp.where(lane == 3, lo,
                                         jnp.where(lane - _AUG_SEL0 == pos // SEL_BLOCK, 1.0, 0.0))))


def _query_pos_features(pos, slope, lane):
    hi, lo = _pos_hi_lo(pos)
    return jnp.where(lane == 0, -(slope * 256.0) * hi,
                     jnp.where(lane == 1, -slope * lo,
                               jnp.where(lane == 2, slope * 256.0,
                                         jnp.where(lane == 3, slope, 0.0))))


def _pre_kernel(x_ref, sc_ref, sh_ref, g1_ref, w_ref, bd_ref, tri_ref, gains_ref, bff_ref,
                qf_ref, frow_ref, qn_ref, nrow_ref, wrow_ref, gate_ref, lf_ref,
                *rest, do_cum):
    nb, tt, d = x_ref.shape
    tm = nb * tt
    x = x_ref[...]
    ms = jnp.mean(x * x, axis=-1, keepdims=True)
    h = x * lax.rsqrt(ms + RMS_EPS) * g1_ref[...] * (1.0 + sc_ref[...]) + sh_ref[...]
    h = h.reshape(tm, d).astype(BF16)
    bd2 = bd_ref[...]

    def proj(c0, n_chunks=1):
        z = _dot(h, w_ref[:, c0:c0 + n_chunks * LANES])
        return [z[:, i * LANES:(i + 1) * LANES] for i in range(n_chunks)]

    def rms_pair(za, zb):
        n = _head_rms(jnp.concatenate([za, zb], axis=1), bd2)
        return n[:, 0:LANES], n[:, LANES:2 * LANES]

    scale = HEAD_DIM ** -0.5
    for q_ref, c0, grow in ((qf_ref, _C_FQ, 0), (qn_ref, _C_NQ, 2)):
        z = proj(c0, N_REP)
        for r, zc in enumerate(rms_pair(z[0], z[1]) + rms_pair(z[2], z[3])):
            q_ref[:, r * LANES:(r + 1) * LANES] = (zc * (gains_ref[grow:grow + 1, :] * scale)).astype(BF16)

    def put(ref, i, val):
        if do_cum:
            ref[0, i * LANES:(i + 1) * LANES, :] = val.T
        else:
            ref[:, i * LANES:(i + 1) * LANES] = val

    fk, fv = proj(_C_FK, 2)
    ck, cv = proj(_C_NKV, 2)
    sk, sv = proj(_C_NKV + 2 * LANES, 2)
    wk, wv = proj(_C_NKV + 4 * LANES, 2)
    fk, sk = rms_pair(fk, sk)
    fk = fk * gains_ref[1:2, :]
    sk = sk * gains_ref[3:4, :]
    wk = _head_rms(wk, bd2[0:LANES, 0:LANES]) * gains_ref[4:5, :]
    put(frow_ref, 0, fk)
    put(frow_ref, 1, fv)
    for i, val in enumerate((ck, cv, sk, sv)):
        put(nrow_ref, i, val)
    put(wrow_ref, 0, wk)
    put(wrow_ref, 1, wv)

    (zl,) = proj(_C_SMALL)
    gate_ref[...] = _sigmoid(zl)
    xl = zl + bff_ref[...]
    lf = jnp.minimum(xl, 0.0) - jnp.log1p(jnp.exp(-jnp.abs(xl)))
    lf_ref[...] = lf

    if do_cum:
        c_ref, kf_ref, vf_ref, ks_ref, vs_ref, kw_ref, vw_ref, nraw_ref, carry_ref = rest
        nraw_ref[:, 0:LANES] = ck
        nraw_ref[:, LANES:2 * LANES] = cv

        @pl.when(pl.program_id(1) == 0)
        def _():
            carry_ref[...] = jnp.zeros_like(carry_ref)

        c = _dot_exact_l(tri_ref[...], lf) + carry_ref[0:1, :]
        c_ref[...] = c
        carry_ref[...] = jnp.broadcast_to(c[tm - 1:tm, :], carry_ref.shape)

        lane = lax.broadcasted_iota(jnp.int32, (tm, LANES), 1)
        c1, c2, c3 = (p.astype(F32) for p in _split3(c))
        aug_f = jnp.where(lane < N_HEADS, -c1,
                          jnp.where(lane < 2 * N_HEADS, -pltpu.roll(c2, N_HEADS, 1),
                                    jnp.where(lane < 3 * N_HEADS, -pltpu.roll(c3, 2 * N_HEADS, 1),
                                              jnp.where(lane < 3 * N_HEADS + 3, 1.0, 0.0))))
        kf_ref[:, 0:LANES] = fk.astype(BF16)
        kf_ref[:, LANES:2 * LANES] = aug_f.astype(BF16)
        vf_ref[...] = _v_with_ones(fv)

        pos = pl.program_id(1) * tm + lax.broadcasted_iota(jnp.int32, (tm, 1), 0)
        aug_p = _key_pos_features(pos, lane).astype(BF16)
        ks_ref[:, 0:LANES] = sk.astype(BF16)
        ks_ref[:, LANES:2 * LANES] = aug_p
        vs_ref[...] = _v_with_ones(sv)
        kw_ref[:, 0:LANES] = wk.astype(BF16)
        kw_ref[:, LANES:2 * LANES] = aug_p
        vw_ref[...] = _v_with_ones(wv)


def _pre_call(x3, mod3, g1, w_p, bd, tri, gains, bff, nb, tt, do_cum):
    NB, TT, d = x3.shape
    tm = nb * tt
    n = NB * TT
    gi, gj = NB // nb, TT // tt
    tok = lambda i, j: (i * gj + j, 0)
    full = lambda i, j: (0, 0)
    in_specs = [
        pl.BlockSpec((nb, tt, d), lambda i, j: (i, j, 0)),
        pl.BlockSpec((nb, 1, d), lambda i, j: (i, 0, 1)),
        pl.BlockSpec((nb, 1, d), lambda i, j: (i, 0, 0)),
        pl.BlockSpec((1, d), full),
        pl.BlockSpec(w_p.shape, full),
        pl.BlockSpec(bd.shape, full),
        pl.BlockSpec(tri.shape, full),
        pl.BlockSpec(gains.shape, full),
        pl.BlockSpec(bff.shape, full),
    ]
    def rows_out(width):
        if do_cum:
            return (jax.ShapeDtypeStruct((NB, width, TT), F32),
                    pl.BlockSpec((1, width, tm), lambda i, j: (i, 0, j)))
        return jax.ShapeDtypeStruct((n, width), F32), pl.BlockSpec((tm, width), tok)

    def tok_out(width, dtype):
        return jax.ShapeDtypeStruct((n, width), dtype), pl.BlockSpec((tm, width), tok)

    outs = [
        tok_out(512, BF16),
        rows_out(256),
        tok_out(512, BF16),
        rows_out(512),
        rows_out(256),
        tok_out(LANES, F32),
        tok_out(LANES, F32),
    ]
    scratch = []
    if do_cum:
        assert nb == 1
        outs += [tok_out(LANES, F32)]
        outs += [tok_out(256, BF16)] * 6
        outs += [tok_out(256, F32)]
        scratch = [pltpu.VMEM((8, LANES), F32)]
    out_shape = [o[0] for o in outs]
    out_specs = [o[1] for o in outs]
    return pl.pallas_call(
        functools.partial(_pre_kernel, do_cum=do_cum),
        grid=(gi, gj),
        in_specs=in_specs,
        out_specs=out_specs,
        out_shape=out_shape,
        scratch_shapes=scratch,
        compiler_params=_cparams(("arbitrary", "arbitrary")),
        name="pre_mixer",
    )(x3, mod3, mod3, g1, w_p, bd, tri, gains, bff)


def _half_mask(rows, g):
    lane = lax.broadcasted_iota(jnp.int32, (rows, LANES), 1)
    return (lane < HEAD_DIM) if g == 0 else (lane >= HEAD_DIM)


def _slope(h):
    return 2.0 ** (-8.0 * (h + 1) / N_HEADS)


def _select_blocks(imp, cur):
    blk = lax.broadcasted_iota(jnp.int32, imp.shape, 1)
    forced = (blk == 0) | (blk == cur) | (blk == cur - 1)
    score = jnp.where(blk <= cur, imp + jnp.where(forced, FORCE_BONUS, 0.0), -jnp.inf)
    blkf = blk.astype(F32)

    def body(_, carry):
        sc, sel = carry
        mx = jnp.max(sc, axis=-1, keepdims=True)
        idx = jnp.min(jnp.where(sc == mx, blkf, 1e9), axis=-1, keepdims=True)
        pick = blkf == idx
        return jnp.where(pick, -jnp.inf, sc), jnp.where(pick, 1.0, sel)

    _, sel = lax.fori_loop(0, N_SELECT, body, (score, jnp.zeros(imp.shape, F32)))
    return jnp.where(blk <= cur, sel, 0.0)


def _select_blocks_by_rank(imp_t, cur, n_blocks):
    blk = lax.broadcasted_iota(jnp.int32, imp_t.shape, 0)
    forced = (blk == 0) | (blk == cur) | (blk == cur - 1)
    score = jnp.where(blk <= cur, imp_t + jnp.where(forced, FORCE_BONUS, 0.0), -jnp.inf)
    beaten_by = jnp.zeros(imp_t.shape, F32)
    for i in range(n_blocks):
        si = score[i:i + 1, :]
        beaten_by = beaten_by + jnp.where((si > score) | ((si == score) & (blk > i)), 1.0, 0.0)
    return jnp.where((beaten_by < N_SELECT) & (blk <= cur), 1.0, 0.0)


def _importance_matrix(n_cmp_rows, n_sel_cols, n_cmp):
    r = SEL_BLOCK // CMP_STRIDE
    a = np.zeros((n_cmp_rows, n_sel_cols), np.float32)
    for n in range(n_cmp):
        for j in range(n_sel_cols):
            off = n - r * j
            if off in (-1, r - 1):
                a[n, j] = 0.5
            elif 0 <= off <= r - 2:
                a[n, j] = 1.0
    return a


def _stack_q_features(q_ref, qa_ref, tq, feat_fn):
    for g in range(N_GROUPS):
        for r in range(N_REP):
            h = g * N_REP + r
            qc = q_ref[:, r * LANES:(r + 1) * LANES]
            qa_ref[h * tq:(h + 1) * tq, 0:LANES] = jnp.where(_half_mask(tq, g), qc, jnp.zeros_like(qc))
            qa_ref[h * tq:(h + 1) * tq, LANES:2 * LANES] = feat_fn(h).astype(BF16)


def _stacked_attend(qa, k_ref, v_ref, tk, first, lo, mask_first, mask_rest):
    half = qa.shape[0] // 2

    def probabilities(j, mask, m_old):
        st = pl.multiple_of(j * tk, tk)
        s = mask(_dot_nt(qa, k_ref[pl.ds(st, tk), :]))
        m_new = jnp.max(s, axis=-1, keepdims=True)
        if m_old is not None:
            m_new = jnp.maximum(m_old, m_new)
        p = jnp.exp(s - m_new).astype(BF16)
        va = v_ref[pl.ds(st, tk), :]
        pv = jnp.concatenate([_dot(p[:half], va[:, 0:LANES]), _dot(p[half:], va[:, LANES:2 * LANES])], axis=0)
        return m_new, pv

    def body(j, carry):
        m_old, acc = carry
        m_new, pv = probabilities(j, lambda s: mask_rest(j, s), m_old)
        return m_new, jnp.exp(m_old - m_new) * acc + pv

    m, acc = lax.fori_loop(lo, first, body, probabilities(first, mask_first, None))
    return acc / pltpu.roll(acc, HEAD_DIM, 1)


def _row_pos(t0, tq):
    return t0 + lax.rem(lax.broadcasted_iota(jnp.int32, (N_HEADS * tq, 1), 0), tq)


def _fox_p_kernel(q_ref, ka_ref, va_ref, c_ref, o_ref, qa_ref, *, tq, tk):
    qi = pl.program_id(1)
    t0 = qi * tq
    R = N_HEADS * tq
    lane = lax.broadcasted_iota(jnp.int32, (tq, LANES), 1)
    c1, c2, c3 = (p.astype(F32) for p in _split3(c_ref[...]))

    def q_features(h):
        own = (lane == h) | (lane == N_HEADS + h) | (lane == 2 * N_HEADS + h)
        return jnp.where(lane == 3 * N_HEADS, c1[:, h:h + 1],
                         jnp.where(lane == 3 * N_HEADS + 1, c2[:, h:h + 1],
                                   jnp.where(lane == 3 * N_HEADS + 2, c3[:, h:h + 1],
                                             jnp.where(own, 1.0, 0.0))))

    _stack_q_features(q_ref, qa_ref, tq, q_features)
    qa = qa_ref[...]
    tpos = _row_pos(t0, tq)

    jd = t0 // tk
    kpos = jd * tk + lax.broadcasted_iota(jnp.int32, (1, tk), 1)
    o = _stacked_attend(qa, ka_ref, va_ref, tk, jd, 0,
                        lambda s: jnp.where(kpos <= tpos, s, NEG), lambda j, s: s)
    o_ref[...] = _unstack_o(o, tq).astype(BF16)


def _fox_p_call(qf, ka, va, c_tm, B, T, tq, tk):
    n = B * T
    nq = T // tq
    return pl.pallas_call(
        functools.partial(_fox_p_kernel, tq=tq, tk=tk),
        grid=(B, nq),
        in_specs=[pl.BlockSpec((tq, 512), lambda b, i: (b * nq + i, 0)),
                  pl.BlockSpec((T, 256), lambda b, i: (b, 0)),
                  pl.BlockSpec((T, 256), lambda b, i: (b, 0)),
                  pl.BlockSpec((tq, LANES), lambda b, i: (b * nq + i, 0))],
        out_specs=pl.BlockSpec((tq, 512), lambda b, i: (b * nq + i, 0)),
        out_shape=jax.ShapeDtypeStruct((n, 512), BF16),
        scratch_shapes=[pltpu.VMEM((N_HEADS * tq, 2 * LANES), BF16)],
        compiler_params=_cparams(("arbitrary", "arbitrary")),
        name="fox_prompt",
    )(qf, ka, va, c_tm)


def _chunk_dot(x_ref, chunk0, n, w1_ref):
    rows = jnp.concatenate(
        [x_ref[pl.ds(chunk0 * CMP_STRIDE + c, n, stride=CMP_STRIDE), :].astype(BF16)
         for c in range(CMP_STRIDE)], axis=1)
    return _dot(rows, w1_ref[...])


def _compress(xk_ref, xv_ref, n, w1k_ref, w1v_ref, posk_ref, posv_ref, w2k_ref, w2v_ref, bd, gk):
    return _compress_finish(_chunk_dot(xk_ref, 0, n, w1k_ref), _chunk_dot(xv_ref, 0, n, w1v_ref), n,
                            w1k_ref, w1v_ref, posk_ref, posv_ref, w2k_ref, w2v_ref, bd, gk)


def _compress_finish(acc_k, acc_v, n, w1k_ref, w1v_ref, posk_ref, posv_ref, w2k_ref, w2v_ref, bd, gk):
    ck = _dot(posk_ref[...], w1k_ref[...])
    cv = _dot(posv_ref[...], w1v_ref[...])

    def finish(acc, cst, w2_ref):
        lead = acc[:, 0:2 * LANES] + cst[0:1, 0:2 * LANES]
        trail = acc[:, 2 * LANES:4 * LANES] + cst[1:2, 2 * LANES:4 * LANES]
        hid = lead + pltpu.roll(trail, n - 1, 0)
        act = hid * _sigmoid(hid)
        return _dot(act.astype(BF16), w2_ref[...])

    kc = _head_rms(finish(acc_k, ck, w2k_ref), bd) * gk
    vc = finish(acc_v, cv, w2v_ref)
    return kc, vc


def _cmp_p_kernel(xk_ref, xv_ref, w1k_ref, w1v_ref, posk_ref, posv_ref, w2k_ref, w2v_ref, bd_ref, gk_ref,
                  kc_ref, vc_ref):
    n = kc_ref.shape[0]
    kc, vc = _compress(xk_ref, xv_ref, n, w1k_ref, w1v_ref, posk_ref, posv_ref, w2k_ref, w2v_ref,
                       bd_ref[...], gk_ref[...])
    kc_ref[:, 0:LANES] = kc.astype(BF16)
    c_end = lax.broadcasted_iota(jnp.int32, (n, 1), 0) * CMP_STRIDE + (2 * CMP_STRIDE - 1)
    lane = lax.broadcasted_iota(jnp.int32, (n, LANES), 1)
    kc_ref[:, LANES:2 * LANES] = _key_pos_features(c_end, lane).astype(BF16)
    vc_ref[...] = vc.astype(BF16)


def _cmp_p_call(nrow, cw, B, T):
    nc = T // CMP_STRIDE
    consts = [cw["w1k"], cw["w1v"], cw["posk"], cw["posv"], cw["w2k"], cw["w2v"], cw["bd"], cw["gk"]]
    cspecs = [pl.BlockSpec(a.shape, (lambda b, nd=a.ndim: (0,) * nd)) for a in consts]
    return pl.pallas_call(
        _cmp_p_kernel,
        grid=(B,),
        in_specs=[pl.BlockSpec((T, LANES), lambda b: (b, 0)),
                  pl.BlockSpec((T, LANES), lambda b: (b, 1))] + cspecs,
        out_specs=[pl.BlockSpec((nc, 2 * LANES), lambda b: (b, 0)),
                   pl.BlockSpec((nc, LANES), lambda b: (b, 0))],
        out_shape=[jax.ShapeDtypeStruct((B * nc, 2 * LANES), BF16),
                   jax.ShapeDtypeStruct((B * nc, LANES), BF16)],
        compiler_params=_cparams(("arbitrary",)),
        name="compress_prompt",
    )(nrow, nrow, *consts)


def _gate_cols(gate_ref, tn):
    R = N_HEADS * tn
    gl = lax.broadcasted_iota(jnp.int32, (R, LANES), 1)
    hrow = lax.broadcasted_iota(jnp.int32, (R, LANES), 0) // tn
    gt = jnp.broadcast_to(gate_ref[...][None], (N_HEADS, tn, LANES)).reshape(R, LANES)
    return [jnp.sum(jnp.where(gl == GATE_LANE0 + 3 * hrow + j, gt, 0.0), axis=-1, keepdims=True)
            for j in range(3)]


def _nsa_p_kernel(q_ref, ks_ref, vs_ref, kw_ref, vw_ref, kc_ref, vc_ref, gate_ref, at_ref, o_ref, qa_ref,
                  *, tq, tk, tw, n_sel):
    qi = pl.program_id(1)
    t0 = qi * tq
    R = N_HEADS * tq
    nc = kc_ref.shape[0]
    lane = lax.broadcasted_iota(jnp.int32, (tq, LANES), 1)
    pos_q = t0 + lax.broadcasted_iota(jnp.int32, (tq, 1), 0)
    _stack_q_features(q_ref, qa_ref, tq, lambda h: _query_pos_features(pos_q, _slope(h), lane))
    qa = qa_ref[...]
    tpos = _row_pos(t0, tq)

    nid = lax.broadcasted_iota(jnp.int32, (1, nc), 1)
    cvalid = (nid * CMP_STRIDE + (2 * CMP_STRIDE - 1) <= tpos) & (nid < nc - 1)
    s = jnp.where(cvalid, _dot_nt(qa, kc_ref[...]), NEG)
    mx = jnp.max(s, axis=-1, keepdims=True)
    p = jnp.where(cvalid, jnp.exp(s - mx), 0.0)
    den = jnp.sum(p, axis=-1, keepdims=True)
    p = p / jnp.where(den > 0, den, 1.0)
    o_c = _dot(p.astype(BF16), vc_ref[...])
    p4 = p.reshape(N_GROUPS, N_REP, tq, nc)
    psum = (p4[:, 0] + p4[:, 1] + p4[:, 2] + p4[:, 3]).reshape(N_GROUPS * tq, nc)

    assert tw + tq <= WINDOW

    def key_pos(j):
        return j * tw + lax.broadcasted_iota(jnp.int32, (1, tw), 1)

    jw = t0 // tw
    o_w = _stacked_attend(qa, kw_ref, vw_ref, tw, jw, jnp.maximum(t0 - (WINDOW - 1), 0) // tw,
                          lambda s: jnp.where(key_pos(jw) <= tpos, s, NEG),
                          lambda j, s: jnp.where(tpos - key_pos(j) < WINDOW, s, NEG))

    p1, p2, p3 = _split3(psum)
    at = at_ref[...]
    imp_t = _dot_nt(at, p1) + _dot_nt(at, p2) + _dot_nt(at, p3)
    n_rows = -(-n_sel // 8) * 8
    cur = (t0 + lax.rem(lax.broadcasted_iota(jnp.int32, (1, N_GROUPS * tq), 1), tq)) // SEL_BLOCK
    sel_t = _select_blocks_by_rank(imp_t[0:n_rows], cur, n_sel)
    sel_t = jnp.concatenate([sel_t, jnp.zeros((LANES - n_rows, N_GROUPS * tq), F32)], axis=0)
    seln = pltpu.roll(((sel_t - 1.0) * MASK_BIG).T, _AUG_SEL0, 1)
    for g in range(N_GROUPS):
        for r in range(N_REP):
            h = g * N_REP + r
            feat = jnp.where(lane >= _AUG_SEL0, seln[g * tq:(g + 1) * tq],
                             _query_pos_features(pos_q, _slope(h), lane))
            qa_ref[h * tq:(h + 1) * tq, LANES:2 * LANES] = feat.astype(BF16)
    qa = qa_ref[...]

    js = t0 // tk
    kpos = js * tk + lax.broadcasted_iota(jnp.int32, (1, tk), 1)
    o_s = _stacked_attend(qa, ks_ref, vs_ref, tk, js, 0,
                          lambda s: jnp.where(kpos <= tpos, s, NEG), lambda j, s: s)

    gc, gs, gw = _gate_cols(gate_ref, tq)
    o_ref[...] = _unstack_o(gc * o_c + gs * o_s + gw * o_w, tq).astype(BF16)


def _nsa_p_call(qn, ks, vs, kw, vw, kc, vc, gates, a_mat, B, T, tq, tk, tw):
    n = B * T
    nq = T // tq
    nc = T // CMP_STRIDE
    seq = lambda b, i: (b, 0)
    blk = lambda b, i: (b * nq + i, 0)
    return pl.pallas_call(
        functools.partial(_nsa_p_kernel, tq=tq, tk=tk, tw=tw, n_sel=-(-T // SEL_BLOCK)),
        grid=(B, nq),
        in_specs=[pl.BlockSpec((tq, 512), blk),
                  pl.BlockSpec((T, 256), seq), pl.BlockSpec((T, 256), seq),
                  pl.BlockSpec((T, 256), seq), pl.BlockSpec((T, 256), seq),
                  pl.BlockSpec((nc, 2 * LANES), seq),
                  pl.BlockSpec((nc, LANES), seq),
                  pl.BlockSpec((tq, LANES), blk),
                  pl.BlockSpec(a_mat.shape, lambda b, i: (0, 0))],
        out_specs=pl.BlockSpec((tq, 512), blk),
        out_shape=jax.ShapeDtypeStruct((n, 512), BF16),
        scratch_shapes=[pltpu.VMEM((N_HEADS * tq, 2 * LANES), BF16)],
        compiler_params=_cparams(("arbitrary", "arbitrary")),
        name="nsa_prompt",
    )(qn, ks, vs, kw, vw, kc, vc, gates, a_mat)


def _mix_kernel(x_ref, of_ref, on_ref, sc_ref, sh_ref, gt_ref, g1_ref, wmg_ref, wbf_ref, wbn_ref,
                wout_ref, o_ref):
    nb, tt, d = x_ref.shape
    tm = nb * tt
    x = x_ref[...]
    ms = jnp.mean(x * x, axis=-1, keepdims=True)
    h = x * lax.rsqrt(ms + RMS_EPS) * g1_ref[...] * (1.0 + sc_ref[...]) + sh_ref[...]
    h = h.reshape(tm, d).astype(BF16)
    g_fox = _sigmoid(_dot(h, wmg_ref[:, 0:d]))
    g_nsa = _sigmoid(_dot(h, wmg_ref[:, d:2 * d]))
    mix = g_fox * _dot(of_ref[...], wbf_ref[...]) + g_nsa * _dot(on_ref[...], wbn_ref[...])
    y = _dot(mix.astype(BF16), wout_ref[...]).reshape(nb, tt, d)
    o_ref[...] = x + gt_ref[...] * y


def _mix_call(x3, o_fox, o_nsa, mod3, g1, wmg, wbf, wbn, wout, nb, tt):
    NB, TT, d = x3.shape
    tm = nb * tt
    gi, gj = NB // nb, TT // tt
    full = lambda i, j: (0, 0)
    tok = lambda i, j: (i * gj + j, 0)
    return pl.pallas_call(
        _mix_kernel,
        grid=(gi, gj),
        in_specs=[pl.BlockSpec((nb, tt, d), lambda i, j: (i, j, 0)),
                  pl.BlockSpec((tm, 512), tok),
                  pl.BlockSpec((tm, 512), tok),
                  pl.BlockSpec((nb, 1, d), lambda i, j: (i, 0, 1)),
                  pl.BlockSpec((nb, 1, d), lambda i, j: (i, 0, 0)),
                  pl.BlockSpec((nb, 1, d), lambda i, j: (i, 0, 2)),
                  pl.BlockSpec((1, d), full),
                  pl.BlockSpec(wmg.shape, full),
                  pl.BlockSpec(wbf.shape, full),
                  pl.BlockSpec(wbn.shape, full),
                  pl.BlockSpec(wout.shape, full)],
        out_specs=pl.BlockSpec((nb, tt, d), lambda i, j: (i, j, 0)),
        out_shape=jax.ShapeDtypeStruct((NB, TT, d), F32),
        compiler_params=_cparams(("arbitrary", "arbitrary")),
        name="post_mix",
    )(x3, o_fox, o_nsa, mod3, mod3, mod3, g1, wmg, wbf, wbn, wout)


def _ffn_kernel(x_ref, sc_ref, sh_ref, gt_ref, g2_ref, wup_ref, wdn_ref, o_ref, *, fc):
    nb, tt, d = x_ref.shape
    tm = nb * tt
    x = x_ref[...]
    ms = jnp.mean(x * x, axis=-1, keepdims=True)
    h = x * lax.rsqrt(ms + RMS_EPS) * g2_ref[...] * (1.0 + sc_ref[...]) + sh_ref[...]
    h = h.reshape(tm, d).astype(BF16)
    acc = jnp.zeros((tm, d), F32)
    for c in range(D_FF // fc):
        u = jnp.maximum(_dot(h, wup_ref[:, c * fc:(c + 1) * fc]), 0.0)
        acc = acc + _dot((u * u).astype(BF16), wdn_ref[c * fc:(c + 1) * fc, :])
    o_ref[...] = x + gt_ref[...] * acc.reshape(nb, tt, d)


def _ffn_call(x3, mod3, g2, wup, wdn, nb, tt):
    NB, TT, d = x3.shape
    gi, gj = NB // nb, TT // tt
    full = lambda i, j: (0, 0)
    return pl.pallas_call(
        functools.partial(_ffn_kernel, fc=1024),
        grid=(gi, gj),
        in_specs=[pl.BlockSpec((nb, tt, d), lambda i, j: (i, j, 0)),
                  pl.BlockSpec((nb, 1, d), lambda i, j: (i, 0, 4)),
                  pl.BlockSpec((nb, 1, d), lambda i, j: (i, 0, 3)),
                  pl.BlockSpec((nb, 1, d), lambda i, j: (i, 0, 5)),
                  pl.BlockSpec((1, d), full),
                  pl.BlockSpec(wup.shape, full, pipeline_mode=pl.Buffered(1)),
                  pl.BlockSpec(wdn.shape, full, pipeline_mode=pl.Buffered(1))],
        out_specs=pl.BlockSpec((nb, tt, d), lambda i, j: (i, j, 0)),
        out_shape=jax.ShapeDtypeStruct((NB, TT, d), F32),
        compiler_params=_cparams(("arbitrary", "arbitrary")),
        name="ffn",
    )(x3, mod3, mod3, mod3, g2, wup, wdn)


def _gather_pages(pt_ref, seq, parts, n_pages):
    for p in range(n_pages):
        page = pt_ref[seq, p]
        for pool_ref, row0, buf_ref, sem_ref in parts:
            pltpu.make_async_copy(pool_ref.at[page, pl.ds(row0, buf_ref.shape[0]), :],
                                  buf_ref.at[:, p * PAGE_SIZE:(p + 1) * PAGE_SIZE], sem_ref).start()


def _gather_done(parts):
    for _, _, buf_ref, sem_ref in parts:
        pltpu.make_async_copy(buf_ref, buf_ref, sem_ref).wait()


def _two_sequence_pipeline(pt_ref, parts_a, parts_b, n_pages, compute):
    i = pl.program_id(0)
    n = pl.num_programs(0)

    @pl.when(i == 0)
    def _():
        _gather_pages(pt_ref, 0, parts_a, n_pages)

    _gather_pages(pt_ref, 2 * i + 1, parts_b, n_pages)
    _gather_done(parts_a)
    out_a = compute(0, [part[2] for part in parts_a])
    _gather_pages(pt_ref, lax.rem(2 * i + 2, 2 * n), parts_a, n_pages)
    _gather_done(parts_b)
    out_b = compute(1, [part[2] for part in parts_b])

    @pl.when(i == n - 1)
    def _():
        _gather_done(parts_a)

    return out_a, out_b


def _softmax_pv(tiles):
    m = functools.reduce(jnp.maximum, [jnp.max(s, axis=-1, keepdims=True) for s, _ in tiles])
    l = jnp.zeros_like(m)
    acc = None
    for s, pv in tiles:
        p = jnp.exp(s - m)
        l = l + jnp.sum(p, axis=-1, keepdims=True)
        contrib = pv(p.astype(BF16))
        acc = contrib if acc is None else acc + contrib
    return acc / l


def _lane_cumsum(x, n):
    lane = lax.broadcasted_iota(jnp.int32, x.shape, 1)
    s = 1
    while s < n:
        x = x + jnp.where(lane >= s, pltpu.roll(x, s, 1), 0.0)
        s *= 2
    return x


def _stack_q(q_ref, tn):
    parts = []
    for g in range(N_GROUPS):
        for r in range(N_REP):
            qc = q_ref[:, r * LANES:(r + 1) * LANES]
            parts.append(jnp.where(_half_mask(tn, g), qc, jnp.zeros_like(qc)))
    return jnp.concatenate(parts, axis=0).astype(BF16)


def _unstack_o(o, tn):
    chunks = []
    for r in range(N_REP):
        a = o[r * tn:(r + 1) * tn]
        b = o[(N_REP + r) * tn:(N_REP + r + 1) * tn]
        chunks.append(jnp.where(_half_mask(tn, 0), a, b))
    return jnp.concatenate(chunks, axis=1)


def _per_head_rows(x8, tn):
    n = x8.shape[1]
    return jnp.broadcast_to(x8[:, None, :], (N_HEADS, tn, n)).reshape(N_HEADS * tn, n)


def _slope_col(tn):
    hrow = lax.broadcasted_iota(jnp.int32, (N_HEADS * tn, 1), 0) // tn
    col = jnp.zeros((N_HEADS * tn, 1), F32)
    for h in range(N_HEADS):
        col = jnp.where(hrow == h, _slope(h), col)
    return col


def _tok_col(tn):
    return lax.rem(lax.broadcasted_iota(jnp.int32, (N_HEADS * tn, 1), 0), tn)


def _pad_keys(x, dtype):
    tn = x.shape[0]
    return jnp.concatenate([x, jnp.zeros((LANES - tn, LANES), x.dtype)], axis=0).astype(dtype)


def _past_key_tiles(qs, kv_ref, kc, bias_fn):
    tiles = []
    for c in range(kv_ref.shape[1] // kc):
        kt = kv_ref[0:LANES, c * kc:(c + 1) * kc].astype(BF16)
        vt = kv_ref[LANES:2 * LANES, c * kc:(c + 1) * kc].astype(BF16)
        tiles.append((_dot(qs, kt) + bias_fn(c), lambda p, vt=vt: _dot_nt(p, vt)))
    return tiles


def _new_key_tile(qs, new_ref, col0, bias):
    knew = _pad_keys(new_ref[:, col0:col0 + LANES], BF16)
    vnew = _pad_keys(new_ref[:, col0 + LANES:col0 + 2 * LANES], BF16)
    return (_dot_nt(qs, knew) + bias, lambda p: _dot(p, vnew))


def _fox_s_kernel(pt_ref, pool_ref, lpool_ref, q_ref, new_ref, lfn_ref, o_ref,
                  kva_ref, kvb_ref, lfa_ref, lfb_ref, sem_ref, *, n_pages, tn, kc):
    past = n_pages * PAGE_SIZE
    R = N_HEADS * tn
    tcol = _tok_col(tn)
    tl = lax.broadcasted_iota(jnp.int32, (R, LANES), 1)

    def compute(u, bufs):
        kv_ref, lf_ref = bufs
        rows = pl.ds(u * tn, tn)
        qs = _stack_q(q_ref.at[rows], tn)
        cpast = _lane_cumsum(lf_ref[...], past)
        cn = _lane_cumsum(lfn_ref[u], tn) + cpast[:, past - 1:past]
        cn_rows = _per_head_rows(cn, tn)
        cq = jnp.sum(jnp.where(tl == tcol, cn_rows, 0.0), axis=-1, keepdims=True)
        tiles = [_new_key_tile(qs, new_ref.at[rows], 0, jnp.where(tl <= tcol, cq - cn_rows, NEG))]
        tiles += _past_key_tiles(qs, kv_ref, kc,
                                 lambda c: cq - _per_head_rows(cpast[:, c * kc:(c + 1) * kc], tn))
        return _unstack_o(_softmax_pv(tiles), tn)

    parts_a = [(pool_ref, 0, kva_ref, sem_ref.at[0]), (lpool_ref, 0, lfa_ref, sem_ref.at[1])]
    parts_b = [(pool_ref, 0, kvb_ref, sem_ref.at[2]), (lpool_ref, 0, lfb_ref, sem_ref.at[3])]
    outs = _two_sequence_pipeline(pt_ref, parts_a, parts_b, n_pages, compute)
    o_ref[...] = jnp.concatenate(outs, axis=0).astype(BF16)


def _fox_s_call(page_table, fox_t, logf_t, qf, frow, lf_new, tn):
    B, n_pages = page_table.shape
    past = n_pages * PAGE_SIZE
    kc = min(past, 2048)
    assert B % 2 == 0
    grid_spec = pltpu.PrefetchScalarGridSpec(
        num_scalar_prefetch=1,
        grid=(B // 2,),
        in_specs=[pl.BlockSpec(memory_space=pl.ANY),
                  pl.BlockSpec(memory_space=pl.ANY),
                  pl.BlockSpec((2 * tn, 512), lambda b, pt: (b, 0)),
                  pl.BlockSpec((2 * tn, 256), lambda b, pt: (b, 0)),
                  pl.BlockSpec((2, N_HEADS, LANES), lambda b, pt: (b, 0, 0))],
        out_specs=pl.BlockSpec((2 * tn, 512), lambda b, pt: (b, 0)),
        scratch_shapes=[pltpu.VMEM((2 * LANES, past), F32), pltpu.VMEM((2 * LANES, past), F32),
                        pltpu.VMEM((N_HEADS, past), F32), pltpu.VMEM((N_HEADS, past), F32),
                        pltpu.SemaphoreType.DMA((4,))],
    )
    return pl.pallas_call(
        functools.partial(_fox_s_kernel, n_pages=n_pages, tn=tn, kc=kc),
        grid_spec=grid_spec,
        out_shape=jax.ShapeDtypeStruct((B * tn, 512), BF16),
        compiler_params=_cparams(("arbitrary",)),
        name="fox_sample",
    )(page_table, fox_t, logf_t, qf, frow, lf_new)


def _cmp_s_kernel(pt_ref, pool_ref, q_ref, w1k_ref, w1v_ref, posk_ref, posv_ref, w2k_ref, w2v_ref,
                  bd_ref, gk_ref, a_ref, oc_ref, imp_ref, kva_ref, kvb_ref, xk_ref, xv_ref, sem_ref,
                  *, n_pages, tn):
    past = n_pages * PAGE_SIZE
    nc = past // CMP_STRIDE
    R = N_HEADS * tn
    n_split = 4 if n_pages % 4 == 0 else 1
    pps = n_pages // n_split
    cps = pps * PAGE_SIZE // CMP_STRIDE

    def compute(u, bufs):
        (kv_ref,) = bufs
        acc_k, acc_v = [], []
        for sp in range(n_split):
            for pg in range(sp * pps, (sp + 1) * pps):
                tok = slice(pg * PAGE_SIZE, (pg + 1) * PAGE_SIZE)
                xk_ref[tok, :] = kv_ref[0:LANES, tok].T
                xv_ref[tok, :] = kv_ref[LANES:2 * LANES, tok].T
            acc_k.append(_chunk_dot(xk_ref, sp * cps, cps, w1k_ref))
            acc_v.append(_chunk_dot(xv_ref, sp * cps, cps, w1v_ref))
        kc, vc = _compress_finish(jnp.concatenate(acc_k, axis=0), jnp.concatenate(acc_v, axis=0), nc,
                                  w1k_ref, w1v_ref, posk_ref, posv_ref, w2k_ref, w2v_ref,
                                  bd_ref[...], gk_ref[...])
        qs = _stack_q(q_ref.at[pl.ds(u * tn, tn)], tn)
        nid = lax.broadcasted_iota(jnp.int32, (R, nc), 1)
        tpos = past + _tok_col(tn)
        dc = tpos - (nid * CMP_STRIDE + (2 * CMP_STRIDE - 1))
        valid = (dc >= 0) & (nid < nc - 1)
        s = jnp.where(valid, _dot_nt(qs, kc.astype(BF16)) - _slope_col(tn) * dc.astype(F32), NEG)
        mx = jnp.max(s, axis=-1, keepdims=True)
        p = jnp.where(valid, jnp.exp(s - mx), 0.0)
        den = jnp.sum(p, axis=-1, keepdims=True)
        p = p / jnp.where(den > 0, den, 1.0)
        oc_ref[u] = _dot(p.astype(BF16), vc.astype(BF16))
        p4 = p.reshape(N_GROUPS, N_REP, tn, nc)
        psum = (p4[:, 0] + p4[:, 1] + p4[:, 2] + p4[:, 3]).reshape(N_GROUPS * tn, nc)
        imp_ref[u] = _dot_exact_r(psum, a_ref[...])

    _two_sequence_pipeline(pt_ref, [(pool_ref, 0, kva_ref, sem_ref.at[0])],
                           [(pool_ref, 0, kvb_ref, sem_ref.at[1])], n_pages, compute)


def _cmp_s_call(page_table, nsa_pool, qn, cw, a_mat, tn):
    B, n_pages = page_table.shape
    past = n_pages * PAGE_SIZE
    consts = [cw["w1k"], cw["w1v"], cw["posk"], cw["posv"], cw["w2k"], cw["w2v"], cw["bd"], cw["gk"], a_mat]
    cspecs = [pl.BlockSpec(a.shape, (lambda b, pt, nd=a.ndim: (0,) * nd)) for a in consts]
    nsl = a_mat.shape[1]
    grid_spec = pltpu.PrefetchScalarGridSpec(
        num_scalar_prefetch=1,
        grid=(B // 2,),
        in_specs=[pl.BlockSpec(memory_space=pl.ANY),
                  pl.BlockSpec((2 * tn, 512), lambda b, pt: (b, 0))] + cspecs,
        out_specs=[pl.BlockSpec((2, N_HEADS * tn, LANES), lambda b, pt: (b, 0, 0)),
                   pl.BlockSpec((2, N_GROUPS * tn, nsl), lambda b, pt: (b, 0, 0))],
        scratch_shapes=[pltpu.VMEM((2 * LANES, past), F32), pltpu.VMEM((2 * LANES, past), F32),
                        pltpu.VMEM((past, LANES), F32), pltpu.VMEM((past, LANES), F32),
                        pltpu.SemaphoreType.DMA((2,))],
    )
    return pl.pallas_call(
        functools.partial(_cmp_s_kernel, n_pages=n_pages, tn=tn),
        grid_spec=grid_spec,
        out_shape=[jax.ShapeDtypeStruct((B, N_HEADS * tn, LANES), F32),
                   jax.ShapeDtypeStruct((B, N_GROUPS * tn, nsl), F32)],
        compiler_params=_cparams(("arbitrary",)),
        name="compress_sample",
    )(page_table, nsa_pool, qn, *consts)


def _topk_s_kernel(imp_ref, o_ref, *, tn, past):
    rows = imp_ref.shape[0]
    t = past + lax.rem(lax.broadcasted_iota(jnp.int32, (rows, 1), 0), tn)
    sel = _select_blocks(imp_ref[...], t // SEL_BLOCK)
    o_ref[...] = ((sel - 1.0) * MASK_BIG).astype(BF16)


def _topk_s_call(imp2, tn, past):
    rows, nsl = imp2.shape
    tr = min(rows, 256)
    return pl.pallas_call(
        functools.partial(_topk_s_kernel, tn=tn, past=past),
        grid=(rows // tr,),
        in_specs=[pl.BlockSpec((tr, nsl), lambda i: (i, 0))],
        out_specs=pl.BlockSpec((tr, nsl), lambda i: (i, 0)),
        out_shape=jax.ShapeDtypeStruct((rows, nsl), BF16),
        compiler_params=_cparams(("arbitrary",)),
        name="select_sample",
    )(imp2)


def _sel_s_kernel(pt_ref, pool_ref, q_ref, nnew_ref, wnew_ref, win_ref, seln_ref, oc_ref, gate_ref, e_ref,
                  o_ref, wout_ref, kva_ref, kvb_ref, sem_ref, *, n_pages, tn, kc):
    past = n_pages * PAGE_SIZE
    R = N_HEADS * tn
    n_chunks = past // kc
    wlen = win_ref.shape[2]
    tcol = _tok_col(tn)
    slope = _slope_col(tn)
    tl = lax.broadcasted_iota(jnp.int32, (R, LANES), 1)
    newbias = jnp.where(tl <= tcol, -slope * (tcol - tl).astype(F32), NEG)

    def group_rows(x):
        n = x.shape[1]
        x4 = jnp.broadcast_to(x.reshape(N_GROUPS, 1, tn, n), (N_GROUPS, N_REP, tn, n))
        return x4.reshape(R, n)

    def compute(u, bufs):
        (kv_ref,) = bufs
        rows = pl.ds(u * tn, tn)
        qs = _stack_q(q_ref.at[rows], tn)

        seln_new = seln_ref[u, n_chunks]
        mb_new = group_rows(_dot(seln_new, e_ref[...])[:, 0:LANES])

        def past_bias(c):
            mb = group_rows(_dot(seln_ref[u, c], e_ref[...]))
            kpos = c * kc + lax.broadcasted_iota(jnp.int32, (1, kc), 1)
            return mb - slope * ((past + tcol) - kpos).astype(F32)

        tiles = [_new_key_tile(qs, nnew_ref.at[rows], 2 * LANES, jnp.where(tl <= tcol, newbias + mb_new, NEG))]
        o_s = _softmax_pv(tiles + _past_key_tiles(qs, kv_ref, kc, past_bias))

        kt = win_ref[u, 0:LANES, :].astype(BF16)
        vt = win_ref[u, LANES:2 * LANES, :].astype(BF16)
        wpos = (past - wlen) + lax.broadcasted_iota(jnp.int32, (1, wlen), 1)
        dw = (past + tcol) - wpos
        bias = jnp.where((dw < WINDOW) & (wpos >= 0), -slope * dw.astype(F32), NEG)
        o_w = _softmax_pv([_new_key_tile(qs, wnew_ref.at[rows], 0, newbias),
                           (_dot(qs, kt) + bias, lambda p: _dot_nt(p, vt))])

        wnew = wnew_ref.at[rows]
        new_t = jnp.concatenate([_pad_keys(wnew[:, 0:LANES], F32).T,
                                 _pad_keys(wnew[:, LANES:2 * LANES], F32).T], axis=0)
        new_t = pltpu.roll(new_t, LANES - tn, 1)
        shifted = pltpu.roll(win_ref[u], wlen - tn, 1)
        lane = lax.broadcasted_iota(jnp.int32, (2 * LANES, LANES), 1)
        wout_ref[u, :, 0:wlen - LANES] = shifted[:, 0:wlen - LANES]
        wout_ref[u, :, wlen - LANES:wlen] = jnp.where(lane >= LANES - tn, new_t, shifted[:, wlen - LANES:wlen])

        gc, gs, gw = _gate_cols(gate_ref.at[rows], tn)
        return _unstack_o(gc * oc_ref[u] + gs * o_s + gw * o_w, tn)

    outs = _two_sequence_pipeline(pt_ref, [(pool_ref, 2 * LANES, kva_ref, sem_ref.at[0])],
                                  [(pool_ref, 2 * LANES, kvb_ref, sem_ref.at[1])], n_pages, compute)
    o_ref[...] = jnp.concatenate(outs, axis=0).astype(BF16)


def _sel_s_call(page_table, nsa_pool, qn, nrow, wrow, win_t, seln, oc, gates, e_mat, tn, kc):
    B, n_pages = page_table.shape
    past = n_pages * PAGE_SIZE
    wlen = win_t.shape[2]
    grid_spec = pltpu.PrefetchScalarGridSpec(
        num_scalar_prefetch=1,
        grid=(B // 2,),
        in_specs=[pl.BlockSpec(memory_space=pl.ANY),
                  pl.BlockSpec((2 * tn, 512), lambda b, pt: (b, 0)),
                  pl.BlockSpec((2 * tn, 512), lambda b, pt: (b, 0)),
                  pl.BlockSpec((2 * tn, 256), lambda b, pt: (b, 0)),
                  pl.BlockSpec((2, 2 * LANES, wlen), lambda b, pt: (b, 0, 0)),
                  pl.BlockSpec((2,) + seln.shape[1:], lambda b, pt: (b, 0, 0, 0)),
                  pl.BlockSpec((2, N_HEADS * tn, LANES), lambda b, pt: (b, 0, 0)),
                  pl.BlockSpec((2 * tn, LANES), lambda b, pt: (b, 0)),
                  pl.BlockSpec(e_mat.shape, lambda b, pt: (0, 0))],
        out_specs=[pl.BlockSpec((2 * tn, 512), lambda b, pt: (b, 0)),
                   pl.BlockSpec((2, 2 * LANES, wlen), lambda b, pt: (b, 0, 0))],
        scratch_shapes=[pltpu.VMEM((2 * LANES, past), F32), pltpu.VMEM((2 * LANES, past), F32),
                        pltpu.SemaphoreType.DMA((2,))],
    )
    return pl.pallas_call(
        functools.partial(_sel_s_kernel, n_pages=n_pages, tn=tn, kc=kc),
        grid_spec=grid_spec,
        out_shape=[jax.ShapeDtypeStruct((B * tn, 512), BF16),
                   jax.ShapeDtypeStruct((B, 2 * LANES, wlen), F32)],
        compiler_params=_cparams(("arbitrary",)),
        name="select_attend_sample",
    )(page_table, nsa_pool, qn, nrow, wrow, win_t, seln, oc, gates, e_mat)


def _pair_cols(base):
    idx = []
    for r in range(N_REP):
        for g in range(N_GROUPS):
            h = g * N_REP + r
            idx.extend(range(base + h * HEAD_DIM, base + (h + 1) * HEAD_DIM))
    return np.asarray(idx, np.int32)


def _prep_weights(w_in, b_fox_f, fox_qn_g, fox_kn_g, nsa_qn_g, nsa_kn_slc_g, nsa_kn_win_g):
    o_fq, o_fk, o_fv, o_ff, o_nq, o_nkv, o_ng, o_mg = 0, 512, 640, 768, 776, 1288, 2056, 2080
    cols = np.concatenate([
        _pair_cols(o_fq), np.arange(o_fk, o_fk + 128), np.arange(o_fv, o_fv + 128),
        _pair_cols(o_nq), np.arange(o_nkv, o_nkv + 768),
        np.arange(o_ff, o_ff + 8), np.arange(o_ng, o_ng + 24)]).astype(np.int32)
    w_p = jnp.take(w_in, cols, axis=1)
    w_p = jnp.pad(w_p, ((0, 0), (0, _C_END - w_p.shape[1]))).astype(BF16)
    w_mg = w_in[:, o_mg:o_mg + 2 * D_MODEL].astype(BF16)
    tile2 = lambda g: jnp.tile(g, 2)
    gains = jnp.stack([tile2(fox_qn_g), tile2(fox_kn_g), tile2(nsa_qn_g), tile2(nsa_kn_slc_g),
                       tile2(nsa_kn_win_g)] + [jnp.zeros((LANES,), F32)] * 3)
    bff = jnp.zeros((1, LANES), F32).at[0, LOGF_LANE0:LOGF_LANE0 + N_HEADS].set(b_fox_f)
    return w_p, w_mg, gains, bff


def _prep_compress(pos, w1, w2):
    w1r = w1.reshape(2, CMP_STRIDE, HEAD_DIM, CMP_HIDDEN)
    z = jnp.zeros((CMP_STRIDE, HEAD_DIM, CMP_HIDDEN), F32)
    top = jnp.concatenate([w1r[0], z, w1r[1], z], axis=-1)
    bot = jnp.concatenate([z, w1r[0], z, w1r[1]], axis=-1)
    w1b = jnp.concatenate([top, bot], axis=1).astype(BF16)
    w1b = w1b.reshape(CMP_STRIDE * LANES, 4 * LANES)
    pr = pos.reshape(2, CMP_STRIDE, HEAD_DIM)
    posb = jnp.concatenate([jnp.tile(pr, (1, 1, 2)).transpose(1, 0, 2),
                            jnp.zeros((CMP_STRIDE, 6, LANES), F32)], axis=1).astype(BF16)
    posb = posb.transpose(1, 0, 2).reshape(8, CMP_STRIDE * LANES)
    zz = jnp.zeros((CMP_HIDDEN, HEAD_DIM), F32)
    w2b = jnp.concatenate([jnp.concatenate([w2, zz], axis=1),
                           jnp.concatenate([zz, w2], axis=1)], axis=0).astype(BF16)
    return w1b, posb, w2b


def kernel(x_prompt, x_sample, cache_fox_kv, cache_fox_logf, cache_nsa_kv, state_win_kv, page_table,
           c_prompt, c_sample, norm1_g, norm2_g, w_ada, b_ada, w_in, b_fox_f, fox_qn_g, fox_kn_g,
           nsa_qn_g, nsa_kn_cmp_g, nsa_kn_slc_g, nsa_kn_win_g, cmp_pos_k, cmp_w1_k, cmp_w2_k,
           cmp_pos_v, cmp_w1_v, cmp_w2_v, w_br_fox, w_br_nsa, w_out, w_up, w_down):
    assert norm1_g.shape[0] == 1
    B, T, d = x_prompt.shape
    DB, TN, _ = x_sample.shape
    n_phys = cache_fox_kv.shape[1]
    n_pages = page_table.shape[1]
    past = n_pages * PAGE_SIZE
    wlen = state_win_kv.shape[2]
    assert T % 256 == 0 and T >= WINDOW and wlen == WINDOW and TN == 8 and past % CMP_STRIDE == 0

    w_p, w_mg, gains, bff = _prep_weights(w_in[0], b_fox_f[0], fox_qn_g[0], fox_kn_g[0], nsa_qn_g[0],
                                          nsa_kn_slc_g[0], nsa_kn_win_g[0])
    w1k, posk, w2k = _prep_compress(cmp_pos_k[0], cmp_w1_k[0], cmp_w2_k[0])
    w1v, posv, w2v = _prep_compress(cmp_pos_v[0], cmp_w1_v[0], cmp_w2_v[0])
    bd = jnp.asarray(np.kron(np.eye(2), np.ones((HEAD_DIM, HEAD_DIM))), BF16)
    bd2 = jnp.asarray(np.kron(np.eye(4), np.ones((HEAD_DIM, HEAD_DIM))), BF16)
    cw = dict(w1k=w1k, w1v=w1v, posk=posk, posv=posv, w2k=w2k, w2v=w2v, bd=bd,
              gk=jnp.tile(nsa_kn_cmp_g[0], 2).reshape(1, LANES))
    pair_rows = _pair_cols(0)
    wbf = jnp.take(w_br_fox[0], pair_rows, axis=0).astype(BF16)
    wbn = jnp.take(w_br_nsa[0], pair_rows, axis=0).astype(BF16)
    wout = w_out[0].astype(BF16)
    wup = w_up[0].astype(BF16)
    wdn = w_down[0].astype(BF16)
    g1 = norm1_g[0].reshape(1, d)
    g2 = norm2_g[0].reshape(1, d)
    tm_p = 512
    tri = jnp.asarray(np.tril(np.ones((tm_p, tm_p), np.float32)), BF16)

    mod = _ada_call(jnp.concatenate([c_prompt, c_sample], axis=0), w_ada[0], b_ada[0])
    mod_p = mod[:B].reshape(B, 1, 6 * d)
    mod_s = mod[B:].reshape(DB, 1, 6 * d)

    (qf, frow_t, qn, nrow_t, wrow_t, gates, lf, c_tm, kf, vf, ks, vs, kw, vw, nraw) = _pre_call(
        x_prompt, mod_p, g1, w_p, bd2, tri, gains, bff, nb=1, tt=tm_p, do_cum=True)
    tq, tk, tw = 128, 512, 256
    o_fox = _fox_p_call(qf, kf, vf, c_tm, B, T, tq, tk)
    kc_p, vc_p = _cmp_p_call(nraw, cw, B, T)
    nc_p = T // CMP_STRIDE
    nsel_p = -(-T // SEL_BLOCK)
    assert nsel_p <= LANES - _AUG_SEL0 and T % tk == 0
    a_p = jnp.asarray(_importance_matrix(nc_p, LANES, nc_p - 1)[:, :LANES] *
                      (np.arange(LANES) < nsel_p)[None, :], BF16)
    o_nsa = _nsa_p_call(qn, ks, vs, kw, vw, kc_p, vc_p, gates, a_p.T, B, T, tq, tk, tw)
    x1 = _mix_call(x_prompt, o_fox, o_nsa, mod_p, g1, w_mg, wbf, wbn, wout, nb=1, tt=tm_p)
    y_prompt = _ffn_call(x1, mod_p, g2, wup, wdn, nb=1, tt=tm_p)

    nb_s = min(DB, 32)
    (qf_s, frow_s, qn_s, nrow_s, wrow_s, gates_s, lf_s) = _pre_call(
        x_sample, mod_s, g1, w_p, bd2, tri, gains, bff, nb=nb_s, tt=TN, do_cum=False)
    qf_s = qf_s.astype(F32)
    qn_s = qn_s.astype(F32)

    fox_t = jnp.transpose(cache_fox_kv[0], (0, 2, 3, 4, 1)).reshape(n_phys, 2 * LANES, PAGE_SIZE)
    nsa_t = jnp.transpose(cache_nsa_kv[0], (0, 2, 3, 4, 1)).reshape(n_phys, 4 * LANES, PAGE_SIZE)
    logf_t = jnp.transpose(cache_fox_logf[0], (0, 2, 1))
    win_t = jnp.transpose(state_win_kv[0], (0, 2, 3, 4, 1)).reshape(DB, 2 * LANES, wlen)

    lf_new = lf_s[:, LOGF_LANE0:LOGF_LANE0 + N_HEADS].reshape(DB, TN, N_HEADS).transpose(0, 2, 1)
    lf_new = jnp.pad(lf_new, ((0, 0), (0, 0), (0, LANES - TN)))
    o_fox_s = _fox_s_call(page_table, fox_t, logf_t, qf_s, frow_s, lf_new, TN)

    nsa_pool = nsa_t
    nc_s = past // CMP_STRIDE
    nsel_s = -(-(past + TN) // SEL_BLOCK)
    nsl = -(-nsel_s // LANES) * LANES
    a_s = jnp.asarray(_importance_matrix(nc_s, nsl, nc_s - 1) * (np.arange(nsl) < nsel_s)[None, :], BF16)
    oc_s, imp_s = _cmp_s_call(page_table, nsa_pool, qn_s, cw, a_s, TN)
    seln = _topk_s_call(imp_s.reshape(DB * N_GROUPS * TN, nsl), TN, past)
    kc_keys = min(past, 2048)
    bpc = kc_keys // SEL_BLOCK
    seln = seln.reshape(DB, N_GROUPS * TN, nsl // bpc, bpc).transpose(0, 2, 1, 3)
    e_s = jnp.asarray((np.arange(kc_keys)[None, :] // SEL_BLOCK) == np.arange(bpc)[:, None], BF16)
    o_nsa_s, win_new_t = _sel_s_call(page_table, nsa_pool, qn_s, nrow_s, wrow_s, win_t, seln, oc_s, gates_s,
                                     e_s, TN, kc_keys)
    x1_s = _mix_call(x_sample, o_fox_s, o_nsa_s, mod_s, g1, w_mg, wbf, wbn, wout, nb=nb_s, tt=TN)
    y_sample = _ffn_call(x1_s, mod_s, g2, wup, wdn, nb=nb_s, tt=TN)

    def token_major(rows_t, n_slots):
        nbat, _, toks = rows_t.shape
        return rows_t.reshape(nbat, n_slots, N_GROUPS, HEAD_DIM, toks).transpose(0, 4, 1, 2, 3)

    lf_p = lf[:, LOGF_LANE0:LOGF_LANE0 + N_HEADS]
    return (y_prompt, y_sample,
            token_major(frow_t, 2)[None],
            lf_p.reshape(1, B, T, N_HEADS),
            token_major(nrow_t, 4)[None],
            token_major(wrow_t[:, :, T - wlen:], 2)[None],
            frow_s.reshape(1, DB, TN, 2, N_GROUPS, HEAD_DIM),
            lf_s[:, LOGF_LANE0:LOGF_LANE0 + N_HEADS].reshape(1, DB, TN, N_HEADS),
            nrow_s.reshape(1, DB, TN, 4, N_GROUPS, HEAD_DIM),
            token_major(win_new_t, 2)[None])
```

```python
import functools

import numpy as np
import jax
import jax.numpy as jnp
from jax import lax
from jax.experimental import pallas as pl
from jax.experimental.pallas import tpu as pltpu

F32 = jnp.float32
BF16 = jnp.bfloat16

D_MODEL = 1024
HEAD_DIM = 64
N_HEADS = 8
N_GROUPS = 2
N_REP = N_HEADS // N_GROUPS
PAGE_SIZE = 128
CMP_STRIDE = 16
CMP_HIDDEN = 2 * HEAD_DIM
SEL_BLOCK = 64
N_SELECT = 16
WINDOW = 512
D_FF = 4 * D_MODEL
RMS_EPS = 1e-6
FORCE_BONUS = 1.0e4
LANES = 128
NEG = -1e30
MASK_BIG = 2.0 ** 100
LOGF_LANE0 = 0
GATE_LANE0 = 8
VMEM_LIMIT = 56 * 1024 * 1024


def _dot(a, b):
    return jnp.dot(a, b, preferred_element_type=F32)


def _dot_nt(a, b):
    return lax.dot_general(a, b, (((1,), (1,)), ((), ())), preferred_element_type=F32)


def _split3(x):
    x1 = x.astype(BF16)
    r = x - x1.astype(F32)
    x2 = r.astype(BF16)
    x3 = (r - x2.astype(F32)).astype(BF16)
    return x1, x2, x3


def _dot_exact_r(x, m):
    a, b, c = _split3(x)
    return _dot(a, m) + _dot(b, m) + _dot(c, m)


def _dot_exact_l(m, x):
    a, b, c = _split3(x)
    return _dot(m, a) + _dot(m, b) + _dot(m, c)


def _sigmoid(x):
    return 1.0 / (1.0 + jnp.exp(-x))


def _head_rms(zc, bd):
    a = zc * zc
    a1 = a.astype(BF16)
    a2 = (a - a1.astype(F32)).astype(BF16)
    ss = _dot(a1, bd) + _dot(a2, bd)
    return zc * lax.rsqrt(ss * (1.0 / HEAD_DIM) + RMS_EPS)


def _cparams(sem, vmem=VMEM_LIMIT):
    return pltpu.CompilerParams(dimension_semantics=sem, vmem_limit_bytes=vmem)


def _ada_kernel(c_ref, w_ref, b_ref, o_ref):
    c = c_ref[...]
    a = c * _sigmoid(c)
    o_ref[...] = _dot(a.astype(BF16), w_ref[...].astype(BF16)) + b_ref[...]


def _ada_call(c, w_ada, b_ada):
    r, d = c.shape
    n = w_ada.shape[1]
    tn = 1024
    return pl.pallas_call(
        _ada_kernel,
        grid=(n // tn,),
        in_specs=[pl.BlockSpec((r, d), lambda j: (0, 0)),
                  pl.BlockSpec((d, tn), lambda j: (0, j)),
                  pl.BlockSpec((1, tn), lambda j: (0, j))],
        out_specs=pl.BlockSpec((r, tn), lambda j: (0, j)),
        out_shape=jax.ShapeDtypeStruct((r, n), F32),
        compiler_params=_cparams(("arbitrary",)),
        name="ada_mod",
    )(c, w_ada, b_ada.reshape(1, n))


_C_FQ, _C_FK, _C_FV, _C_NQ, _C_NKV, _C_SMALL, _C_END = 0, 512, 640, 768, 1280, 2048, 2176


def _v_with_ones(v):
    lane = lax.broadcasted_iota(jnp.int32, v.shape, 1)
    return jnp.concatenate([jnp.where(lane < HEAD_DIM, v, 1.0), jnp.where(lane < HEAD_DIM, 1.0, v)],
                           axis=1).astype(BF16)


def _pos_hi_lo(pos):
    return (pos >> 8).astype(F32), (pos & 255).astype(F32)


_AUG_SEL0 = 4


def _key_pos_features(pos, lane):
    hi, lo = _pos_hi_lo(pos)
    return jnp.where(lane < 2, 1.0,
                     jnp.where(lane == 2, hi,
                               jnp.where(lane == 3, lo,
                                         jnp.where(lane - _AUG_SEL0 == pos // SEL_BLOCK, 1.0, 0.0))))


def _query_pos_features(pos, slope, lane):
    hi, lo = _pos_hi_lo(pos)
    return jnp.where(lane == 0, -(slope * 256.0) * hi,
                     jnp.where(lane == 1, -slope * lo,
                               jnp.where(lane == 2, slope * 256.0,
                                         jnp.where(lane == 3, slope, 0.0))))


def _pre_kernel(x_ref, sc_ref, sh_ref, g1_ref, w_ref, bd_ref, tri_ref, gains_ref, bff_ref,
                qf_ref, frow_ref, qn_ref, nrow_ref, wrow_ref, gate_ref, lf_ref,
                *rest, do_cum):
    nb, tt, d = x_ref.shape
    tm = nb * tt
    x = x_ref[...]
    ms = jnp.mean(x * x, axis=-1, keepdims=True)
    h = x * lax.rsqrt(ms + RMS_EPS) * g1_ref[...] * (1.0 + sc_ref[...]) + sh_ref[...]
    h = h.reshape(tm, d).astype(BF16)
    bd2 = bd_ref[...]

    def proj(c0, n_chunks=1):
        z = _dot(h, w_ref[:, c0:c0 + n_chunks * LANES])
        return [z[:, i * LANES:(i + 1) * LANES] for i in range(n_chunks)]

    def rms_pair(za, zb):
        n = _head_rms(jnp.concatenate([za, zb], axis=1), bd2)
        return n[:, 0:LANES], n[:, LANES:2 * LANES]

    scale = HEAD_DIM ** -0.5
    for q_ref, c0, grow in ((qf_ref, _C_FQ, 0), (qn_ref, _C_NQ, 2)):
        z = proj(c0, N_REP)
        for r, zc in enumerate(rms_pair(z[0], z[1]) + rms_pair(z[2], z[3])):
            q_ref[:, r * LANES:(r + 1) * LANES] = (zc * (gains_ref[grow:grow + 1, :] * scale)).astype(BF16)

    def put(ref, i, val):
        if do_cum:
            ref[0, i * LANES:(i + 1) * LANES, :] = val.T
        else:
            ref[:, i * LANES:(i + 1) * LANES] = val

    fk, fv = proj(_C_FK, 2)
    ck, cv = proj(_C_NKV, 2)
    sk, sv = proj(_C_NKV + 2 * LANES, 2)
    wk, wv = proj(_C_NKV + 4 * LANES, 2)
    fk, sk = rms_pair(fk, sk)
    fk = fk * gains_ref[1:2, :]
    sk = sk * gains_ref[3:4, :]
    wk = _head_rms(wk, bd2[0:LANES, 0:LANES]) * gains_ref[4:5, :]
    put(frow_ref, 0, fk)
    put(frow_ref, 1, fv)
    for i, val in enumerate((ck, cv, sk, sv)):
        put(nrow_ref, i, val)
    put(wrow_ref, 0, wk)
    put(wrow_ref, 1, wv)

    (zl,) = proj(_C_SMALL)
    gate_ref[...] = _sigmoid(zl)
    xl = zl + bff_ref[...]
    lf = jnp.minimum(xl, 0.0) - jnp.log1p(jnp.exp(-jnp.abs(xl)))
    lf_ref[...] = lf

    if do_cum:
        c_ref, kf_ref, vf_ref, ks_ref, vs_ref, kw_ref, vw_ref, nraw_ref, carry_ref = rest
        nraw_ref[:, 0:LANES] = ck
        nraw_ref[:, LANES:2 * LANES] = cv

        @pl.when(pl.program_id(1) == 0)
        def _():
            carry_ref[...] = jnp.zeros_like(carry_ref)

        c = _dot_exact_l(tri_ref[...], lf) + carry_ref[0:1, :]
        c_ref[...] = c
        carry_ref[...] = jnp.broadcast_to(c[tm - 1:tm, :], carry_ref.shape)

        lane = lax.broadcasted_iota(jnp.int32, (tm, LANES), 1)
        c1, c2, c3 = (p.astype(F32) for p in _split3(c))
        aug_f = jnp.where(lane < N_HEADS, -c1,
                          jnp.where(lane < 2 * N_HEADS, -pltpu.roll(c2, N_HEADS, 1),
                                    jnp.where(lane < 3 * N_HEADS, -pltpu.roll(c3, 2 * N_HEADS, 1),
                                              jnp.where(lane < 3 * N_HEADS + 3, 1.0, 0.0))))
        kf_ref[:, 0:LANES] = fk.astype(BF16)
        kf_ref[:, LANES:2 * LANES] = aug_f.astype(BF16)
        vf_ref[...] = _v_with_ones(fv)

        pos = pl.program_id(1) * tm + lax.broadcasted_iota(jnp.int32, (tm, 1), 0)
        aug_p = _key_pos_features(pos, lane).astype(BF16)
        ks_ref[:, 0:LANES] = sk.astype(BF16)
        ks_ref[:, LANES:2 * LANES] = aug_p
        vs_ref[...] = _v_with_ones(sv)
        kw_ref[:, 0:LANES] = wk.astype(BF16)
        kw_ref[:, LANES:2 * LANES] = aug_p
        vw_ref[...] = _v_with_ones(wv)


def _pre_call(x3, mod3, g1, w_p, bd, tri, gains, bff, nb, tt, do_cum):
    NB, TT, d = x3.shape
    tm = nb * tt
    n = NB * TT
    gi, gj = NB // nb, TT // tt
    tok = lambda i, j: (i * gj + j, 0)
    full = lambda i, j: (0, 0)
    in_specs = [
        pl.BlockSpec((nb, tt, d), lambda i, j: (i, j, 0)),
        pl.BlockSpec((nb, 1, d), lambda i, j: (i, 0, 1)),
        pl.BlockSpec((nb, 1, d), lambda i, j: (i, 0, 0)),
        pl.BlockSpec((1, d), full),
        pl.BlockSpec(w_p.shape, full),
        pl.BlockSpec(bd.shape, full),
        pl.BlockSpec(tri.shape, full),
        pl.BlockSpec(gains.shape, full),
        pl.BlockSpec(bff.shape, full),
    ]
    def rows_out(width):
        if do_cum:
            return (jax.ShapeDtypeStruct((NB, width, TT), F32),
                    pl.BlockSpec((1, width, tm), lambda i, j: (i, 0, j)))
        return jax.ShapeDtypeStruct((n, width), F32), pl.BlockSpec((tm, width), tok)

    def tok_out(width, dtype):
        return jax.ShapeDtypeStruct((n, width), dtype), pl.BlockSpec((tm, width), tok)

    outs = [
        tok_out(512, BF16),
        rows_out(256),
        tok_out(512, BF16),
        rows_out(512),
        rows_out(256),
        tok_out(LANES, F32),
        tok_out(LANES, F32),
    ]
    scratch = []
    if do_cum:
        assert nb == 1
        outs += [tok_out(LANES, F32)]
        outs += [tok_out(256, BF16)] * 6
        outs += [tok_out(256, F32)]
        scratch = [pltpu.VMEM((8, LANES), F32)]
    out_shape = [o[0] for o in outs]
    out_specs = [o[1] for o in outs]
    return pl.pallas_call(
        functools.partial(_pre_kernel, do_cum=do_cum),
        grid=(gi, gj),
        in_specs=in_specs,
        out_specs=out_specs,
        out_shape=out_shape,
        scratch_shapes=scratch,
        compiler_params=_cparams(("arbitrary", "arbitrary")),
        name="pre_mixer",
    )(x3, mod3, mod3, g1, w_p, bd, tri, gains, bff)


def _half_mask(rows, g):
    lane = lax.broadcasted_iota(jnp.int32, (rows, LANES), 1)
    return (lane < HEAD_DIM) if g == 0 else (lane >= HEAD_DIM)


def _slope(h):
    return 2.0 ** (-8.0 * (h + 1) / N_HEADS)


def _select_blocks(imp, cur):
    blk = lax.broadcasted_iota(jnp.int32, imp.shape, 1)
    forced = (blk == 0) | (blk == cur) | (blk == cur - 1)
    score = jnp.where(blk <= cur, imp + jnp.where(forced, FORCE_BONUS, 0.0), -jnp.inf)
    blkf = blk.astype(F32)

    def body(_, carry):
        sc, sel = carry
        mx = jnp.max(sc, axis=-1, keepdims=True)
        idx = jnp.min(jnp.where(sc == mx, blkf, 1e9), axis=-1, keepdims=True)
        pick = blkf == idx
        return jnp.where(pick, -jnp.inf, sc), jnp.where(pick, 1.0, sel)

    _, sel = lax.fori_loop(0, N_SELECT, body, (score, jnp.zeros(imp.shape, F32)))
    return jnp.where(blk <= cur, sel, 0.0)


def _select_blocks_by_rank(imp_t, cur, n_blocks):
    blk = lax.broadcasted_iota(jnp.int32, imp_t.shape, 0)
    forced = (blk == 0) | (blk == cur) | (blk == cur - 1)
    score = jnp.where(blk <= cur, imp_t + jnp.where(forced, FORCE_BONUS, 0.0), -jnp.inf)
    beaten_by = jnp.zeros(imp_t.shape, F32)
    for i in range(n_blocks):
        si = score[i:i + 1, :]
        beaten_by = beaten_by + jnp.where((si > score) | ((si == score) & (blk > i)), 1.0, 0.0)
    return jnp.where((beaten_by < N_SELECT) & (blk <= cur), 1.0, 0.0)


def _importance_matrix(n_cmp_rows, n_sel_cols, n_cmp):
    r = SEL_BLOCK // CMP_STRIDE
    a = np.zeros((n_cmp_rows, n_sel_cols), np.float32)
    for n in range(n_cmp):
        for j in range(n_sel_cols):
            off = n - r * j
            if off in (-1, r - 1):
                a[n, j] = 0.5
            elif 0 <= off <= r - 2:
                a[n, j] = 1.0
    return a


def _stack_q_features(q_ref, qa_ref, tq, feat_fn):
    for g in range(N_GROUPS):
        for r in range(N_REP):
            h = g * N_REP + r
            qc = q_ref[:, r * LANES:(r + 1) * LANES]
            qa_ref[h * tq:(h + 1) * tq, 0:LANES] = jnp.where(_half_mask(tq, g), qc, jnp.zeros_like(qc))
            qa_ref[h * tq:(h + 1) * tq, LANES:2 * LANES] = feat_fn(h).astype(BF16)


def _stacked_attend(qa, k_ref, v_ref, tk, first, lo, mask_first, mask_rest):
    half = qa.shape[0] // 2

    def probabilities(j, mask, m_old):
        st = pl.multiple_of(j * tk, tk)
        s = mask(_dot_nt(qa, k_ref[pl.ds(st, tk), :]))
        m_new = jnp.max(s, axis=-1, keepdims=True)
        if m_old is not None:
            m_new = jnp.maximum(m_old, m_new)
        p = jnp.exp(s - m_new).astype(BF16)
        va = v_ref[pl.ds(st, tk), :]
        pv = jnp.concatenate([_dot(p[:half], va[:, 0:LANES]), _dot(p[half:], va[:, LANES:2 * LANES])], axis=0)
        return m_new, pv

    def body(j, carry):
        m_old, acc = carry
        m_new, pv = probabilities(j, lambda s: mask_rest(j, s), m_old)
        return m_new, jnp.exp(m_old - m_new) * acc + pv

    m, acc = lax.fori_loop(lo, first, body, probabilities(first, mask_first, None))
    return acc / pltpu.roll(acc, HEAD_DIM, 1)


def _row_pos(t0, tq):
    return t0 + lax.rem(lax.broadcasted_iota(jnp.int32, (N_HEADS * tq, 1), 0), tq)


def _fox_p_kernel(q_ref, ka_ref, va_ref, c_ref, o_ref, qa_ref, *, tq, tk):
    qi = pl.program_id(1)
    t0 = qi * tq
    R = N_HEADS * tq
    lane = lax.broadcasted_iota(jnp.int32, (tq, LANES), 1)
    c1, c2, c3 = (p.astype(F32) for p in _split3(c_ref[...]))

    def q_features(h):
        own = (lane == h) | (lane == N_HEADS + h) | (lane == 2 * N_HEADS + h)
        return jnp.where(lane == 3 * N_HEADS, c1[:, h:h + 1],
                         jnp.where(lane == 3 * N_HEADS + 1, c2[:, h:h + 1],
                                   jnp.where(lane == 3 * N_HEADS + 2, c3[:, h:h + 1],
                                             jnp.where(own, 1.0, 0.0))))

    _stack_q_features(q_ref, qa_ref, tq, q_features)
    qa = qa_ref[...]
    tpos = _row_pos(t0, tq)

    jd = t0 // tk
    kpos = jd * tk + lax.broadcasted_iota(jnp.int32, (1, tk), 1)
    o = _stacked_attend(qa, ka_ref, va_ref, tk, jd, 0,
                        lambda s: jnp.where(kpos <= tpos, s, NEG), lambda j, s: s)
    o_ref[...] = _unstack_o(o, tq).astype(BF16)


def _fox_p_call(qf, ka, va, c_tm, B, T, tq, tk):
    n = B * T
    nq = T // tq
    return pl.pallas_call(
        functools.partial(_fox_p_kernel, tq=tq, tk=tk),
        grid=(B, nq),
        in_specs=[pl.BlockSpec((tq, 512), lambda b, i: (b * nq + i, 0)),
                  pl.BlockSpec((T, 256), lambda b, i: (b, 0)),
                  pl.BlockSpec((T, 256), lambda b, i: (b, 0)),
                  pl.BlockSpec((tq, LANES), lambda b, i: (b * nq + i, 0))],
        out_specs=pl.BlockSpec((tq, 512), lambda b, i: (b * nq + i, 0)),
        out_shape=jax.ShapeDtypeStruct((n, 512), BF16),
        scratch_shapes=[pltpu.VMEM((N_HEADS * tq, 2 * LANES), BF16)],
        compiler_params=_cparams(("arbitrary", "arbitrary")),
        name="fox_prompt",
    )(qf, ka, va, c_tm)


def _chunk_dot(x_ref, chunk0, n, w1_ref):
    rows = jnp.concatenate(
        [x_ref[pl.ds(chunk0 * CMP_STRIDE + c, n, stride=CMP_STRIDE), :].astype(BF16)
         for c in range(CMP_STRIDE)], axis=1)
    return _dot(rows, w1_ref[...])


def _compress(xk_ref, xv_ref, n, w1k_ref, w1v_ref, posk_ref, posv_ref, w2k_ref, w2v_ref, bd, gk):
    return _compress_finish(_chunk_dot(xk_ref, 0, n, w1k_ref), _chunk_dot(xv_ref, 0, n, w1v_ref), n,
                            w1k_ref, w1v_ref, posk_ref, posv_ref, w2k_ref, w2v_ref, bd, gk)


def _compress_finish(acc_k, acc_v, n, w1k_ref, w1v_ref, posk_ref, posv_ref, w2k_ref, w2v_ref, bd, gk):
    ck = _dot(posk_ref[...], w1k_ref[...])
    cv = _dot(posv_ref[...], w1v_ref[...])

    def finish(acc, cst, w2_ref):
        lead = acc[:, 0:2 * LANES] + cst[0:1, 0:2 * LANES]
        trail = acc[:, 2 * LANES:4 * LANES] + cst[1:2, 2 * LANES:4 * LANES]
        hid = lead + pltpu.roll(trail, n - 1, 0)
        act = hid * _sigmoid(hid)
        return _dot(act.astype(BF16), w2_ref[...])

    kc = _head_rms(finish(acc_k, ck, w2k_ref), bd) * gk
    vc = finish(acc_v, cv, w2v_ref)
    return kc, vc


def _cmp_p_kernel(xk_ref, xv_ref, w1k_ref, w1v_ref, posk_ref, posv_ref, w2k_ref, w2v_ref, bd_ref, gk_ref,
                  kc_ref, vc_ref):
    n = kc_ref.shape[0]
    kc, vc = _compress(xk_ref, xv_ref, n, w1k_ref, w1v_ref, posk_ref, posv_ref, w2k_ref, w2v_ref,
                       bd_ref[...], gk_ref[...])
    kc_ref[:, 0:LANES] = kc.astype(BF16)
    c_end = lax.broadcasted_iota(jnp.int32, (n, 1), 0) * CMP_STRIDE + (2 * CMP_STRIDE - 1)
    lane = lax.broadcasted_iota(jnp.int32, (n, LANES), 1)
    kc_ref[:, LANES:2 * LANES] = _key_pos_features(c_end, lane).astype(BF16)
    vc_ref[...] = vc.astype(BF16)


def _cmp_p_call(nrow, cw, B, T):
    nc = T // CMP_STRIDE
    consts = [cw["w1k"], cw["w1v"], cw["posk"], cw["posv"], cw["w2k"], cw["w2v"], cw["bd"], cw["gk"]]
    cspecs = [pl.BlockSpec(a.shape, (lambda b, nd=a.ndim: (0,) * nd)) for a in consts]
    return pl.pallas_call(
        _cmp_p_kernel,
        grid=(B,),
        in_specs=[pl.BlockSpec((T, LANES), lambda b: (b, 0)),
                  pl.BlockSpec((T, LANES), lambda b: (b, 1))] + cspecs,
        out_specs=[pl.BlockSpec((nc, 2 * LANES), lambda b: (b, 0)),
                   pl.BlockSpec((nc, LANES), lambda b: (b, 0))],
        out_shape=[jax.ShapeDtypeStruct((B * nc, 2 * LANES), BF16),
                   jax.ShapeDtypeStruct((B * nc, LANES), BF16)],
        compiler_params=_cparams(("arbitrary",)),
        name="compress_prompt",
    )(nrow, nrow, *consts)


def _gate_cols(gate_ref, tn):
    R = N_HEADS * tn
    gl = lax.broadcasted_iota(jnp.int32, (R, LANES), 1)
    hrow = lax.broadcasted_iota(jnp.int32, (R, LANES), 0) // tn
    gt = jnp.broadcast_to(gate_ref[...][None], (N_HEADS, tn, LANES)).reshape(R, LANES)
    return [jnp.sum(jnp.where(gl == GATE_LANE0 + 3 * hrow + j, gt, 0.0), axis=-1, keepdims=True)
            for j in range(3)]


def _nsa_p_kernel(q_ref, ks_ref, vs_ref, kw_ref, vw_ref, kc_ref, vc_ref, gate_ref, at_ref, o_ref, qa_ref,
                  *, tq, tk, tw, n_sel):
    qi = pl.program_id(1)
    t0 = qi * tq
    R = N_HEADS * tq
    nc = kc_ref.shape[0]
    lane = lax.broadcasted_iota(jnp.int32, (tq, LANES), 1)
    pos_q = t0 + lax.broadcasted_iota(jnp.int32, (tq, 1), 0)
    _stack_q_features(q_ref, qa_ref, tq, lambda h: _query_pos_features(pos_q, _slope(h), lane))
    qa = qa_ref[...]
    tpos = _row_pos(t0, tq)

    nid = lax.broadcasted_iota(jnp.int32, (1, nc), 1)
    cvalid = (nid * CMP_STRIDE + (2 * CMP_STRIDE - 1) <= tpos) & (nid < nc - 1)
    s = jnp.where(cvalid, _dot_nt(qa, kc_ref[...]), NEG)
    mx = jnp.max(s, axis=-1, keepdims=True)
    p = jnp.where(cvalid, jnp.exp(s - mx), 0.0)
    den = jnp.sum(p, axis=-1, keepdims=True)
    p = p / jnp.where(den > 0, den, 1.0)
    o_c = _dot(p.astype(BF16), vc_ref[...])
    p4 = p.reshape(N_GROUPS, N_REP, tq, nc)
    psum = (p4[:, 0] + p4[:, 1] + p4[:, 2] + p4[:, 3]).reshape(N_GROUPS * tq, nc)

    assert tw + tq <= WINDOW

    def key_pos(j):
        return j * tw + lax.broadcasted_iota(jnp.int32, (1, tw), 1)

    jw = t0 // tw
    o_w = _stacked_attend(qa, kw_ref, vw_ref, tw, jw, jnp.maximum(t0 - (WINDOW - 1), 0) // tw,
                          lambda s: jnp.where(key_pos(jw) <= tpos, s, NEG),
                          lambda j, s: jnp.where(tpos - key_pos(j) < WINDOW, s, NEG))

    p1, p2, p3 = _split3(psum)
    at = at_ref[...]
    imp_t = _dot_nt(at, p1) + _dot_nt(at, p2) + _dot_nt(at, p3)
    n_rows = -(-n_sel // 8) * 8
    cur = (t0 + lax.rem(lax.broadcasted_iota(jnp.int32, (1, N_GROUPS * tq), 1), tq)) // SEL_BLOCK
    sel_t = _select_blocks_by_rank(imp_t[0:n_rows], cur, n_sel)
    sel_t = jnp.concatenate([sel_t, jnp.zeros((LANES - n_rows, N_GROUPS * tq), F32)], axis=0)
    seln = pltpu.roll(((sel_t - 1.0) * MASK_BIG).T, _AUG_SEL0, 1)
    for g in range(N_GROUPS):
        for r in range(N_REP):
            h = g * N_REP + r
            feat = jnp.where(lane >= _AUG_SEL0, seln[g * tq:(g + 1) * tq],
                             _query_pos_features(pos_q, _slope(h), lane))
            qa_ref[h * tq:(h + 1) * tq, LANES:2 * LANES] = feat.astype(BF16)
    qa = qa_ref[...]

    js = t0 // tk
    kpos = js * tk + lax.broadcasted_iota(jnp.int32, (1, tk), 1)
    o_s = _stacked_attend(qa, ks_ref, vs_ref, tk, js, 0,
                          lambda s: jnp.where(kpos <= tpos, s, NEG), lambda j, s: s)

    gc, gs, gw = _gate_cols(gate_ref, tq)
    o_ref[...] = _unstack_o(gc * o_c + gs * o_s + gw * o_w, tq).astype(BF16)


def _nsa_p_call(qn, ks, vs, kw, vw, kc, vc, gates, a_mat, B, T, tq, tk, tw):
    n = B * T
    nq = T // tq
    nc = T // CMP_STRIDE
    seq = lambda b, i: (b, 0)
    blk = lambda b, i: (b * nq + i, 0)
    return pl.pallas_call(
        functools.partial(_nsa_p_kernel, tq=tq, tk=tk, tw=tw, n_sel=-(-T // SEL_BLOCK)),
        grid=(B, nq),
        in_specs=[pl.BlockSpec((tq, 512), blk),
                  pl.BlockSpec((T, 256), seq), pl.BlockSpec((T, 256), seq),
                  pl.BlockSpec((T, 256), seq), pl.BlockSpec((T, 256), seq),
                  pl.BlockSpec((nc, 2 * LANES), seq),
                  pl.BlockSpec((nc, LANES), seq),
                  pl.BlockSpec((tq, LANES), blk),
                  pl.BlockSpec(a_mat.shape, lambda b, i: (0, 0))],
        out_specs=pl.BlockSpec((tq, 512), blk),
        out_shape=jax.ShapeDtypeStruct((n, 512), BF16),
        scratch_shapes=[pltpu.VMEM((N_HEADS * tq, 2 * LANES), BF16)],
        compiler_params=_cparams(("arbitrary", "arbitrary")),
        name="nsa_prompt",
    )(qn, ks, vs, kw, vw, kc, vc, gates, a_mat)


def _mix_kernel(x_ref, of_ref, on_ref, sc_ref, sh_ref, gt_ref, g1_ref, wmg_ref, wbf_ref, wbn_ref,
                wout_ref, o_ref):
    nb, tt, d = x_ref.shape
    tm = nb * tt
    x = x_ref[...]
    ms = jnp.mean(x * x, axis=-1, keepdims=True)
    h = x * lax.rsqrt(ms + RMS_EPS) * g1_ref[...] * (1.0 + sc_ref[...]) + sh_ref[...]
    h = h.reshape(tm, d).astype(BF16)
    g_fox = _sigmoid(_dot(h, wmg_ref[:, 0:d]))
    g_nsa = _sigmoid(_dot(h, wmg_ref[:, d:2 * d]))
    mix = g_fox * _dot(of_ref[...], wbf_ref[...]) + g_nsa * _dot(on_ref[...], wbn_ref[...])
    y = _dot(mix.astype(BF16), wout_ref[...]).reshape(nb, tt, d)
    o_ref[...] = x + gt_ref[...] * y


def _mix_call(x3, o_fox, o_nsa, mod3, g1, wmg, wbf, wbn, wout, nb, tt):
    NB, TT, d = x3.shape
    tm = nb * tt
    gi, gj = NB // nb, TT // tt
    full = lambda i, j: (0, 0)
    tok = lambda i, j: (i * gj + j, 0)
    return pl.pallas_call(
        _mix_kernel,
        grid=(gi, gj),
        in_specs=[pl.BlockSpec((nb, tt, d), lambda i, j: (i, j, 0)),
                  pl.BlockSpec((tm, 512), tok),
                  pl.BlockSpec((tm, 512), tok),
                  pl.BlockSpec((nb, 1, d), lambda i, j: (i, 0, 1)),
                  pl.BlockSpec((nb, 1, d), lambda i, j: (i, 0, 0)),
                  pl.BlockSpec((nb, 1, d), lambda i, j: (i, 0, 2)),
                  pl.BlockSpec((1, d), full),
                  pl.BlockSpec(wmg.shape, full),
                  pl.BlockSpec(wbf.shape, full),
                  pl.BlockSpec(wbn.shape, full),
                  pl.BlockSpec(wout.shape, full)],
        out_specs=pl.BlockSpec((nb, tt, d), lambda i, j: (i, j, 0)),
        out_shape=jax.ShapeDtypeStruct((NB, TT, d), F32),
        compiler_params=_cparams(("arbitrary", "arbitrary")),
        name="post_mix",
    )(x3, o_fox, o_nsa, mod3, mod3, mod3, g1, wmg, wbf, wbn, wout)


def _ffn_kernel(x_ref, sc_ref, sh_ref, gt_ref, g2_ref, wup_ref, wdn_ref, o_ref, *, fc):
    nb, tt, d = x_ref.shape
    tm = nb * tt
    x = x_ref[...]
    ms = jnp.mean(x * x, axis=-1, keepdims=True)
    h = x * lax.rsqrt(ms + RMS_EPS) * g2_ref[...] * (1.0 + sc_ref[...]) + sh_ref[...]
    h = h.reshape(tm, d).astype(BF16)
    acc = jnp.zeros((tm, d), F32)
    for c in range(D_FF // fc):
        u = jnp.maximum(_dot(h, wup_ref[:, c * fc:(c + 1) * fc]), 0.0)
        acc = acc + _dot((u * u).astype(BF16), wdn_ref[c * fc:(c + 1) * fc, :])
    o_ref[...] = x + gt_ref[...] * acc.reshape(nb, tt, d)


def _ffn_call(x3, mod3, g2, wup, wdn, nb, tt):
    NB, TT, d = x3.shape
    gi, gj = NB // nb, TT // tt
    full = lambda i, j: (0, 0)
    return pl.pallas_call(
        functools.partial(_ffn_kernel, fc=1024),
        grid=(gi, gj),
        in_specs=[pl.BlockSpec((nb, tt, d), lambda i, j: (i, j, 0)),
                  pl.BlockSpec((nb, 1, d), lambda i, j: (i, 0, 4)),
                  pl.BlockSpec((nb, 1, d), lambda i, j: (i, 0, 3)),
                  pl.BlockSpec((nb, 1, d), lambda i, j: (i, 0, 5)),
                  pl.BlockSpec((1, d), full),
                  pl.BlockSpec(wup.shape, full, pipeline_mode=pl.Buffered(1)),
                  pl.BlockSpec(wdn.shape, full, pipeline_mode=pl.Buffered(1))],
        out_specs=pl.BlockSpec((nb, tt, d), lambda i, j: (i, j, 0)),
        out_shape=jax.ShapeDtypeStruct((NB, TT, d), F32),
        compiler_params=_cparams(("arbitrary", "arbitrary")),
        name="ffn",
    )(x3, mod3, mod3, mod3, g2, wup, wdn)


def _gather_pages(pt_ref, seq, parts, n_pages):
    for p in range(n_pages):
        page = pt_ref[seq, p]
        for pool_ref, row0, buf_ref, sem_ref in parts:
            pltpu.make_async_copy(pool_ref.at[page, pl.ds(row0, buf_ref.shape[0]), :],
                                  buf_ref.at[:, p * PAGE_SIZE:(p + 1) * PAGE_SIZE], sem_ref
                                  ).start(priority=p % 2)


def _gather_done(parts):
    for _, _, buf_ref, sem_ref in parts:
        pltpu.make_async_copy(buf_ref, buf_ref, sem_ref).wait()


def _two_sequence_pipeline(pt_ref, parts_a, parts_b, n_pages, compute):
    i = pl.program_id(0)
    n = pl.num_programs(0)

    @pl.when(i == 0)
    def _():
        _gather_pages(pt_ref, 0, parts_a, n_pages)

    _gather_pages(pt_ref, 2 * i + 1, parts_b, n_pages)
    _gather_done(parts_a)
    out_a = compute(0, [part[2] for part in parts_a])
    _gather_pages(pt_ref, lax.rem(2 * i + 2, 2 * n), parts_a, n_pages)
    _gather_done(parts_b)
    out_b = compute(1, [part[2] for part in parts_b])

    @pl.when(i == n - 1)
    def _():
        _gather_done(parts_a)

    return out_a, out_b


def _softmax_pv(tiles):
    m = functools.reduce(jnp.maximum, [jnp.max(s, axis=-1, keepdims=True) for s, _ in tiles])
    l = jnp.zeros_like(m)
    acc = None
    for s, pv in tiles:
        p = jnp.exp(s - m)
        l = l + jnp.sum(p, axis=-1, keepdims=True)
        contrib = pv(p.astype(BF16))
        acc = contrib if acc is None else acc + contrib
    return acc / l


def _lane_cumsum(x, n):
    lane = lax.broadcasted_iota(jnp.int32, x.shape, 1)
    s = 1
    while s < n:
        x = x + jnp.where(lane >= s, pltpu.roll(x, s, 1), 0.0)
        s *= 2
    return x


def _stack_q(q_ref, tn):
    parts = []
    for g in range(N_GROUPS):
        for r in range(N_REP):
            qc = q_ref[:, r * LANES:(r + 1) * LANES]
            parts.append(jnp.where(_half_mask(tn, g), qc, jnp.zeros_like(qc)))
    return jnp.concatenate(parts, axis=0).astype(BF16)


def _unstack_o(o, tn):
    chunks = []
    for r in range(N_REP):
        a = o[r * tn:(r + 1) * tn]
        b = o[(N_REP + r) * tn:(N_REP + r + 1) * tn]
        chunks.append(jnp.where(_half_mask(tn, 0), a, b))
    return jnp.concatenate(chunks, axis=1)


def _per_head_rows(x8, tn):
    n = x8.shape[1]
    return jnp.broadcast_to(x8[:, None, :], (N_HEADS, tn, n)).reshape(N_HEADS * tn, n)


def _slope_col(tn):
    hrow = lax.broadcasted_iota(jnp.int32, (N_HEADS * tn, 1), 0) // tn
    col = jnp.zeros((N_HEADS * tn, 1), F32)
    for h in range(N_HEADS):
        col = jnp.where(hrow == h, _slope(h), col)
    return col


def _tok_col(tn):
    return lax.rem(lax.broadcasted_iota(jnp.int32, (N_HEADS * tn, 1), 0), tn)


def _pad_keys(x, dtype):
    tn = x.shape[0]
    return jnp.concatenate([x, jnp.zeros((LANES - tn, LANES), x.dtype)], axis=0).astype(dtype)


def _past_key_tiles(qs, kv_ref, kc, bias_fn):
    tiles = []
    for c in range(kv_ref.shape[1] // kc):
        kt = kv_ref[0:LANES, c * kc:(c + 1) * kc].astype(BF16)
        vt = kv_ref[LANES:2 * LANES, c * kc:(c + 1) * kc].astype(BF16)
        tiles.append((_dot(qs, kt) + bias_fn(c), lambda p, vt=vt: _dot_nt(p, vt)))
    return tiles


def _new_key_tile(qs, new_ref, col0, bias):
    knew = _pad_keys(new_ref[:, col0:col0 + LANES], BF16)
    vnew = _pad_keys(new_ref[:, col0 + LANES:col0 + 2 * LANES], BF16)
    return (_dot_nt(qs, knew) + bias, lambda p: _dot(p, vnew))


def _fox_s_kernel(pt_ref, pool_ref, lpool_ref, q_ref, new_ref, lfn_ref, o_ref,
                  kva_ref, kvb_ref, lfa_ref, lfb_ref, sem_ref, *, n_pages, tn, kc):
    past = n_pages * PAGE_SIZE
    R = N_HEADS * tn
    tcol = _tok_col(tn)
    tl = lax.broadcasted_iota(jnp.int32, (R, LANES), 1)

    def compute(u, bufs):
        kv_ref, lf_ref = bufs
        rows = pl.ds(u * tn, tn)
        qs = _stack_q(q_ref.at[rows], tn)
        cpast = _lane_cumsum(lf_ref[...], past)
        cn = _lane_cumsum(lfn_ref[u], tn) + cpast[:, past - 1:past]
        cn_rows = _per_head_rows(cn, tn)
        cq = jnp.sum(jnp.where(tl == tcol, cn_rows, 0.0), axis=-1, keepdims=True)
        tiles = [_new_key_tile(qs, new_ref.at[rows], 0, jnp.where(tl <= tcol, cq - cn_rows, NEG))]
        tiles += _past_key_tiles(qs, kv_ref, kc,
                                 lambda c: cq - _per_head_rows(cpast[:, c * kc:(c + 1) * kc], tn))
        return _unstack_o(_softmax_pv(tiles), tn)

    parts_a = [(pool_ref, 0, kva_ref, sem_ref.at[0]), (lpool_ref, 0, lfa_ref, sem_ref.at[1])]
    parts_b = [(pool_ref, 0, kvb_ref, sem_ref.at[2]), (lpool_ref, 0, lfb_ref, sem_ref.at[3])]
    outs = _two_sequence_pipeline(pt_ref, parts_a, parts_b, n_pages, compute)
    o_ref[...] = jnp.concatenate(outs, axis=0).astype(BF16)


def _fox_s_call(page_table, fox_t, logf_t, qf, frow, lf_new, tn):
    B, n_pages = page_table.shape
    past = n_pages * PAGE_SIZE
    kc = min(past, 2048)
    assert B % 2 == 0
    grid_spec = pltpu.PrefetchScalarGridSpec(
        num_scalar_prefetch=1,
        grid=(B // 2,),
        in_specs=[pl.BlockSpec(memory_space=pl.ANY),
                  pl.BlockSpec(memory_space=pl.ANY),
                  pl.BlockSpec((2 * tn, 512), lambda b, pt: (b, 0)),
                  pl.BlockSpec((2 * tn, 256), lambda b, pt: (b, 0)),
                  pl.BlockSpec((2, N_HEADS, LANES), lambda b, pt: (b, 0, 0))],
        out_specs=pl.BlockSpec((2 * tn, 512), lambda b, pt: (b, 0)),
        scratch_shapes=[pltpu.VMEM((2 * LANES, past), F32), pltpu.VMEM((2 * LANES, past), F32),
                        pltpu.VMEM((N_HEADS, past), F32), pltpu.VMEM((N_HEADS, past), F32),
                        pltpu.SemaphoreType.DMA((4,))],
    )
    return pl.pallas_call(
        functools.partial(_fox_s_kernel, n_pages=n_pages, tn=tn, kc=kc),
        grid_spec=grid_spec,
        out_shape=jax.ShapeDtypeStruct((B * tn, 512), BF16),
        compiler_params=_cparams(("arbitrary",)),
        name="fox_sample",
    )(page_table, fox_t, logf_t, qf, frow, lf_new)


def _cmp_s_kernel(pt_ref, pool_ref, q_ref, w1k_ref, w1v_ref, posk_ref, posv_ref, w2k_ref, w2v_ref,
                  bd_ref, gk_ref, a_ref, oc_ref, imp_ref, kva_ref, kvb_ref, xk_ref, xv_ref, sem_ref,
                  *, n_pages, tn):
    past = n_pages * PAGE_SIZE
    nc = past // CMP_STRIDE
    R = N_HEADS * tn
    n_split = 2 if n_pages % 2 == 0 else 1
    pps = n_pages // n_split
    cps = pps * PAGE_SIZE // CMP_STRIDE

    def compute(u, bufs):
        (kv_ref,) = bufs
        acc_k, acc_v = [], []
        for sp in range(n_split):
            for pg in range(sp * pps, (sp + 1) * pps):
                tok = slice(pg * PAGE_SIZE, (pg + 1) * PAGE_SIZE)
                xk_ref[tok, :] = kv_ref[0:LANES, tok].T
                xv_ref[tok, :] = kv_ref[LANES:2 * LANES, tok].T
            acc_k.append(_chunk_dot(xk_ref, sp * cps, cps, w1k_ref))
            acc_v.append(_chunk_dot(xv_ref, sp * cps, cps, w1v_ref))
        kc, vc = _compress_finish(jnp.concatenate(acc_k, axis=0), jnp.concatenate(acc_v, axis=0), nc,
                                  w1k_ref, w1v_ref, posk_ref, posv_ref, w2k_ref, w2v_ref,
                                  bd_ref[...], gk_ref[...])
        qs = _stack_q(q_ref.at[pl.ds(u * tn, tn)], tn)
        nid = lax.broadcasted_iota(jnp.int32, (R, nc), 1)
        tpos = past + _tok_col(tn)
        dc = tpos - (nid * CMP_STRIDE + (2 * CMP_STRIDE - 1))
        valid = (dc >= 0) & (nid < nc - 1)
        s = jnp.where(valid, _dot_nt(qs, kc.astype(BF16)) - _slope_col(tn) * dc.astype(F32), NEG)
        mx = jnp.max(s, axis=-1, keepdims=True)
        p = jnp.where(valid, jnp.exp(s - mx), 0.0)
        den = jnp.sum(p, axis=-1, keepdims=True)
        p = p / jnp.where(den > 0, den, 1.0)
        oc_ref[u] = _dot(p.astype(BF16), vc.astype(BF16))
        p4 = p.reshape(N_GROUPS, N_REP, tn, nc)
        psum = (p4[:, 0] + p4[:, 1] + p4[:, 2] + p4[:, 3]).reshape(N_GROUPS * tn, nc)
        imp_ref[u] = _dot_exact_r(psum, a_ref[...])

    _two_sequence_pipeline(pt_ref, [(pool_ref, 0, kva_ref, sem_ref.at[0])],
                           [(pool_ref, 0, kvb_ref, sem_ref.at[1])], n_pages, compute)


def _cmp_s_call(page_table, nsa_pool, qn, cw, a_mat, tn):
    B, n_pages = page_table.shape
    past = n_pages * PAGE_SIZE
    consts = [cw["w1k"], cw["w1v"], cw["posk"], cw["posv"], cw["w2k"], cw["w2v"], cw["bd"], cw["gk"], a_mat]
    cspecs = [pl.BlockSpec(a.shape, (lambda b, pt, nd=a.ndim: (0,) * nd)) for a in consts]
    nsl = a_mat.shape[1]
    grid_spec = pltpu.PrefetchScalarGridSpec(
        num_scalar_prefetch=1,
        grid=(B // 2,),
        in_specs=[pl.BlockSpec(memory_space=pl.ANY),
                  pl.BlockSpec((2 * tn, 512), lambda b, pt: (b, 0))] + cspecs,
        out_specs=[pl.BlockSpec((2, N_HEADS * tn, LANES), lambda b, pt: (b, 0, 0)),
                   pl.BlockSpec((2, N_GROUPS * tn, nsl), lambda b, pt: (b, 0, 0))],
        scratch_shapes=[pltpu.VMEM((2 * LANES, past), F32), pltpu.VMEM((2 * LANES, past), F32),
                        pltpu.VMEM((past, LANES), F32), pltpu.VMEM((past, LANES), F32),
                        pltpu.SemaphoreType.DMA((2,))],
    )
    return pl.pallas_call(
        functools.partial(_cmp_s_kernel, n_pages=n_pages, tn=tn),
        grid_spec=grid_spec,
        out_shape=[jax.ShapeDtypeStruct((B, N_HEADS * tn, LANES), F32),
                   jax.ShapeDtypeStruct((B, N_GROUPS * tn, nsl), F32)],
        compiler_params=_cparams(("arbitrary",)),
        name="compress_sample",
    )(page_table, nsa_pool, qn, *consts)


def _topk_s_kernel(imp_ref, o_ref, *, tn, past):
    rows = imp_ref.shape[0]
    t = past + lax.rem(lax.broadcasted_iota(jnp.int32, (rows, 1), 0), tn)
    sel = _select_blocks(imp_ref[...], t // SEL_BLOCK)
    o_ref[...] = ((sel - 1.0) * MASK_BIG).astype(BF16)


def _topk_s_call(imp2, tn, past):
    rows, nsl = imp2.shape
    tr = min(rows, 256)
    return pl.pallas_call(
        functools.partial(_topk_s_kernel, tn=tn, past=past),
        grid=(rows // tr,),
        in_specs=[pl.BlockSpec((tr, nsl), lambda i: (i, 0))],
        out_specs=pl.BlockSpec((tr, nsl), lambda i: (i, 0)),
        out_shape=jax.ShapeDtypeStruct((rows, nsl), BF16),
        compiler_params=_cparams(("arbitrary",)),
        name="select_sample",
    )(imp2)


def _sel_s_kernel(pt_ref, pool_ref, q_ref, nnew_ref, wnew_ref, win_ref, seln_ref, oc_ref, gate_ref, e_ref,
                  o_ref, wout_ref, kva_ref, kvb_ref, sem_ref, *, n_pages, tn, kc):
    past = n_pages * PAGE_SIZE
    R = N_HEADS * tn
    n_chunks = past // kc
    wlen = win_ref.shape[2]
    tcol = _tok_col(tn)
    slope = _slope_col(tn)
    tl = lax.broadcasted_iota(jnp.int32, (R, LANES), 1)
    newbias = jnp.where(tl <= tcol, -slope * (tcol - tl).astype(F32), NEG)

    def group_rows(x):
        n = x.shape[1]
        x4 = jnp.broadcast_to(x.reshape(N_GROUPS, 1, tn, n), (N_GROUPS, N_REP, tn, n))
        return x4.reshape(R, n)

    def compute(u, bufs):
        (kv_ref,) = bufs
        rows = pl.ds(u * tn, tn)
        qs = _stack_q(q_ref.at[rows], tn)

        seln_new = seln_ref[u, n_chunks]
        mb_new = group_rows(_dot(seln_new, e_ref[...])[:, 0:LANES])

        def past_bias(c):
            mb = group_rows(_dot(seln_ref[u, c], e_ref[...]))
            kpos = c * kc + lax.broadcasted_iota(jnp.int32, (1, kc), 1)
            return mb - slope * ((past + tcol) - kpos).astype(F32)

        tiles = [_new_key_tile(qs, nnew_ref.at[rows], 2 * LANES, jnp.where(tl <= tcol, newbias + mb_new, NEG))]
        o_s = _softmax_pv(tiles + _past_key_tiles(qs, kv_ref, kc, past_bias))

        kt = win_ref[u, 0:LANES, :].astype(BF16)
        vt = win_ref[u, LANES:2 * LANES, :].astype(BF16)
        wpos = (past - wlen) + lax.broadcasted_iota(jnp.int32, (1, wlen), 1)
        dw = (past + tcol) - wpos
        bias = jnp.where((dw < WINDOW) & (wpos >= 0), -slope * dw.astype(F32), NEG)
        o_w = _softmax_pv([_new_key_tile(qs, wnew_ref.at[rows], 0, newbias),
                           (_dot(qs, kt) + bias, lambda p: _dot_nt(p, vt))])

        wnew = wnew_ref.at[rows]
        new_t = jnp.concatenate([_pad_keys(wnew[:, 0:LANES], F32).T,
                                 _pad_keys(wnew[:, LANES:2 * LANES], F32).T], axis=0)
        new_t = pltpu.roll(new_t, LANES - tn, 1)
        shifted = pltpu.roll(win_ref[u], wlen - tn, 1)
        lane = lax.broadcasted_iota(jnp.int32, (2 * LANES, LANES), 1)
        wout_ref[u, :, 0:wlen - LANES] = shifted[:, 0:wlen - LANES]
        wout_ref[u, :, wlen - LANES:wlen] = jnp.where(lane >= LANES - tn, new_t, shifted[:, wlen - LANES:wlen])

        gc, gs, gw = _gate_cols(gate_ref.at[rows], tn)
        return _unstack_o(gc * oc_ref[u] + gs * o_s + gw * o_w, tn)

    outs = _two_sequence_pipeline(pt_ref, [(pool_ref, 2 * LANES, kva_ref, sem_ref.at[0])],
                                  [(pool_ref, 2 * LANES, kvb_ref, sem_ref.at[1])], n_pages, compute)
    o_ref[...] = jnp.concatenate(outs, axis=0).astype(BF16)


def _sel_s_call(page_table, nsa_pool, qn, nrow, wrow, win_t, seln, oc, gates, e_mat, tn, kc):
    B, n_pages = page_table.shape
    past = n_pages * PAGE_SIZE
    wlen = win_t.shape[2]
    grid_spec = pltpu.PrefetchScalarGridSpec(
        num_scalar_prefetch=1,
        grid=(B // 2,),
        in_specs=[pl.BlockSpec(memory_space=pl.ANY),
                  pl.BlockSpec((2 * tn, 512), lambda b, pt: (b, 0)),
                  pl.BlockSpec((2 * tn, 512), lambda b, pt: (b, 0)),
                  pl.BlockSpec((2 * tn, 256), lambda b, pt: (b, 0)),
                  pl.BlockSpec((2, 2 * LANES, wlen), lambda b, pt: (b, 0, 0)),
                  pl.BlockSpec((2,) + seln.shape[1:], lambda b, pt: (b, 0, 0, 0)),
                  pl.BlockSpec((2, N_HEADS * tn, LANES), lambda b, pt: (b, 0, 0)),
                  pl.BlockSpec((2 * tn, LANES), lambda b, pt: (b, 0)),
                  pl.BlockSpec(e_mat.shape, lambda b, pt: (0, 0))],
        out_specs=[pl.BlockSpec((2 * tn, 512), lambda b, pt: (b, 0)),
                   pl.BlockSpec((2, 2 * LANES, wlen), lambda b, pt: (b, 0, 0))],
        scratch_shapes=[pltpu.VMEM((2 * LANES, past), F32), pltpu.VMEM((2 * LANES, past), F32),
                        pltpu.SemaphoreType.DMA((2,))],
    )
    return pl.pallas_call(
        functools.partial(_sel_s_kernel, n_pages=n_pages, tn=tn, kc=kc),
        grid_spec=grid_spec,
        out_shape=[jax.ShapeDtypeStruct((B * tn, 512), BF16),
                   jax.ShapeDtypeStruct((B, 2 * LANES, wlen), F32)],
        compiler_params=_cparams(("arbitrary",)),
        name="select_attend_sample",
    )(page_table, nsa_pool, qn, nrow, wrow, win_t, seln, oc, gates, e_mat)


def _pair_cols(base):
    idx = []
    for r in range(N_REP):
        for g in range(N_GROUPS):
            h = g * N_REP + r
            idx.extend(range(base + h * HEAD_DIM, base + (h + 1) * HEAD_DIM))
    return np.asarray(idx, np.int32)


def _prep_weights(w_in, b_fox_f, fox_qn_g, fox_kn_g, nsa_qn_g, nsa_kn_slc_g, nsa_kn_win_g):
    o_fq, o_fk, o_fv, o_ff, o_nq, o_nkv, o_ng, o_mg = 0, 512, 640, 768, 776, 1288, 2056, 2080
    cols = np.concatenate([
        _pair_cols(o_fq), np.arange(o_fk, o_fk + 128), np.arange(o_fv, o_fv + 128),
        _pair_cols(o_nq), np.arange(o_nkv, o_nkv + 768),
        np.arange(o_ff, o_ff + 8), np.arange(o_ng, o_ng + 24)]).astype(np.int32)
    w_p = jnp.take(w_in, cols, axis=1)
    w_p = jnp.pad(w_p, ((0, 0), (0, _C_END - w_p.shape[1]))).astype(BF16)
    w_mg = w_in[:, o_mg:o_mg + 2 * D_MODEL].astype(BF16)
    tile2 = lambda g: jnp.tile(g, 2)
    gains = jnp.stack([tile2(fox_qn_g), tile2(fox_kn_g), tile2(nsa_qn_g), tile2(nsa_kn_slc_g),
                       tile2(nsa_kn_win_g)] + [jnp.zeros((LANES,), F32)] * 3)
    bff = jnp.zeros((1, LANES), F32).at[0, LOGF_LANE0:LOGF_LANE0 + N_HEADS].set(b_fox_f)
    return w_p, w_mg, gains, bff


def _prep_compress(pos, w1, w2):
    w1r = w1.reshape(2, CMP_STRIDE, HEAD_DIM, CMP_HIDDEN)
    z = jnp.zeros((CMP_STRIDE, HEAD_DIM, CMP_HIDDEN), F32)
    top = jnp.concatenate([w1r[0], z, w1r[1], z], axis=-1)
    bot = jnp.concatenate([z, w1r[0], z, w1r[1]], axis=-1)
    w1b = jnp.concatenate([top, bot], axis=1).astype(BF16)
    w1b = w1b.reshape(CMP_STRIDE * LANES, 4 * LANES)
    pr = pos.reshape(2, CMP_STRIDE, HEAD_DIM)
    posb = jnp.concatenate([jnp.tile(pr, (1, 1, 2)).transpose(1, 0, 2),
                            jnp.zeros((CMP_STRIDE, 6, LANES), F32)], axis=1).astype(BF16)
    posb = posb.transpose(1, 0, 2).reshape(8, CMP_STRIDE * LANES)
    zz = jnp.zeros((CMP_HIDDEN, HEAD_DIM), F32)
    w2b = jnp.concatenate([jnp.concatenate([w2, zz], axis=1),
                           jnp.concatenate([zz, w2], axis=1)], axis=0).astype(BF16)
    return w1b, posb, w2b


def kernel(x_prompt, x_sample, cache_fox_kv, cache_fox_logf, cache_nsa_kv, state_win_kv, page_table,
           c_prompt, c_sample, norm1_g, norm2_g, w_ada, b_ada, w_in, b_fox_f, fox_qn_g, fox_kn_g,
           nsa_qn_g, nsa_kn_cmp_g, nsa_kn_slc_g, nsa_kn_win_g, cmp_pos_k, cmp_w1_k, cmp_w2_k,
           cmp_pos_v, cmp_w1_v, cmp_w2_v, w_br_fox, w_br_nsa, w_out, w_up, w_down):
    assert norm1_g.shape[0] == 1
    B, T, d = x_prompt.shape
    DB, TN, _ = x_sample.shape
    n_phys = cache_fox_kv.shape[1]
    n_pages = page_table.shape[1]
    past = n_pages * PAGE_SIZE
    wlen = state_win_kv.shape[2]
    assert T % 256 == 0 and T >= WINDOW and wlen == WINDOW and TN == 8 and past % CMP_STRIDE == 0

    w_p, w_mg, gains, bff = _prep_weights(w_in[0], b_fox_f[0], fox_qn_g[0], fox_kn_g[0], nsa_qn_g[0],
                                          nsa_kn_slc_g[0], nsa_kn_win_g[0])
    w1k, posk, w2k = _prep_compress(cmp_pos_k[0], cmp_w1_k[0], cmp_w2_k[0])
    w1v, posv, w2v = _prep_compress(cmp_pos_v[0], cmp_w1_v[0], cmp_w2_v[0])
    bd = jnp.asarray(np.kron(np.eye(2), np.ones((HEAD_DIM, HEAD_DIM))), BF16)
    bd2 = jnp.asarray(np.kron(np.eye(4), np.ones((HEAD_DIM, HEAD_DIM))), BF16)
    cw = dict(w1k=w1k, w1v=w1v, posk=posk, posv=posv, w2k=w2k, w2v=w2v, bd=bd,
              gk=jnp.tile(nsa_kn_cmp_g[0], 2).reshape(1, LANES))
    pair_rows = _pair_cols(0)
    wbf = jnp.take(w_br_fox[0], pair_rows, axis=0).astype(BF16)
    wbn = jnp.take(w_br_nsa[0], pair_rows, axis=0).astype(BF16)
    wout = w_out[0].astype(BF16)
    wup = w_up[0].astype(BF16)
    wdn = w_down[0].astype(BF16)
    g1 = norm1_g[0].reshape(1, d)
    g2 = norm2_g[0].reshape(1, d)
    tm_p = 512
    tri = jnp.asarray(np.tril(np.ones((tm_p, tm_p), np.float32)), BF16)

    mod = _ada_call(jnp.concatenate([c_prompt, c_sample], axis=0), w_ada[0], b_ada[0])
    mod_p = mod[:B].reshape(B, 1, 6 * d)
    mod_s = mod[B:].reshape(DB, 1, 6 * d)

    (qf, frow_t, qn, nrow_t, wrow_t, gates, lf, c_tm, kf, vf, ks, vs, kw, vw, nraw) = _pre_call(
        x_prompt, mod_p, g1, w_p, bd2, tri, gains, bff, nb=1, tt=tm_p, do_cum=True)
    tq, tk, tw = 128, 512, 256
    o_fox = _fox_p_call(qf, kf, vf, c_tm, B, T, tq, tk)
    kc_p, vc_p = _cmp_p_call(nraw, cw, B, T)
    nc_p = T // CMP_STRIDE
    nsel_p = -(-T // SEL_BLOCK)
    assert nsel_p <= LANES - _AUG_SEL0 and T % tk == 0
    a_p = jnp.asarray(_importance_matrix(nc_p, LANES, nc_p - 1)[:, :LANES] *
                      (np.arange(LANES) < nsel_p)[None, :], BF16)
    o_nsa = _nsa_p_call(qn, ks, vs, kw, vw, kc_p, vc_p, gates, a_p.T, B, T, tq, tk, tw)
    x1 = _mix_call(x_prompt, o_fox, o_nsa, mod_p, g1, w_mg, wbf, wbn, wout, nb=1, tt=tm_p)
    y_prompt = _ffn_call(x1, mod_p, g2, wup, wdn, nb=1, tt=tm_p)

    nb_s = min(DB, 32)
    (qf_s, frow_s, qn_s, nrow_s, wrow_s, gates_s, lf_s) = _pre_call(
        x_sample, mod_s, g1, w_p, bd2, tri, gains, bff, nb=nb_s, tt=TN, do_cum=False)
    qf_s = qf_s.astype(F32)
    qn_s = qn_s.astype(F32)

    fox_t = jnp.transpose(cache_fox_kv[0], (0, 2, 3, 4, 1)).reshape(n_phys, 2 * LANES, PAGE_SIZE)
    nsa_t = jnp.transpose(cache_nsa_kv[0], (0, 2, 3, 4, 1)).reshape(n_phys, 4 * LANES, PAGE_SIZE)
    logf_t = jnp.transpose(cache_fox_logf[0], (0, 2, 1))
    win_t = jnp.transpose(state_win_kv[0], (0, 2, 3, 4, 1)).reshape(DB, 2 * LANES, wlen)

    lf_new = lf_s[:, LOGF_LANE0:LOGF_LANE0 + N_HEADS].reshape(DB, TN, N_HEADS).transpose(0, 2, 1)
    lf_new = jnp.pad(lf_new, ((0, 0), (0, 0), (0, LANES - TN)))
    o_fox_s = _fox_s_call(page_table, fox_t, logf_t, qf_s, frow_s, lf_new, TN)

    nsa_pool = nsa_t
    nc_s = past // CMP_STRIDE
    nsel_s = -(-(past + TN) // SEL_BLOCK)
    nsl = -(-nsel_s // LANES) * LANES
    a_s = jnp.asarray(_importance_matrix(nc_s, nsl, nc_s - 1) * (np.arange(nsl) < nsel_s)[None, :], BF16)
    oc_s, imp_s = _cmp_s_call(page_table, nsa_pool, qn_s, cw, a_s, TN)
    seln = _topk_s_call(imp_s.reshape(DB * N_GROUPS * TN, nsl), TN, past)
    kc_keys = min(past, 2048)
    bpc = kc_keys // SEL_BLOCK
    seln = seln.reshape(DB, N_GROUPS * TN, nsl // bpc, bpc).transpose(0, 2, 1, 3)
    e_s = jnp.asarray((np.arange(kc_keys)[None, :] // SEL_BLOCK) == np.arange(bpc)[:, None], BF16)
    o_nsa_s, win_new_t = _sel_s_call(page_table, nsa_pool, qn_s, nrow_s, wrow_s, win_t, seln, oc_s, gates_s,
                                     e_s, TN, kc_keys)
    x1_s = _mix_call(x_sample, o_fox_s, o_nsa_s, mod_s, g1, w_mg, wbf, wbn, wout, nb=nb_s, tt=TN)
    y_sample = _ffn_call(x1_s, mod_s, g2, wup, wdn, nb=nb_s, tt=TN)

    def token_major(rows_t, n_slots):
        nbat, _, toks = rows_t.shape
        return rows_t.reshape(nbat, n_slots, N_GROUPS, HEAD_DIM, toks).transpose(0, 4, 1, 2, 3)

    lf_p = lf[:, LOGF_LANE0:LOGF_LANE0 + N_HEADS]
    return (y_prompt, y_sample,
            token_major(frow_t, 2)[None],
            lf_p.reshape(1, B, T, N_HEADS),
            token_major(nrow_t, 4)[None],
            token_major(wrow_t[:, :, T - wlen:], 2)[None],
            frow_s.reshape(1, DB, TN, 2, N_GROUPS, HEAD_DIM),
            lf_s[:, LOGF_LANE0:LOGF_LANE0 + N_HEADS].reshape(1, DB, TN, N_HEADS),
            nrow_s.reshape(1, DB, TN, 4, N_GROUPS, HEAD_DIM),
            token_major(win_new_t, 2)[None])
```

```python
import functools

import numpy as np
import jax
import jax.numpy as jnp
from jax import lax
from jax.experimental import pallas as pl
from jax.experimental.pallas import tpu as pltpu

F32 = jnp.float32
BF16 = jnp.bfloat16

D_MODEL = 1024
HEAD_DIM = 64
N_HEADS = 8
N_GROUPS = 2
N_REP = N_HEADS // N_GROUPS
PAGE_SIZE = 128
CMP_STRIDE = 16
CMP_HIDDEN = 2 * HEAD_DIM
SEL_BLOCK = 64
N_SELECT = 16
WINDOW = 512
D_FF = 4 * D_MODEL
RMS_EPS = 1e-6
FORCE_BONUS = 1.0e4
LANES = 128
NEG = -1e30
MASK_BIG = 2.0 ** 100
LOGF_LANE0 = 0
GATE_LANE0 = 8
VMEM_LIMIT = 56 * 1024 * 1024


def _dot(a, b):
    return jnp.dot(a, b, preferred_element_type=F32)


def _dot_nt(a, b):
    return lax.dot_general(a, b, (((1,), (1,)), ((), ())), preferred_element_type=F32)


def _split3(x):
    x1 = x.astype(BF16)
    r = x - x1.astype(F32)
    x2 = r.astype(BF16)
    x3 = (r - x2.astype(F32)).astype(BF16)
    return x1, x2, x3


def _dot_exact_r(x, m):
    a, b, c = _split3(x)
    return _dot(a, m) + _dot(b, m) + _dot(c, m)


def _dot_exact_l(m, x):
    a, b, c = _split3(x)
    return _dot(m, a) + _dot(m, b) + _dot(m, c)


def _sigmoid(x):
    return 1.0 / (1.0 + jnp.exp(-x))


def _head_rms(zc, bd):
    a = zc * zc
    a1 = a.astype(BF16)
    a2 = (a - a1.astype(F32)).astype(BF16)
    ss = _dot(a1, bd) + _dot(a2, bd)
    return zc * lax.rsqrt(ss * (1.0 / HEAD_DIM) + RMS_EPS)


def _cparams(sem, vmem=VMEM_LIMIT):
    return pltpu.CompilerParams(dimension_semantics=sem, vmem_limit_bytes=vmem)


def _ada_kernel(c_ref, w_ref, b_ref, o_ref):
    c = c_ref[...]
    a = c * _sigmoid(c)
    o_ref[...] = _dot(a.astype(BF16), w_ref[...].astype(BF16)) + b_ref[...]


def _ada_call(c, w_ada, b_ada):
    r, d = c.shape
    n = w_ada.shape[1]
    tn = 1024
    return pl.pallas_call(
        _ada_kernel,
        grid=(n // tn,),
        in_specs=[pl.BlockSpec((r, d), lambda j: (0, 0)),
                  pl.BlockSpec((d, tn), lambda j: (0, j)),
                  pl.BlockSpec((1, tn), lambda j: (0, j))],
        out_specs=pl.BlockSpec((r, tn), lambda j: (0, j)),
        out_shape=jax.ShapeDtypeStruct((r, n), F32),
        compiler_params=_cparams(("arbitrary",)),
        name="ada_mod",
    )(c, w_ada, b_ada.reshape(1, n))


_C_FQ, _C_FK, _C_FV, _C_NQ, _C_NKV, _C_SMALL, _C_END = 0, 512, 640, 768, 1280, 2048, 2176


def _v_with_ones(v):
    lane = lax.broadcasted_iota(jnp.int32, v.shape, 1)
    return jnp.concatenate([jnp.where(lane < HEAD_DIM, v, 1.0), jnp.where(lane < HEAD_DIM, 1.0, v)],
                           axis=1).astype(BF16)


def _pos_hi_lo(pos):
    return (pos >> 8).astype(F32), (pos & 255).astype(F32)


_AUG_SEL0 = 4


def _key_pos_features(pos, lane):
    hi, lo = _pos_hi_lo(pos)
    return jnp.where(lane < 2, 1.0,
                     jnp.where(lane == 2, hi,
                               jnp.where(lane == 3, lo,
                                         jnp.where(lane - _AUG_SEL0 == pos // SEL_BLOCK, 1.0, 0.0))))


def _query_pos_features(pos, slope, lane):
    hi, lo = _pos_hi_lo(pos)
    return jnp.where(lane == 0, -(slope * 256.0) * hi,
                     jnp.where(lane == 1, -slope * lo,
                               jnp.where(lane == 2, slope * 256.0,
                                         jnp.where(lane == 3, slope, 0.0))))


def _pre_kernel(x_ref, sc_ref, sh_ref, g1_ref, w_ref, bd_ref, tri_ref, gains_ref, bff_ref,
                qf_ref, frow_ref, qn_ref, nrow_ref, wrow_ref, gate_ref, lf_ref,
                *rest, do_cum):
    nb, tt, d = x_ref.shape
    tm = nb * tt
    x = x_ref[...]
    ms = jnp.mean(x * x, axis=-1, keepdims=True)
    h = x * lax.rsqrt(ms + RMS_EPS) * g1_ref[...] * (1.0 + sc_ref[...]) + sh_ref[...]
    h = h.reshape(tm, d).astype(BF16)
    bd2 = bd_ref[...]

    def proj(c0, n_chunks=1):
        z = _dot(h, w_ref[:, c0:c0 + n_chunks * LANES])
        return [z[:, i * LANES:(i + 1) * LANES] for i in range(n_chunks)]

    def rms_pair(za, zb):
        n = _head_rms(jnp.concatenate([za, zb], axis=1), bd2)
        return n[:, 0:LANES], n[:, LANES:2 * LANES]

    scale = HEAD_DIM ** -0.5
    for q_ref, c0, grow in ((qf_ref, _C_FQ, 0), (qn_ref, _C_NQ, 2)):
        z = proj(c0, N_REP)
        for r, zc in enumerate(rms_pair(z[0], z[1]) + rms_pair(z[2], z[3])):
            q_ref[:, r * LANES:(r + 1) * LANES] = (zc * (gains_ref[grow:grow + 1, :] * scale)).astype(BF16)

    def put(ref, i, val):
        if do_cum:
            ref[0, i * LANES:(i + 1) * LANES, :] = val.T
        else:
            ref[:, i * LANES:(i + 1) * LANES] = val

    fk, fv = proj(_C_FK, 2)
    ck, cv = proj(_C_NKV, 2)
    sk, sv = proj(_C_NKV + 2 * LANES, 2)
    wk, wv = proj(_C_NKV + 4 * LANES, 2)
    fk, sk = rms_pair(fk, sk)
    fk = fk * gains_ref[1:2, :]
    sk = sk * gains_ref[3:4, :]
    wk = _head_rms(wk, bd2[0:LANES, 0:LANES]) * gains_ref[4:5, :]
    put(frow_ref, 0, fk)
    put(frow_ref, 1, fv)
    for i, val in enumerate((ck, cv, sk, sv)):
        put(nrow_ref, i, val)
    put(wrow_ref, 0, wk)
    put(wrow_ref, 1, wv)

    (zl,) = proj(_C_SMALL)
    gate_ref[...] = _sigmoid(zl)
    xl = zl + bff_ref[...]
    lf = jnp.minimum(xl, 0.0) - jnp.log1p(jnp.exp(-jnp.abs(xl)))
    lf_ref[...] = lf

    if do_cum:
        c_ref, kf_ref, vf_ref, ks_ref, vs_ref, kw_ref, vw_ref, nraw_ref, carry_ref = rest
        nraw_ref[:, 0:LANES] = ck
        nraw_ref[:, LANES:2 * LANES] = cv

        @pl.when(pl.program_id(1) == 0)
        def _():
            carry_ref[...] = jnp.zeros_like(carry_ref)

        c = _dot_exact_l(tri_ref[...], lf) + carry_ref[0:1, :]
        c_ref[...] = c
        carry_ref[...] = jnp.broadcast_to(c[tm - 1:tm, :], carry_ref.shape)

        lane = lax.broadcasted_iota(jnp.int32, (tm, LANES), 1)
        c1, c2, c3 = (p.astype(F32) for p in _split3(c))
        aug_f = jnp.where(lane < N_HEADS, -c1,
                          jnp.where(lane < 2 * N_HEADS, -pltpu.roll(c2, N_HEADS, 1),
                                    jnp.where(lane < 3 * N_HEADS, -pltpu.roll(c3, 2 * N_HEADS, 1),
                                              jnp.where(lane < 3 * N_HEADS + 3, 1.0, 0.0))))
        kf_ref[:, 0:LANES] = fk.astype(BF16)
        kf_ref[:, LANES:2 * LANES] = aug_f.astype(BF16)
        vf_ref[...] = _v_with_ones(fv)

        pos = pl.program_id(1) * tm + lax.broadcasted_iota(jnp.int32, (tm, 1), 0)
        aug_p = _key_pos_features(pos, lane).astype(BF16)
        ks_ref[:, 0:LANES] = sk.astype(BF16)
        ks_ref[:, LANES:2 * LANES] = aug_p
        vs_ref[...] = _v_with_ones(sv)
        kw_ref[:, 0:LANES] = wk.astype(BF16)
        kw_ref[:, LANES:2 * LANES] = aug_p
        vw_ref[...] = _v_with_ones(wv)


def _pre_call(x3, mod3, g1, w_p, bd, tri, gains, bff, nb, tt, do_cum):
    NB, TT, d = x3.shape
    tm = nb * tt
    n = NB * TT
    gi, gj = NB // nb, TT // tt
    tok = lambda i, j: (i * gj + j, 0)
    full = lambda i, j: (0, 0)
    in_specs = [
        pl.BlockSpec((nb, tt, d), lambda i, j: (i, j, 0)),
        pl.BlockSpec((nb, 1, d), lambda i, j: (i, 0, 1)),
        pl.BlockSpec((nb, 1, d), lambda i, j: (i, 0, 0)),
        pl.BlockSpec((1, d), full),
        pl.BlockSpec(w_p.shape, full),
        pl.BlockSpec(bd.shape, full),
        pl.BlockSpec(tri.shape, full),
        pl.BlockSpec(gains.shape, full),
        pl.BlockSpec(bff.shape, full),
    ]
    def rows_out(width):
        if do_cum:
            return (jax.ShapeDtypeStruct((NB, width, TT), F32),
                    pl.BlockSpec((1, width, tm), lambda i, j: (i, 0, j)))
        return jax.ShapeDtypeStruct((n, width), F32), pl.BlockSpec((tm, width), tok)

    def tok_out(width, dtype):
        return jax.ShapeDtypeStruct((n, width), dtype), pl.BlockSpec((tm, width), tok)

    outs = [
        tok_out(512, BF16),
        rows_out(256),
        tok_out(512, BF16),
        rows_out(512),
        rows_out(256),
        tok_out(LANES, F32),
        tok_out(LANES, F32),
    ]
    scratch = []
    if do_cum:
        assert nb == 1
        outs += [tok_out(LANES, F32)]
        outs += [tok_out(256, BF16)] * 6
        outs += [tok_out(256, F32)]
        scratch = [pltpu.VMEM((8, LANES), F32)]
    out_shape = [o[0] for o in outs]
    out_specs = [o[1] for o in outs]
    return pl.pallas_call(
        functools.partial(_pre_kernel, do_cum=do_cum),
        grid=(gi, gj),
        in_specs=in_specs,
        out_specs=out_specs,
        out_shape=out_shape,
        scratch_shapes=scratch,
        compiler_params=_cparams(("arbitrary", "arbitrary")),
        name="pre_mixer",
    )(x3, mod3, mod3, g1, w_p, bd, tri, gains, bff)


def _half_mask(rows, g):
    lane = lax.broadcasted_iota(jnp.int32, (rows, LANES), 1)
    return (lane < HEAD_DIM) if g == 0 else (lane >= HEAD_DIM)


def _slope(h):
    return 2.0 ** (-8.0 * (h + 1) / N_HEADS)


def _select_blocks(imp, cur):
    blk = lax.broadcasted_iota(jnp.int32, imp.shape, 1)
    forced = (blk == 0) | (blk == cur) | (blk == cur - 1)
    score = jnp.where(blk <= cur, imp + jnp.where(forced, FORCE_BONUS, 0.0), -jnp.inf)
    blkf = blk.astype(F32)

    def body(_, carry):
        sc, sel = carry
        mx = jnp.max(sc, axis=-1, keepdims=True)
        idx = jnp.min(jnp.where(sc == mx, blkf, 1e9), axis=-1, keepdims=True)
        pick = blkf == idx
        return jnp.where(pick, -jnp.inf, sc), jnp.where(pick, 1.0, sel)

    _, sel = lax.fori_loop(0, N_SELECT, body, (score, jnp.zeros(imp.shape, F32)))
    return jnp.where(blk <= cur, sel, 0.0)


def _select_blocks_by_rank(imp_t, cur, n_blocks):
    blk = lax.broadcasted_iota(jnp.int32, imp_t.shape, 0)
    forced = (blk == 0) | (blk == cur) | (blk == cur - 1)
    score = jnp.where(blk <= cur, imp_t + jnp.where(forced, FORCE_BONUS, 0.0), -jnp.inf)
    beaten_by = jnp.zeros(imp_t.shape, F32)
    for i in range(n_blocks):
        si = score[i:i + 1, :]
        beaten_by = beaten_by + jnp.where((si > score) | ((si == score) & (blk > i)), 1.0, 0.0)
    return jnp.where((beaten_by < N_SELECT) & (blk <= cur), 1.0, 0.0)


def _importance_matrix(n_cmp_rows, n_sel_cols, n_cmp):
    r = SEL_BLOCK // CMP_STRIDE
    a = np.zeros((n_cmp_rows, n_sel_cols), np.float32)
    for n in range(n_cmp):
        for j in range(n_sel_cols):
            off = n - r * j
            if off in (-1, r - 1):
                a[n, j] = 0.5
            elif 0 <= off <= r - 2:
                a[n, j] = 1.0
    return a


def _stack_q_features(q_ref, qa_ref, tq, feat_fn):
    for g in range(N_GROUPS):
        for r in range(N_REP):
            h = g * N_REP + r
            qc = q_ref[:, r * LANES:(r + 1) * LANES]
            qa_ref[h * tq:(h + 1) * tq, 0:LANES] = jnp.where(_half_mask(tq, g), qc, jnp.zeros_like(qc))
            qa_ref[h * tq:(h + 1) * tq, LANES:2 * LANES] = feat_fn(h).astype(BF16)


def _stacked_attend(qa, k_ref, v_ref, tk, first, lo, mask_first, mask_rest):
    half = qa.shape[0] // 2

    def probabilities(j, mask, m_old):
        st = pl.multiple_of(j * tk, tk)
        s = mask(_dot_nt(qa, k_ref[pl.ds(st, tk), :]))
        m_new = jnp.max(s, axis=-1, keepdims=True)
        if m_old is not None:
            m_new = jnp.maximum(m_old, m_new)
        p = jnp.exp(s - m_new).astype(BF16)
        va = v_ref[pl.ds(st, tk), :]
        pv = jnp.concatenate([_dot(p[:half], va[:, 0:LANES]), _dot(p[half:], va[:, LANES:2 * LANES])], axis=0)
        return m_new, pv

    def body(j, carry):
        m_old, acc = carry
        m_new, pv = probabilities(j, lambda s: mask_rest(j, s), m_old)
        return m_new, jnp.exp(m_old - m_new) * acc + pv

    m, acc = lax.fori_loop(lo, first, body, probabilities(first, mask_first, None))
    return acc / pltpu.roll(acc, HEAD_DIM, 1)


def _row_pos(t0, tq):
    return t0 + lax.rem(lax.broadcasted_iota(jnp.int32, (N_HEADS * tq, 1), 0), tq)


def _fox_p_kernel(q_ref, ka_ref, va_ref, c_ref, o_ref, qa_ref, *, tq, tk):
    qi = pl.program_id(1)
    t0 = qi * tq
    R = N_HEADS * tq
    lane = lax.broadcasted_iota(jnp.int32, (tq, LANES), 1)
    c1, c2, c3 = (p.astype(F32) for p in _split3(c_ref[...]))

    def q_features(h):
        own = (lane == h) | (lane == N_HEADS + h) | (lane == 2 * N_HEADS + h)
        return jnp.where(lane == 3 * N_HEADS, c1[:, h:h + 1],
                         jnp.where(lane == 3 * N_HEADS + 1, c2[:, h:h + 1],
                                   jnp.where(lane == 3 * N_HEADS + 2, c3[:, h:h + 1],
                                             jnp.where(own, 1.0, 0.0))))

    _stack_q_features(q_ref, qa_ref, tq, q_features)
    qa = qa_ref[...]
    tpos = _row_pos(t0, tq)

    jd = t0 // tk
    kpos = jd * tk + lax.broadcasted_iota(jnp.int32, (1, tk), 1)
    o = _stacked_attend(qa, ka_ref, va_ref, tk, jd, 0,
                        lambda s: jnp.where(kpos <= tpos, s, NEG), lambda j, s: s)
    o_ref[...] = _unstack_o(o, tq).astype(BF16)


def _fox_p_call(qf, ka, va, c_tm, B, T, tq, tk):
    n = B * T
    nq = T // tq
    return pl.pallas_call(
        functools.partial(_fox_p_kernel, tq=tq, tk=tk),
        grid=(B, nq),
        in_specs=[pl.BlockSpec((tq, 512), lambda b, i: (b * nq + i, 0)),
                  pl.BlockSpec((T, 256), lambda b, i: (b, 0)),
                  pl.BlockSpec((T, 256), lambda b, i: (b, 0)),
                  pl.BlockSpec((tq, LANES), lambda b, i: (b * nq + i, 0))],
        out_specs=pl.BlockSpec((tq, 512), lambda b, i: (b * nq + i, 0)),
        out_shape=jax.ShapeDtypeStruct((n, 512), BF16),
        scratch_shapes=[pltpu.VMEM((N_HEADS * tq, 2 * LANES), BF16)],
        compiler_params=_cparams(("arbitrary", "arbitrary")),
        name="fox_prompt",
    )(qf, ka, va, c_tm)


def _chunk_dot(x_ref, chunk0, n, w1_ref):
    rows = jnp.concatenate(
        [x_ref[pl.ds(chunk0 * CMP_STRIDE + c, n, stride=CMP_STRIDE), :].astype(BF16)
         for c in range(CMP_STRIDE)], axis=1)
    return _dot(rows, w1_ref[...])


def _compress(xk_ref, xv_ref, n, w1k_ref, w1v_ref, posk_ref, posv_ref, w2k_ref, w2v_ref, bd, gk):
    return _compress_finish(_chunk_dot(xk_ref, 0, n, w1k_ref), _chunk_dot(xv_ref, 0, n, w1v_ref), n,
                            w1k_ref, w1v_ref, posk_ref, posv_ref, w2k_ref, w2v_ref, bd, gk)


def _compress_finish(acc_k, acc_v, n, w1k_ref, w1v_ref, posk_ref, posv_ref, w2k_ref, w2v_ref, bd, gk):
    ck = _dot(posk_ref[...], w1k_ref[...])
    cv = _dot(posv_ref[...], w1v_ref[...])

    def finish(acc, cst, w2_ref):
        lead = acc[:, 0:2 * LANES] + cst[0:1, 0:2 * LANES]
        trail = acc[:, 2 * LANES:4 * LANES] + cst[1:2, 2 * LANES:4 * LANES]
        hid = lead + pltpu.roll(trail, n - 1, 0)
        act = hid * _sigmoid(hid)
        return _dot(act.astype(BF16), w2_ref[...])

    kc = _head_rms(finish(acc_k, ck, w2k_ref), bd) * gk
    vc = finish(acc_v, cv, w2v_ref)
    return kc, vc


def _cmp_p_kernel(xk_ref, xv_ref, w1k_ref, w1v_ref, posk_ref, posv_ref, w2k_ref, w2v_ref, bd_ref, gk_ref,
                  kc_ref, vc_ref):
    n = kc_ref.shape[0]
    kc, vc = _compress(xk_ref, xv_ref, n, w1k_ref, w1v_ref, posk_ref, posv_ref, w2k_ref, w2v_ref,
                       bd_ref[...], gk_ref[...])
    kc_ref[:, 0:LANES] = kc.astype(BF16)
    c_end = lax.broadcasted_iota(jnp.int32, (n, 1), 0) * CMP_STRIDE + (2 * CMP_STRIDE - 1)
    lane = lax.broadcasted_iota(jnp.int32, (n, LANES), 1)
    kc_ref[:, LANES:2 * LANES] = _key_pos_features(c_end, lane).astype(BF16)
    vc_ref[...] = vc.astype(BF16)


def _cmp_p_call(nrow, cw, B, T):
    nc = T // CMP_STRIDE
    consts = [cw["w1k"], cw["w1v"], cw["posk"], cw["posv"], cw["w2k"], cw["w2v"], cw["bd"], cw["gk"]]
    cspecs = [pl.BlockSpec(a.shape, (lambda b, nd=a.ndim: (0,) * nd)) for a in consts]
    return pl.pallas_call(
        _cmp_p_kernel,
        grid=(B,),
        in_specs=[pl.BlockSpec((T, LANES), lambda b: (b, 0)),
                  pl.BlockSpec((T, LANES), lambda b: (b, 1))] + cspecs,
        out_specs=[pl.BlockSpec((nc, 2 * LANES), lambda b: (b, 0)),
                   pl.BlockSpec((nc, LANES), lambda b: (b, 0))],
        out_shape=[jax.ShapeDtypeStruct((B * nc, 2 * LANES), BF16),
                   jax.ShapeDtypeStruct((B * nc, LANES), BF16)],
        compiler_params=_cparams(("arbitrary",)),
        name="compress_prompt",
    )(nrow, nrow, *consts)


def _gate_cols(gate_ref, tn):
    R = N_HEADS * tn
    gl = lax.broadcasted_iota(jnp.int32, (R, LANES), 1)
    hrow = lax.broadcasted_iota(jnp.int32, (R, LANES), 0) // tn
    gt = jnp.broadcast_to(gate_ref[...][None], (N_HEADS, tn, LANES)).reshape(R, LANES)
    return [jnp.sum(jnp.where(gl == GATE_LANE0 + 3 * hrow + j, gt, 0.0), axis=-1, keepdims=True)
            for j in range(3)]


def _nsa_p_kernel(q_ref, ks_ref, vs_ref, kw_ref, vw_ref, kc_ref, vc_ref, gate_ref, at_ref, o_ref, qa_ref,
                  *, tq, tk, tw, n_sel):
    qi = pl.program_id(1)
    t0 = qi * tq
    R = N_HEADS * tq
    nc = kc_ref.shape[0]
    lane = lax.broadcasted_iota(jnp.int32, (tq, LANES), 1)
    pos_q = t0 + lax.broadcasted_iota(jnp.int32, (tq, 1), 0)
    _stack_q_features(q_ref, qa_ref, tq, lambda h: _query_pos_features(pos_q, _slope(h), lane))
    qa = qa_ref[...]
    tpos = _row_pos(t0, tq)

    nid = lax.broadcasted_iota(jnp.int32, (1, nc), 1)
    cvalid = (nid * CMP_STRIDE + (2 * CMP_STRIDE - 1) <= tpos) & (nid < nc - 1)
    s = jnp.where(cvalid, _dot_nt(qa, kc_ref[...]), NEG)
    mx = jnp.max(s, axis=-1, keepdims=True)
    p = jnp.where(cvalid, jnp.exp(s - mx), 0.0)
    den = jnp.sum(p, axis=-1, keepdims=True)
    p = p / jnp.where(den > 0, den, 1.0)
    o_c = _dot(p.astype(BF16), vc_ref[...])
    p4 = p.reshape(N_GROUPS, N_REP, tq, nc)
    psum = (p4[:, 0] + p4[:, 1] + p4[:, 2] + p4[:, 3]).reshape(N_GROUPS * tq, nc)

    assert tw + tq <= WINDOW

    def key_pos(j):
        return j * tw + lax.broadcasted_iota(jnp.int32, (1, tw), 1)

    jw = t0 // tw
    o_w = _stacked_attend(qa, kw_ref, vw_ref, tw, jw, jnp.maximum(t0 - (WINDOW - 1), 0) // tw,
                          lambda s: jnp.where(key_pos(jw) <= tpos, s, NEG),
                          lambda j, s: jnp.where(tpos - key_pos(j) < WINDOW, s, NEG))

    p1, p2, p3 = _split3(psum)
    at = at_ref[...]
    imp_t = _dot_nt(at, p1) + _dot_nt(at, p2) + _dot_nt(at, p3)
    n_rows = -(-n_sel // 8) * 8
    cur = (t0 + lax.rem(lax.broadcasted_iota(jnp.int32, (1, N_GROUPS * tq), 1), tq)) // SEL_BLOCK
    sel_t = _select_blocks_by_rank(imp_t[0:n_rows], cur, n_sel)
    sel_t = jnp.concatenate([sel_t, jnp.zeros((LANES - n_rows, N_GROUPS * tq), F32)], axis=0)
    seln = pltpu.roll(((sel_t - 1.0) * MASK_BIG).T, _AUG_SEL0, 1)
    for g in range(N_GROUPS):
        for r in range(N_REP):
            h = g * N_REP + r
            feat = jnp.where(lane >= _AUG_SEL0, seln[g * tq:(g + 1) * tq],
                             _query_pos_features(pos_q, _slope(h), lane))
            qa_ref[h * tq:(h + 1) * tq, LANES:2 * LANES] = feat.astype(BF16)
    qa = qa_ref[...]

    js = t0 // tk
    kpos = js * tk + lax.broadcasted_iota(jnp.int32, (1, tk), 1)
    o_s = _stacked_attend(qa, ks_ref, vs_ref, tk, js, 0,
                          lambda s: jnp.where(kpos <= tpos, s, NEG), lambda j, s: s)

    gc, gs, gw = _gate_cols(gate_ref, tq)
    o_ref[...] = _unstack_o(gc * o_c + gs * o_s + gw * o_w, tq).astype(BF16)


def _nsa_p_call(qn, ks, vs, kw, vw, kc, vc, gates, a_mat, B, T, tq, tk, tw):
    n = B * T
    nq = T // tq
    nc = T // CMP_STRIDE
    seq = lambda b, i: (b, 0)
    blk = lambda b, i: (b * nq + i, 0)
    return pl.pallas_call(
        functools.partial(_nsa_p_kernel, tq=tq, tk=tk, tw=tw, n_sel=-(-T // SEL_BLOCK)),
        grid=(B, nq),
        in_specs=[pl.BlockSpec((tq, 512), blk),
                  pl.BlockSpec((T, 256), seq), pl.BlockSpec((T, 256), seq),
                  pl.BlockSpec((T, 256), seq), pl.BlockSpec((T, 256), seq),
                  pl.BlockSpec((nc, 2 * LANES), seq),
                  pl.BlockSpec((nc, LANES), seq),
                  pl.BlockSpec((tq, LANES), blk),
                  pl.BlockSpec(a_mat.shape, lambda b, i: (0, 0))],
        out_specs=pl.BlockSpec((tq, 512), blk),
        out_shape=jax.ShapeDtypeStruct((n, 512), BF16),
        scratch_shapes=[pltpu.VMEM((N_HEADS * tq, 2 * LANES), BF16)],
        compiler_params=_cparams(("arbitrary", "arbitrary")),
        name="nsa_prompt",
    )(qn, ks, vs, kw, vw, kc, vc, gates, a_mat)


def _mix_kernel(x_ref, of_ref, on_ref, sc_ref, sh_ref, gt_ref, g1_ref, wmg_ref, wbf_ref, wbn_ref,
                wout_ref, o_ref):
    nb, tt, d = x_ref.shape
    tm = nb * tt
    x = x_ref[...]
    ms = jnp.mean(x * x, axis=-1, keepdims=True)
    h = x * lax.rsqrt(ms + RMS_EPS) * g1_ref[...] * (1.0 + sc_ref[...]) + sh_ref[...]
    h = h.reshape(tm, d).astype(BF16)
    g_fox = _sigmoid(_dot(h, wmg_ref[:, 0:d]))
    g_nsa = _sigmoid(_dot(h, wmg_ref[:, d:2 * d]))
    mix = g_fox * _dot(of_ref[...], wbf_ref[...]) + g_nsa * _dot(on_ref[...], wbn_ref[...])
    y = _dot(mix.astype(BF16), wout_ref[...]).reshape(nb, tt, d)
    o_ref[...] = x + gt_ref[...] * y


def _mix_call(x3, o_fox, o_nsa, mod3, g1, wmg, wbf, wbn, wout, nb, tt):
    NB, TT, d = x3.shape
    tm = nb * tt
    gi, gj = NB // nb, TT // tt
    full = lambda i, j: (0, 0)
    tok = lambda i, j: (i * gj + j, 0)
    return pl.pallas_call(
        _mix_kernel,
        grid=(gi, gj),
        in_specs=[pl.BlockSpec((nb, tt, d), lambda i, j: (i, j, 0)),
                  pl.BlockSpec((tm, 512), tok),
                  pl.BlockSpec((tm, 512), tok),
                  pl.BlockSpec((nb, 1, d), lambda i, j: (i, 0, 1)),
                  pl.BlockSpec((nb, 1, d), lambda i, j: (i, 0, 0)),
                  pl.BlockSpec((nb, 1, d), lambda i, j: (i, 0, 2)),
                  pl.BlockSpec((1, d), full),
                  pl.BlockSpec(wmg.shape, full),
                  pl.BlockSpec(wbf.shape, full),
                  pl.BlockSpec(wbn.shape, full),
                  pl.BlockSpec(wout.shape, full)],
        out_specs=pl.BlockSpec((nb, tt, d), lambda i, j: (i, j, 0)),
        out_shape=jax.ShapeDtypeStruct((NB, TT, d), F32),
        compiler_params=_cparams(("arbitrary", "arbitrary")),
        name="post_mix",
    )(x3, o_fox, o_nsa, mod3, mod3, mod3, g1, wmg, wbf, wbn, wout)


def _ffn_kernel(x_ref, sc_ref, sh_ref, gt_ref, g2_ref, wup_ref, wdn_ref, o_ref, *, fc):
    nb, tt, d = x_ref.shape
    tm = nb * tt
    x = x_ref[...]
    ms = jnp.mean(x * x, axis=-1, keepdims=True)
    h = x * lax.rsqrt(ms + RMS_EPS) * g2_ref[...] * (1.0 + sc_ref[...]) + sh_ref[...]
    h = h.reshape(tm, d).astype(BF16)
    acc = jnp.zeros((tm, d), F32)
    for c in range(D_FF // fc):
        u = jnp.maximum(_dot(h, wup_ref[:, c * fc:(c + 1) * fc]), 0.0)
        acc = acc + _dot((u * u).astype(BF16), wdn_ref[c * fc:(c + 1) * fc, :])
    o_ref[...] = x + gt_ref[...] * acc.reshape(nb, tt, d)


def _ffn_call(x3, mod3, g2, wup, wdn, nb, tt):
    NB, TT, d = x3.shape
    gi, gj = NB // nb, TT // tt
    full = lambda i, j: (0, 0)
    return pl.pallas_call(
        functools.partial(_ffn_kernel, fc=1024),
        grid=(gi, gj),
        in_specs=[pl.BlockSpec((nb, tt, d), lambda i, j: (i, j, 0)),
                  pl.BlockSpec((nb, 1, d), lambda i, j: (i, 0, 4)),
                  pl.BlockSpec((nb, 1, d), lambda i, j: (i, 0, 3)),
                  pl.BlockSpec((nb, 1, d), lambda i, j: (i, 0, 5)),
                  pl.BlockSpec((1, d), full),
                  pl.BlockSpec(wup.shape, full, pipeline_mode=pl.Buffered(1)),
                  pl.BlockSpec(wdn.shape, full, pipeline_mode=pl.Buffered(1))],
        out_specs=pl.BlockSpec((nb, tt, d), lambda i, j: (i, j, 0)),
        out_shape=jax.ShapeDtypeStruct((NB, TT, d), F32),
        compiler_params=_cparams(("arbitrary", "arbitrary")),
        name="ffn",
    )(x3, mod3, mod3, mod3, g2, wup, wdn)


def _gather_pages(pt_ref, seq, parts, n_pages):
    for p in range(n_pages):
        page = pt_ref[seq, p]
        for pool_ref, row0, buf_ref, sem_ref in parts:
            pltpu.make_async_copy(pool_ref.at[page, pl.ds(row0, buf_ref.shape[0]), :],
                                  buf_ref.at[:, p * PAGE_SIZE:(p + 1) * PAGE_SIZE], sem_ref
                                  ).start(priority=p % 2)


def _gather_done(parts):
    for _, _, buf_ref, sem_ref in parts:
        pltpu.make_async_copy(buf_ref, buf_ref, sem_ref).wait()


def _two_sequence_pipeline(pt_ref, parts_a, parts_b, n_pages, compute):
    i = pl.program_id(0)
    n = pl.num_programs(0)

    @pl.when(i == 0)
    def _():
        _gather_pages(pt_ref, 0, parts_a, n_pages)

    _gather_pages(pt_ref, 2 * i + 1, parts_b, n_pages)
    _gather_done(parts_a)
    out_a = compute(0, [part[2] for part in parts_a])
    _gather_pages(pt_ref, lax.rem(2 * i + 2, 2 * n), parts_a, n_pages)
    _gather_done(parts_b)
    out_b = compute(1, [part[2] for part in parts_b])

    @pl.when(i == n - 1)
    def _():
        _gather_done(parts_a)

    return out_a, out_b


def _softmax_pv(tiles):
    m = functools.reduce(jnp.maximum, [jnp.max(s, axis=-1, keepdims=True) for s, _ in tiles])
    l = jnp.zeros_like(m)
    acc = None
    for s, pv in tiles:
        p = jnp.exp(s - m)
        l = l + jnp.sum(p, axis=-1, keepdims=True)
        contrib = pv(p.astype(BF16))
        acc = contrib if acc is None else acc + contrib
    return acc / l


def _lane_cumsum(x, n):
    lane = lax.broadcasted_iota(jnp.int32, x.shape, 1)
    s = 1
    while s < n:
        x = x + jnp.where(lane >= s, pltpu.roll(x, s, 1), 0.0)
        s *= 2
    return x


def _stack_q(q_ref, tn):
    parts = []
    for g in range(N_GROUPS):
        for r in range(N_REP):
            qc = q_ref[:, r * LANES:(r + 1) * LANES]
            parts.append(jnp.where(_half_mask(tn, g), qc, jnp.zeros_like(qc)))
    return jnp.concatenate(parts, axis=0).astype(BF16)


def _unstack_o(o, tn):
    chunks = []
    for r in range(N_REP):
        a = o[r * tn:(r + 1) * tn]
        b = o[(N_REP + r) * tn:(N_REP + r + 1) * tn]
        chunks.append(jnp.where(_half_mask(tn, 0), a, b))
    return jnp.concatenate(chunks, axis=1)


def _per_head_rows(x8, tn):
    n = x8.shape[1]
    return jnp.broadcast_to(x8[:, None, :], (N_HEADS, tn, n)).reshape(N_HEADS * tn, n)


def _slope_col(tn):
    hrow = lax.broadcasted_iota(jnp.int32, (N_HEADS * tn, 1), 0) // tn
    col = jnp.zeros((N_HEADS * tn, 1), F32)
    for h in range(N_HEADS):
        col = jnp.where(hrow == h, _slope(h), col)
    return col


def _tok_col(tn):
    return lax.rem(lax.broadcasted_iota(jnp.int32, (N_HEADS * tn, 1), 0), tn)


def _pad_keys(x, dtype):
    tn = x.shape[0]
    return jnp.concatenate([x, jnp.zeros((LANES - tn, LANES), x.dtype)], axis=0).astype(dtype)


def _past_key_tiles(qs, kv_ref, kc, bias_fn):
    tiles = []
    for c in range(kv_ref.shape[1] // kc):
        kt = kv_ref[0:LANES, c * kc:(c + 1) * kc].astype(BF16)
        vt = kv_ref[LANES:2 * LANES, c * kc:(c + 1) * kc].astype(BF16)
        tiles.append((_dot(qs, kt) + bias_fn(c), lambda p, vt=vt: _dot_nt(p, vt)))
    return tiles


def _new_key_tile(qs, new_ref, col0, bias):
    knew = _pad_keys(new_ref[:, col0:col0 + LANES], BF16)
    vnew = _pad_keys(new_ref[:, col0 + LANES:col0 + 2 * LANES], BF16)
    return (_dot_nt(qs, knew) + bias, lambda p: _dot(p, vnew))


def _fox_s_kernel(pt_ref, pool_ref, lpool_ref, q_ref, new_ref, lfn_ref, o_ref,
                  kva_ref, kvb_ref, lfa_ref, lfb_ref, sem_ref, *, n_pages, tn, kc):
    past = n_pages * PAGE_SIZE
    R = N_HEADS * tn
    tcol = _tok_col(tn)
    tl = lax.broadcasted_iota(jnp.int32, (R, LANES), 1)

    def compute(u, bufs):
        kv_ref, lf_ref = bufs
        rows = pl.ds(u * tn, tn)
        qs = _stack_q(q_ref.at[rows], tn)
        cpast = _lane_cumsum(lf_ref[...], past)
        cn = _lane_cumsum(lfn_ref[u], tn) + cpast[:, past - 1:past]
        cn_rows = _per_head_rows(cn, tn)
        cq = jnp.sum(jnp.where(tl == tcol, cn_rows, 0.0), axis=-1, keepdims=True)
        tiles = [_new_key_tile(qs, new_ref.at[rows], 0, jnp.where(tl <= tcol, cq - cn_rows, NEG))]
        tiles += _past_key_tiles(qs, kv_ref, kc,
                                 lambda c: cq - _per_head_rows(cpast[:, c * kc:(c + 1) * kc], tn))
        return _unstack_o(_softmax_pv(tiles), tn)

    parts_a = [(pool_ref, 0, kva_ref, sem_ref.at[0]), (lpool_ref, 0, lfa_ref, sem_ref.at[1])]
    parts_b = [(pool_ref, 0, kvb_ref, sem_ref.at[2]), (lpool_ref, 0, lfb_ref, sem_ref.at[3])]
    outs = _two_sequence_pipeline(pt_ref, parts_a, parts_b, n_pages, compute)
    o_ref[...] = jnp.concatenate(outs, axis=0).astype(BF16)


def _fox_s_call(page_table, fox_t, logf_t, qf, frow, lf_new, tn):
    B, n_pages = page_table.shape
    past = n_pages * PAGE_SIZE
    kc = min(past, 2048)
    assert B % 2 == 0
    grid_spec = pltpu.PrefetchScalarGridSpec(
        num_scalar_prefetch=1,
        grid=(B // 2,),
        in_specs=[pl.BlockSpec(memory_space=pl.ANY),
                  pl.BlockSpec(memory_space=pl.ANY),
                  pl.BlockSpec((2 * tn, 512), lambda b, pt: (b, 0)),
                  pl.BlockSpec((2 * tn, 256), lambda b, pt: (b, 0)),
                  pl.BlockSpec((2, N_HEADS, LANES), lambda b, pt: (b, 0, 0))],
        out_specs=pl.BlockSpec((2 * tn, 512), lambda b, pt: (b, 0)),
        scratch_shapes=[pltpu.VMEM((2 * LANES, past), F32), pltpu.VMEM((2 * LANES, past), F32),
                        pltpu.VMEM((N_HEADS, past), F32), pltpu.VMEM((N_HEADS, past), F32),
                        pltpu.SemaphoreType.DMA((4,))],
    )
    return pl.pallas_call(
        functools.partial(_fox_s_kernel, n_pages=n_pages, tn=tn, kc=kc),
        grid_spec=grid_spec,
        out_shape=jax.ShapeDtypeStruct((B * tn, 512), BF16),
        compiler_params=_cparams(("arbitrary",)),
        name="fox_sample",
    )(page_table, fox_t, logf_t, qf, frow, lf_new)


def _cmp_s_kernel(pt_ref, pool_ref, q_ref, w1k_ref, w1v_ref, posk_ref, posv_ref, w2k_ref, w2v_ref,
                  bd_ref, gk_ref, a_ref, oc_ref, imp_ref, kva_ref, kvb_ref, xk_ref, xv_ref, sem_ref,
                  *, n_pages, tn):
    past = n_pages * PAGE_SIZE
    nc = past // CMP_STRIDE
    R = N_HEADS * tn
    n_split = 2 if n_pages % 2 == 0 else 1
    pps = n_pages // n_split
    cps = pps * PAGE_SIZE // CMP_STRIDE

    def compute(u, bufs):
        (kv_ref,) = bufs
        acc_k, acc_v = [], []
        for sp in range(n_split):
            for pg in range(sp * pps, (sp + 1) * pps):
                tok = slice(pg * PAGE_SIZE, (pg + 1) * PAGE_SIZE)
                xk_ref[tok, :] = kv_ref[0:LANES, tok].T
                xv_ref[tok, :] = kv_ref[LANES:2 * LANES, tok].T
            acc_k.append(_chunk_dot(xk_ref, sp * cps, cps, w1k_ref))
            acc_v.append(_chunk_dot(xv_ref, sp * cps, cps, w1v_ref))
        kc, vc = _compress_finish(jnp.concatenate(acc_k, axis=0), jnp.concatenate(acc_v, axis=0), nc,
                                  w1k_ref, w1v_ref, posk_ref, posv_ref, w2k_ref, w2v_ref,
                                  bd_ref[...], gk_ref[...])
        qs = _stack_q(q_ref.at[pl.ds(u * tn, tn)], tn)
        nid = lax.broadcasted_iota(jnp.int32, (R, nc), 1)
        tpos = past + _tok_col(tn)
        dc = tpos - (nid * CMP_STRIDE + (2 * CMP_STRIDE - 1))
        valid = (dc >= 0) & (nid < nc - 1)
        s = jnp.where(valid, _dot_nt(qs, kc.astype(BF16)) - _slope_col(tn) * dc.astype(F32), NEG)
        mx = jnp.max(s, axis=-1, keepdims=True)
        p = jnp.where(valid, jnp.exp(s - mx), 0.0)
        den = jnp.sum(p, axis=-1, keepdims=True)
        p = p / jnp.where(den > 0, den, 1.0)
        oc_ref[u] = _dot(p.astype(BF16), vc.astype(BF16))
        p4 = p.reshape(N_GROUPS, N_REP, tn, nc)
        psum = (p4[:, 0] + p4[:, 1] + p4[:, 2] + p4[:, 3]).reshape(N_GROUPS * tn, nc)
        imp_ref[u] = _dot_exact_r(psum, a_ref[...])

    _two_sequence_pipeline(pt_ref, [(pool_ref, 0, kva_ref, sem_ref.at[0])],
                           [(pool_ref, 0, kvb_ref, sem_ref.at[1])], n_pages, compute)


def _cmp_s_call(page_table, nsa_pool, qn, cw, a_mat, tn):
    B, n_pages = page_table.shape
    past = n_pages * PAGE_SIZE
    consts = [cw["w1k"], cw["w1v"], cw["posk"], cw["posv"], cw["w2k"], cw["w2v"], cw["bd"], cw["gk"], a_mat]
    cspecs = [pl.BlockSpec(a.shape, (lambda b, pt, nd=a.ndim: (0,) * nd)) for a in consts]
    nsl = a_mat.shape[1]
    grid_spec = pltpu.PrefetchScalarGridSpec(
        num_scalar_prefetch=1,
        grid=(B // 2,),
        in_specs=[pl.BlockSpec(memory_space=pl.ANY),
                  pl.BlockSpec((2 * tn, 512), lambda b, pt: (b, 0))] + cspecs,
        out_specs=[pl.BlockSpec((2, N_HEADS * tn, LANES), lambda b, pt: (b, 0, 0)),
                   pl.BlockSpec((2, N_GROUPS * tn, nsl), lambda b, pt: (b, 0, 0))],
        scratch_shapes=[pltpu.VMEM((2 * LANES, past), F32), pltpu.VMEM((2 * LANES, past), F32),
                        pltpu.VMEM((past, LANES), F32), pltpu.VMEM((past, LANES), F32),
                        pltpu.SemaphoreType.DMA((2,))],
    )
    return pl.pallas_call(
        functools.partial(_cmp_s_kernel, n_pages=n_pages, tn=tn),
        grid_spec=grid_spec,
        out_shape=[jax.ShapeDtypeStruct((B, N_HEADS * tn, LANES), F32),
                   jax.ShapeDtypeStruct((B, N_GROUPS * tn, nsl), F32)],
        compiler_params=_cparams(("arbitrary",)),
        name="compress_sample",
    )(page_table, nsa_pool, qn, *consts)


def _topk_s_kernel(imp_ref, o_ref, *, tn, past):
    rows = imp_ref.shape[0]
    t = past + lax.rem(lax.broadcasted_iota(jnp.int32, (rows, 1), 0), tn)
    sel = _select_blocks(imp_ref[...], t // SEL_BLOCK)
    o_ref[...] = ((sel - 1.0) * MASK_BIG).astype(BF16)


def _topk_s_call(imp2, tn, past):
    rows, nsl = imp2.shape
    tr = min(rows, 256)
    return pl.pallas_call(
        functools.partial(_topk_s_kernel, tn=tn, past=past),
        grid=(rows // tr,),
        in_specs=[pl.BlockSpec((tr, nsl), lambda i: (i, 0))],
        out_specs=pl.BlockSpec((tr, nsl), lambda i: (i, 0)),
        out_shape=jax.ShapeDtypeStruct((rows, nsl), BF16),
        compiler_params=_cparams(("arbitrary",)),
        name="select_sample",
    )(imp2)


def _sel_s_kernel(pt_ref, pool_ref, q_ref, nnew_ref, wnew_ref, win_ref, seln_ref, oc_ref, gate_ref, e_ref,
                  o_ref, wout_ref, kva_ref, kvb_ref, sem_ref, *, n_pages, tn, kc):
    past = n_pages * PAGE_SIZE
    R = N_HEADS * tn
    n_chunks = past // kc
    wlen = win_ref.shape[2]
    tcol = _tok_col(tn)
    slope = _slope_col(tn)
    tl = lax.broadcasted_iota(jnp.int32, (R, LANES), 1)
    newbias = jnp.where(tl <= tcol, -slope * (tcol - tl).astype(F32), NEG)

    def group_rows(x):
        n = x.shape[1]
        x4 = jnp.broadcast_to(x.reshape(N_GROUPS, 1, tn, n), (N_GROUPS, N_REP, tn, n))
        return x4.reshape(R, n)

    def compute(u, bufs):
        (kv_ref,) = bufs
        rows = pl.ds(u * tn, tn)
        qs = _stack_q(q_ref.at[rows], tn)

        seln_new = seln_ref[u, n_chunks]
        mb_new = group_rows(_dot(seln_new, e_ref[...])[:, 0:LANES])

        def past_bias(c):
            mb = group_rows(_dot(seln_ref[u, c], e_ref[...]))
            kpos = c * kc + lax.broadcasted_iota(jnp.int32, (1, kc), 1)
            return mb - slope * ((past + tcol) - kpos).astype(F32)

        tiles = [_new_key_tile(qs, nnew_ref.at[rows], 2 * LANES, jnp.where(tl <= tcol, newbias + mb_new, NEG))]
        o_s = _softmax_pv(tiles + _past_key_tiles(qs, kv_ref, kc, past_bias))

        kt = win_ref[u, 0:LANES, :].astype(BF16)
        vt = win_ref[u, LANES:2 * LANES, :].astype(BF16)
        wpos = (past - wlen) + lax.broadcasted_iota(jnp.int32, (1, wlen), 1)
        dw = (past + tcol) - wpos
        bias = jnp.where((dw < WINDOW) & (wpos >= 0), -slope * dw.astype(F32), NEG)
        o_w = _softmax_pv([_new_key_tile(qs, wnew_ref.at[rows], 0, newbias),
                           (_dot(qs, kt) + bias, lambda p: _dot_nt(p, vt))])

        wnew = wnew_ref.at[rows]
        new_t = jnp.concatenate([_pad_keys(wnew[:, 0:LANES], F32).T,
                                 _pad_keys(wnew[:, LANES:2 * LANES], F32).T], axis=0)
        new_t = pltpu.roll(new_t, LANES - tn, 1)
        shifted = pltpu.roll(win_ref[u], wlen - tn, 1)
        lane = lax.broadcasted_iota(jnp.int32, (2 * LANES, LANES), 1)
        wout_ref[u, :, 0:wlen - LANES] = shifted[:, 0:wlen - LANES]
        wout_ref[u, :, wlen - LANES:wlen] = jnp.where(lane >= LANES - tn, new_t, shifted[:, wlen - LANES:wlen])

        gc, gs, gw = _gate_cols(gate_ref.at[rows], tn)
        return _unstack_o(gc * oc_ref[u] + gs * o_s + gw * o_w, tn)

    outs = _two_sequence_pipeline(pt_ref, [(pool_ref, 2 * LANES, kva_ref, sem_ref.at[0])],
                                  [(pool_ref, 2 * LANES, kvb_ref, sem_ref.at[1])], n_pages, compute)
    o_ref[...] = jnp.concatenate(outs, axis=0).astype(BF16)


def _sel_s_call(page_table, nsa_pool, qn, nrow, wrow, win_t, seln, oc, gates, e_mat, tn, kc):
    B, n_pages = page_table.shape
    past = n_pages * PAGE_SIZE
    wlen = win_t.shape[2]
    grid_spec = pltpu.PrefetchScalarGridSpec(
        num_scalar_prefetch=1,
        grid=(B // 2,),
        in_specs=[pl.BlockSpec(memory_space=pl.ANY),
                  pl.BlockSpec((2 * tn, 512), lambda b, pt: (b, 0)),
                  pl.BlockSpec((2 * tn, 512), lambda b, pt: (b, 0)),
                  pl.BlockSpec((2 * tn, 256), lambda b, pt: (b, 0)),
                  pl.BlockSpec((2, 2 * LANES, wlen), lambda b, pt: (b, 0, 0)),
                  pl.BlockSpec((2,) + seln.shape[1:], lambda b, pt: (b, 0, 0, 0)),
                  pl.BlockSpec((2, N_HEADS * tn, LANES), lambda b, pt: (b, 0, 0)),
                  pl.BlockSpec((2 * tn, LANES), lambda b, pt: (b, 0)),
                  pl.BlockSpec(e_mat.shape, lambda b, pt: (0, 0))],
        out_specs=[pl.BlockSpec((2 * tn, 512), lambda b, pt: (b, 0)),
                   pl.BlockSpec((2, 2 * LANES, wlen), lambda b, pt: (b, 0, 0))],
        scratch_shapes=[pltpu.VMEM((2 * LANES, past), F32), pltpu.VMEM((2 * LANES, past), F32),
                        pltpu.SemaphoreType.DMA((2,))],
    )
    return pl.pallas_call(
        functools.partial(_sel_s_kernel, n_pages=n_pages, tn=tn, kc=kc),
        grid_spec=grid_spec,
        out_shape=[jax.ShapeDtypeStruct((B * tn, 512), BF16),
                   jax.ShapeDtypeStruct((B, 2 * LANES, wlen), F32)],
        compiler_params=_cparams(("arbitrary",)),
        name="select_attend_sample",
    )(page_table, nsa_pool, qn, nrow, wrow, win_t, seln, oc, gates, e_mat)


def _pair_cols(base):
    idx = []
    for r in range(N_REP):
        for g in range(N_GROUPS):
            h = g * N_REP + r
            idx.extend(range(base + h * HEAD_DIM, base + (h + 1) * HEAD_DIM))
    return np.asarray(idx, np.int32)


def _prep_weights(w_in, b_fox_f, fox_qn_g, fox_kn_g, nsa_qn_g, nsa_kn_slc_g, nsa_kn_win_g):
    o_fq, o_fk, o_fv, o_ff, o_nq, o_nkv, o_ng, o_mg = 0, 512, 640, 768, 776, 1288, 2056, 2080
    cols = np.concatenate([
        _pair_cols(o_fq), np.arange(o_fk, o_fk + 128), np.arange(o_fv, o_fv + 128),
        _pair_cols(o_nq), np.arange(o_nkv, o_nkv + 768),
        np.arange(o_ff, o_ff + 8), np.arange(o_ng, o_ng + 24)]).astype(np.int32)
    w_p = jnp.take(w_in, cols, axis=1)
    w_p = jnp.pad(w_p, ((0, 0), (0, _C_END - w_p.shape[1]))).astype(BF16)
    w_mg = w_in[:, o_mg:o_mg + 2 * D_MODEL].astype(BF16)
    tile2 = lambda g: jnp.tile(g, 2)
    gains = jnp.stack([tile2(fox_qn_g), tile2(fox_kn_g), tile2(nsa_qn_g), tile2(nsa_kn_slc_g),
                       tile2(nsa_kn_win_g)] + [jnp.zeros((LANES,), F32)] * 3)
    bff = jnp.zeros((1, LANES), F32).at[0, LOGF_LANE0:LOGF_LANE0 + N_HEADS].set(b_fox_f)
    return w_p, w_mg, gains, bff


def _prep_compress(pos, w1, w2):
    w1r = w1.reshape(2, CMP_STRIDE, HEAD_DIM, CMP_HIDDEN)
    z = jnp.zeros((CMP_STRIDE, HEAD_DIM, CMP_HIDDEN), F32)
    top = jnp.concatenate([w1r[0], z, w1r[1], z], axis=-1)
    bot = jnp.concatenate([z, w1r[0], z, w1r[1]], axis=-1)
    w1b = jnp.concatenate([top, bot], axis=1).astype(BF16)
    w1b = w1b.reshape(CMP_STRIDE * LANES, 4 * LANES)
    pr = pos.reshape(2, CMP_STRIDE, HEAD_DIM)
    posb = jnp.concatenate([jnp.tile(pr, (1, 1, 2)).transpose(1, 0, 2),
                            jnp.zeros((CMP_STRIDE, 6, LANES), F32)], axis=1).astype(BF16)
    posb = posb.transpose(1, 0, 2).reshape(8, CMP_STRIDE * LANES)
    zz = jnp.zeros((CMP_HIDDEN, HEAD_DIM), F32)
    w2b = jnp.concatenate([jnp.concatenate([w2, zz], axis=1),
                           jnp.concatenate([zz, w2], axis=1)], axis=0).astype(BF16)
    return w1b, posb, w2b


def kernel(x_prompt, x_sample, cache_fox_kv, cache_fox_logf, cache_nsa_kv, state_win_kv, page_table,
           c_prompt, c_sample, norm1_g, norm2_g, w_ada, b_ada, w_in, b_fox_f, fox_qn_g, fox_kn_g,
           nsa_qn_g, nsa_kn_cmp_g, nsa_kn_slc_g, nsa_kn_win_g, cmp_pos_k, cmp_w1_k, cmp_w2_k,
           cmp_pos_v, cmp_w1_v, cmp_w2_v, w_br_fox, w_br_nsa, w_out, w_up, w_down):
    assert norm1_g.shape[0] == 1
    B, T, d = x_prompt.shape
    DB, TN, _ = x_sample.shape
    n_phys = cache_fox_kv.shape[1]
    n_pages = page_table.shape[1]
    past = n_pages * PAGE_SIZE
    wlen = state_win_kv.shape[2]
    assert T % 256 == 0 and T >= WINDOW and wlen == WINDOW and TN == 8 and past % CMP_STRIDE == 0

    w_p, w_mg, gains, bff = _prep_weights(w_in[0], b_fox_f[0], fox_qn_g[0], fox_kn_g[0], nsa_qn_g[0],
                                          nsa_kn_slc_g[0], nsa_kn_win_g[0])
    w1k, posk, w2k = _prep_compress(cmp_pos_k[0], cmp_w1_k[0], cmp_w2_k[0])
    w1v, posv, w2v = _prep_compress(cmp_pos_v[0], cmp_w1_v[0], cmp_w2_v[0])
    bd = jnp.asarray(np.kron(np.eye(2), np.ones((HEAD_DIM, HEAD_DIM))), BF16)
    bd2 = jnp.asarray(np.kron(np.eye(4), np.ones((HEAD_DIM, HEAD_DIM))), BF16)
    cw = dict(w1k=w1k, w1v=w1v, posk=posk, posv=posv, w2k=w2k, w2v=w2v, bd=bd,
              gk=jnp.tile(nsa_kn_cmp_g[0], 2).reshape(1, LANES))
    pair_rows = _pair_cols(0)
    wbf = jnp.take(w_br_fox[0], pair_rows, axis=0).astype(BF16)
    wbn = jnp.take(w_br_nsa[0], pair_rows, axis=0).astype(BF16)
    wout = w_out[0].astype(BF16)
    wup = w_up[0].astype(BF16)
    wdn = w_down[0].astype(BF16)
    g1 = norm1_g[0].reshape(1, d)
    g2 = norm2_g[0].reshape(1, d)
    tm_p = 512
    tri = jnp.asarray(np.tril(np.ones((tm_p, tm_p), np.float32)), BF16)

    mod = _ada_call(jnp.concatenate([c_prompt, c_sample], axis=0), w_ada[0], b_ada[0])
    mod_p = mod[:B].reshape(B, 1, 6 * d)
    mod_s = mod[B:].reshape(DB, 1, 6 * d)

    (qf, frow_t, qn, nrow_t, wrow_t, gates, lf, c_tm, kf, vf, ks, vs, kw, vw, nraw) = _pre_call(
        x_prompt, mod_p, g1, w_p, bd2, tri, gains, bff, nb=1, tt=tm_p, do_cum=True)
    tq, tk, tw = 256, 512, 256
    o_fox = _fox_p_call(qf, kf, vf, c_tm, B, T, tq, tk)
    kc_p, vc_p = _cmp_p_call(nraw, cw, B, T)
    nc_p = T // CMP_STRIDE
    nsel_p = -(-T // SEL_BLOCK)
    assert nsel_p <= LANES - _AUG_SEL0 and T % tk == 0
    a_p = jnp.asarray(_importance_matrix(nc_p, LANES, nc_p - 1)[:, :LANES] *
                      (np.arange(LANES) < nsel_p)[None, :], BF16)
    o_nsa = _nsa_p_call(qn, ks, vs, kw, vw, kc_p, vc_p, gates, a_p.T, B, T, tq, tk, tw)
    x1 = _mix_call(x_prompt, o_fox, o_nsa, mod_p, g1, w_mg, wbf, wbn, wout, nb=1, tt=tm_p)
    y_prompt = _ffn_call(x1, mod_p, g2, wup, wdn, nb=1, tt=tm_p)

    nb_s = min(DB, 32)
    (qf_s, frow_s, qn_s, nrow_s, wrow_s, gates_s, lf_s) = _pre_call(
        x_sample, mod_s, g1, w_p, bd2, tri, gains, bff, nb=nb_s, tt=TN, do_cum=False)
    qf_s = qf_s.astype(F32)
    qn_s = qn_s.astype(F32)

    fox_t = jnp.transpose(cache_fox_kv[0], (0, 2, 3, 4, 1)).reshape(n_phys, 2 * LANES, PAGE_SIZE)
    nsa_t = jnp.transpose(cache_nsa_kv[0], (0, 2, 3, 4, 1)).reshape(n_phys, 4 * LANES, PAGE_SIZE)
    logf_t = jnp.transpose(cache_fox_logf[0], (0, 2, 1))
    win_t = jnp.transpose(state_win_kv[0], (0, 2, 3, 4, 1)).reshape(DB, 2 * LANES, wlen)

    lf_new = lf_s[:, LOGF_LANE0:LOGF_LANE0 + N_HEADS].reshape(DB, TN, N_HEADS).transpose(0, 2, 1)
    lf_new = jnp.pad(lf_new, ((0, 0), (0, 0), (0, LANES - TN)))
    o_fox_s = _fox_s_call(page_table, fox_t, logf_t, qf_s, frow_s, lf_new, TN)

    nsa_pool = nsa_t
    nc_s = past // CMP_STRIDE
    nsel_s = -(-(past + TN) // SEL_BLOCK)
    nsl = -(-nsel_s // LANES) * LANES
    a_s = jnp.asarray(_importance_matrix(nc_s, nsl, nc_s - 1) * (np.arange(nsl) < nsel_s)[None, :], BF16)
    oc_s, imp_s = _cmp_s_call(page_table, nsa_pool, qn_s, cw, a_s, TN)
    seln = _topk_s_call(imp_s.reshape(DB * N_GROUPS * TN, nsl), TN, past)
    kc_keys = min(past, 2048)
    bpc = kc_keys // SEL_BLOCK
    seln = seln.reshape(DB, N_GROUPS * TN, nsl // bpc, bpc).transpose(0, 2, 1, 3)
    e_s = jnp.asarray((np.arange(kc_keys)[None, :] // SEL_BLOCK) == np.arange(bpc)[:, None], BF16)
    o_nsa_s, win_new_t = _sel_s_call(page_table, nsa_pool, qn_s, nrow_s, wrow_s, win_t, seln, oc_s, gates_s,
                                     e_s, TN, kc_keys)
    x1_s = _mix_call(x_sample, o_fox_s, o_nsa_s, mod_s, g1, w_mg, wbf, wbn, wout, nb=nb_s, tt=TN)
    y_sample = _ffn_call(x1_s, mod_s, g2, wup, wdn, nb=nb_s, tt=TN)

    def token_major(rows_t, n_slots):
        nbat, _, toks = rows_t.shape
        return rows_t.reshape(nbat, n_slots, N_GROUPS, HEAD_DIM, toks).transpose(0, 4, 1, 2, 3)

    lf_p = lf[:, LOGF_LANE0:LOGF_LANE0 + N_HEADS]
    return (y_prompt, y_sample,
            token_major(frow_t, 2)[None],
            lf_p.reshape(1, B, T, N_HEADS),
            token_major(nrow_t, 4)[None],
            token_major(wrow_t[:, :, T - wlen:], 2)[None],
            frow_s.reshape(1, DB, TN, 2, N_GROUPS, HEAD_DIM),
            lf_s[:, LOGF_LANE0:LOGF_LANE0 + N_HEADS].reshape(1, DB, TN, N_HEADS),
            nrow_s.reshape(1, DB, TN, 4, N_GROUPS, HEAD_DIM),
            token_major(win_new_t, 2)[None])
```
